```python
import math
import jax, jax.numpy as jnp
from jax import lax
import numpy as np

D_MODEL = 2048
BATCH = 16
SEQ = 256
DEPTH = 2
DEC_BATCH = 8
DEC_SEQ = 4096
PAST_LEN = 256

GRID_W = 64
N_MIXERS = 2
N_DN = (DEPTH + 1) // 2
N_HY = DEPTH // 2
DN_HEADS = 16
DN_DK = 128
DN_DV = 128
DN_QK_W = DN_HEADS * DN_DK
DN_V_W = DN_HEADS * DN_DV
DN_IN_W = 2 * DN_QK_W + 2 * DN_V_W + 4 * DN_HEADS
DN_CHUNK = 64
SHORT_CONV = 3
HY_EMB = 33
HY_BANDS = (HY_EMB - 1) // 2
HY_FW = 64
HY_TARGET = 1e-2
HY_FAST = 0.3
HY_SLOW = 1.5
N_EXPERTS = 32
TOP_K = 4
D_FF = 2048
SWIGLU_LIMIT = 7.0
SWIGLU_ALPHA = 1.702
MOE_BLOCK = 256
RMS_EPS = 1e-6
POS_BASE = 10000.0

kernel_name = "hybrid_deltanet_hyena_moe_diffusion_step"


def _rmsnorm(x, g):
    x32 = x.astype(jnp.float32)
    y = x32 * lax.rsqrt(jnp.mean(x32 * x32, axis=-1, keepdims=True) + RMS_EPS)
    return (y * g.astype(jnp.float32)).astype(x.dtype)


def _modulation(cond, w_mod, b_mod):
    mod = jax.nn.silu(cond) @ w_mod + b_mod
    return jnp.split(mod[:, None, :], 6, axis=-1)


def _modulate(h, shift, scale):
    return h * (1 + scale) + shift


def _short_conv(x, w):
    T = x.shape[1]
    half = SHORT_CONV // 2
    xp = jnp.pad(x, ((0, 0), (half, half), (0, 0)))
    return sum(xp[:, j:j + T] * w[j] for j in range(SHORT_CONV))


def _grid_pos_embedding(T, dtype):
    rows = T // GRID_W
    row = jnp.repeat(jnp.arange(rows), GRID_W)
    col = jnp.tile(jnp.arange(GRID_W), rows)
    quarter = D_MODEL // 4
    omega = 1.0 / (POS_BASE ** (jnp.arange(quarter, dtype=jnp.float32) / quarter))

    def axis_emb(p):
        ang = p.astype(jnp.float32)[:, None] * omega[None]
        return jnp.concatenate([jnp.sin(ang), jnp.cos(ang)], axis=-1)

    return jnp.concatenate([axis_emb(row), axis_emb(col)], axis=-1).astype(dtype)


def _l2norm(x):
    return x * lax.rsqrt(jnp.sum(x * x, axis=-1, keepdims=True) + 1e-6)


def _gated_delta_chunked(q, k, v, beta, g, s0):
    B, T, H, _ = q.shape
    DV = v.shape[-1]
    C = DN_CHUNK
    n = T // C

    def chunks(a):
        a = a.reshape((B, n, C, H) + a.shape[3:])
        return jnp.moveaxis(a, (1, 3), (0, 2))

    qc, kc, vc, bc, gc = chunks(q), chunks(k), chunks(v), chunks(beta), chunks(g)
    gc = jnp.cumsum(gc, axis=-1)
    idx = jnp.arange(C)
    incl = idx[:, None] >= idx[None, :]
    decay = jnp.exp(jnp.where(incl, gc[..., :, None] - gc[..., None, :], -jnp.inf))
    kb = kc * bc[..., None]
    a_strict = jnp.where(idx[:, None] > idx[None, :],
                         jnp.einsum('nbhid,nbhjd->nbhij', kb, kc) * decay, 0.0)
    eye = jnp.eye(C, dtype=jnp.float32)
    rhs = jnp.concatenate([vc * bc[..., None], kb * jnp.exp(gc)[..., None]], axis=-1)
    sol = lax.linalg.triangular_solve(eye + a_strict, rhs, left_side=True, lower=True,
                                      unit_diagonal=True)
    u, w = sol[..., :DV], sol[..., DV:]
    qk = jnp.einsum('nbhid,nbhjd->nbhij', qc, kc) * decay
    q_dec = qc * jnp.exp(gc)[..., None]
    g_last = gc[..., -1]
    k_tail = kc * jnp.exp(g_last[..., None] - gc)[..., None]

    def step(S, inp):
        u_i, w_i, qk_i, qd_i, kt_i, gl_i = inp
        v_new = u_i - jnp.einsum('bhck,bhkv->bhcv', w_i, S)
        o_i = jnp.einsum('bhck,bhkv->bhcv', qd_i, S) + jnp.einsum('bhij,bhjv->bhiv', qk_i, v_new)
        S = S * jnp.exp(gl_i)[..., None, None] + jnp.einsum('bhck,bhcv->bhkv', kt_i, v_new)
        return S, o_i

    s_fin, o = lax.scan(step, s0, (u, w, qk, q_dec, k_tail, g_last))
    o = jnp.moveaxis(o, (0, 2), (1, 3)).reshape(B, T, H, DV)
    return o, s_fin


def _deltanet(h, s0_f, s0_b, w_in, conv_w, a_log, dt_bias, norm_g, w_out):
    B, T, _ = h.shape
    f32 = jnp.float32
    proj = h @ w_in
    qkv_w = 2 * DN_QK_W + DN_V_W
    qkv = jax.nn.silu(_short_conv(proj[..., :qkv_w], conv_w)).astype(f32)
    z = proj[..., qkv_w:qkv_w + DN_V_W].astype(f32).reshape(B, T, DN_HEADS, DN_DV)
    ba = proj[..., qkv_w + DN_V_W:].astype(f32).reshape(B, T, 2, 2, DN_HEADS)
    q = _l2norm(qkv[..., :DN_QK_W].reshape(B, T, DN_HEADS, DN_DK)) * (DN_DK ** -0.5)
    k = _l2norm(qkv[..., DN_QK_W:2 * DN_QK_W].reshape(B, T, DN_HEADS, DN_DK))
    v = qkv[..., 2 * DN_QK_W:].reshape(B, T, DN_HEADS, DN_DV)
    beta = jax.nn.sigmoid(ba[:, :, 0])
    g = -jnp.exp(a_log.astype(f32)) * jax.nn.softplus(ba[:, :, 1] + dt_bias.astype(f32))
    o_f, s_f = _gated_delta_chunked(q, k, v, beta[:, :, 0], g[:, :, 0], s0_f)
    rev = lambda a: jnp.flip(a, axis=1)
    o_b, s_b = _gated_delta_chunked(rev(q), rev(k), rev(v), rev(beta[:, :, 1]), rev(g[:, :, 1]), s0_b)
    o = o_f + rev(o_b)
    o = o * lax.rsqrt(jnp.mean(o * o, axis=-1, keepdims=True) + RMS_EPS) * norm_g.astype(f32)
    o = o * jax.nn.silu(z)
    return o.reshape(B, T, DN_V_W).astype(h.dtype) @ w_out, s_f, s_b


def _hyena_filter_freq(L, w1, b1, w2, b2, w3, b3, w4, freq):
    f32 = jnp.float32
    pos = jnp.arange(L, dtype=f32)
    t = pos / max(L - 1, 1)
    bands = jnp.linspace(1e-4, HY_BANDS - 1, HY_BANDS, dtype=f32)
    ang = (2.0 * math.pi / L) * pos[:, None] * bands[None]
    feats = jnp.concatenate([t[:, None], jnp.cos(ang), -jnp.sin(ang)], axis=-1)
    fr = freq.astype(f32)
    zf = jnp.sin(fr[0] * (feats @ w1.astype(f32) + b1.astype(f32)))
    zf = jnp.sin(fr[1] * (zf @ w2.astype(f32) + b2.astype(f32)))
    zf = jnp.sin(fr[2] * (zf @ w3.astype(f32) + b3.astype(f32)))
    filt = (zf @ w4.astype(f32)).reshape(L, 2, D_MODEL)
    deltas = jnp.abs(jnp.linspace(math.log(HY_TARGET) / HY_SLOW, math.log(HY_TARGET) / HY_FAST,
                                  D_MODEL, dtype=f32))
    filt = filt * jnp.exp(-t[:, None, None] * deltas[None, None])
    two_sided = jnp.concatenate([filt[:, 0], jnp.zeros((1, D_MODEL), f32), filt[:0:-1, 1]], axis=0)
    two_sided = two_sided * lax.rsqrt(jnp.sum(two_sided * two_sided, axis=0, keepdims=True) + 1e-6)
    return jnp.fft.rfft(two_sided, axis=0)


def _hyena(h, w_in, conv_w, w1, b1, w2, b2, w3, b3, w4, freq, bias, w_out):
    B, T, _ = h.shape
    u = _short_conv(h @ w_in, conv_w).astype(jnp.float32)
    x0, x1, v = jnp.split(u, 3, axis=-1)
    k_freq = _hyena_filter_freq(T, w1, b1, w2, b2, w3, b3, w4, freq)
    s = x1 * v
    conv = jnp.fft.irfft(jnp.fft.rfft(s, n=2 * T, axis=1) * k_freq[None], n=2 * T, axis=1)[:, :T]
    y = x0 * (conv + s * bias.astype(jnp.float32))
    return y.astype(h.dtype) @ w_out


def _moe(h, w_router, b_router, w_in, b_in, w_out, b_out):
    B, T, D = h.shape
    N = B * T
    A = N * TOP_K
    n_blocks = -(-A // MOE_BLOCK) + N_EXPERTS
    n_rows = n_blocks * MOE_BLOCK
    xt = h.reshape(N, D)
    logits = (xt @ w_router + b_router).astype(jnp.float32)
    top_logit, top_idx = lax.top_k(logits, TOP_K)
    gates = jax.nn.softmax(top_logit, axis=-1)
    e_flat = top_idx.reshape(A)
    order = jnp.argsort(e_flat)
    e_sorted = e_flat[order]
    tok_sorted = (order // TOP_K).astype(jnp.int32)
    gate_sorted = gates.reshape(A)[order]
    counts = jnp.bincount(e_flat, length=N_EXPERTS)
    starts = jnp.cumsum(counts) - counts
    padded = (counts + MOE_BLOCK - 1) // MOE_BLOCK * MOE_BLOCK
    pad_end = jnp.cumsum(padded)
    pad_start = pad_end - padded
    dest = pad_start[e_sorted] + jnp.arange(A) - starts[e_sorted]
    row_tok = jnp.full((n_rows,), N, jnp.int32).at[dest].set(tok_sorted)
    row_gate = jnp.zeros((n_rows,), jnp.float32).at[dest].set(gate_sorted)
    block_expert = jnp.minimum(
        jnp.searchsorted(pad_end, jnp.arange(n_blocks) * MOE_BLOCK, side='right'), N_EXPERTS - 1)
    x_rows = jnp.concatenate([xt, jnp.zeros((1, D), xt.dtype)], axis=0)[row_tok]
    x_rows = x_rows.reshape(n_blocks, MOE_BLOCK, D)

    def expert_block(args):
        xb, e = args
        gu = xb @ w_in[e] + b_in[e]
        gate = jnp.minimum(gu[:, :D_FF], SWIGLU_LIMIT)
        up = jnp.clip(gu[:, D_FF:], -SWIGLU_LIMIT, SWIGLU_LIMIT)
        return ((up + 1) * gate * jax.nn.sigmoid(SWIGLU_ALPHA * gate)) @ w_out[e] + b_out[e]

    y_rows = lax.map(expert_block, (x_rows, block_expert)).reshape(n_rows, D)
    y_rows = y_rows * row_gate[:, None].astype(y_rows.dtype)
    return jax.ops.segment_sum(y_rows, row_tok, num_segments=N + 1)[:N].reshape(B, T, D)


def setup_inputs(seed: int = 0) -> dict:
    key = jax.random.key(seed)
    ks = list(jax.random.split(key, 40))
    f32 = jnp.float32
    D = D_MODEL

    def nk():
        return ks.pop(0)

    def nrm(shape, scale):
        return jax.random.normal(nk(), shape, f32) * scale

    def gain(shape):
        return 1.0 + nrm(shape, 0.01)

    inp = {}
    inp["x_prompt"] = nrm((BATCH, SEQ, D), 1.0)
    inp["x_sample"] = nrm((DEC_BATCH, DEC_SEQ, D), 1.0)
    inp["state_delta"] = nrm((DEC_BATCH, N_DN, 2, DN_HEADS, DN_DK, DN_DV), 0.1)
    inp["c"] = nrm((DEC_BATCH, D), 1.0)
    inp["c_ctx"] = nrm((D,), 1.0)
    inp["w_mod"] = nrm((DEPTH, D, 6 * D), D ** -0.5)
    inp["b_mod"] = nrm((DEPTH, 6 * D), 0.01)
    inp["norm_mix"] = gain((DEPTH, D))
    inp["norm_ffn"] = gain((DEPTH, D))
    inp["norm_final"] = gain((D,))
    inp["dn_w_in"] = nrm((N_DN, D, DN_IN_W), D ** -0.5)
    inp["dn_conv"] = nrm((N_DN, SHORT_CONV, 2 * DN_QK_W + DN_V_W), SHORT_CONV ** -0.5)
    inp["dn_a_log"] = jnp.log(jax.random.uniform(nk(), (N_DN, 2, DN_HEADS), f32, 1.0, 16.0))
    dt = jnp.exp(jax.random.uniform(nk(), (N_DN, 2, DN_HEADS), f32, math.log(1e-3), math.log(1e-1)))
    inp["dn_dt_bias"] = dt + jnp.log(-jnp.expm1(-dt))
    inp["dn_norm"] = gain((N_DN, DN_DV))
    inp["dn_w_out"] = nrm((N_DN, DN_V_W, D), DN_V_W ** -0.5)
    inp["hy_w_in"] = nrm((N_HY, D, 3 * D), D ** -0.5)
    inp["hy_conv"] = nrm((N_HY, SHORT_CONV, 3 * D), SHORT_CONV ** -0.5)
    inp["hy_w1"] = nrm((N_HY, HY_EMB, HY_FW), HY_EMB ** -0.5)
    inp["hy_b1"] = nrm((N_HY, HY_FW), 0.1)
    inp["hy_w2"] = nrm((N_HY, HY_FW, HY_FW), HY_FW ** -0.5)
    inp["hy_b2"] = nrm((N_HY, HY_FW), 0.1)
    inp["hy_w3"] = nrm((N_HY, HY_FW, HY_FW), HY_FW ** -0.5)
    inp["hy_b3"] = nrm((N_HY, HY_FW), 0.1)
    inp["hy_w4"] = nrm((N_HY, HY_FW, 2 * D), HY_FW ** -0.5)
    inp["hy_freq"] = gain((N_HY, 3, HY_FW))
    inp["hy_bias"] = nrm((N_HY, D), 0.1)
    inp["hy_w_out"] = nrm((N_HY, D, D), D ** -0.5)
    inp["moe_w_router"] = nrm((DEPTH, D, N_EXPERTS), D ** -0.5)
    inp["moe_b_router"] = nrm((DEPTH, N_EXPERTS), 0.01)
    inp["moe_w_in"] = nrm((DEPTH, N_EXPERTS, D, 2 * D_FF), D ** -0.5)
    inp["moe_b_in"] = nrm((DEPTH, N_EXPERTS, 2 * D_FF), 0.01)
    inp["moe_w_out"] = nrm((DEPTH, N_EXPERTS, D_FF, D), D_FF ** -0.5)
    inp["moe_b_out"] = nrm((DEPTH, N_EXPERTS, D), 0.01)
    return inp


def reference(x_prompt, x_sample, state_delta, c, c_ctx, w_mod, b_mod, norm_mix, norm_ffn, norm_final,
              dn_w_in, dn_conv, dn_a_log, dn_dt_bias, dn_norm, dn_w_out,
              hy_w_in, hy_conv, hy_w1, hy_b1, hy_w2, hy_b2, hy_w3, hy_b3, hy_w4, hy_freq, hy_bias, hy_w_out,
              moe_w_router, moe_b_router, moe_w_in, moe_b_in, moe_w_out, moe_b_out):
    xp = x_prompt
    xs = x_sample + _grid_pos_embedding(x_sample.shape[1], x_sample.dtype)[None]
    new_states = []
    for l in range(DEPTH):
        i = l // N_MIXERS
        sh_mp, sc_mp, g_mp, sh_fp, sc_fp, g_fp = _modulation(c_ctx[None], w_mod[l], b_mod[l])
        sh_ms, sc_ms, g_ms, sh_fs, sc_fs, g_fs = _modulation(c, w_mod[l], b_mod[l])
        hp = _modulate(_rmsnorm(xp, norm_mix[l]), sh_mp, sc_mp)
        hs = _modulate(_rmsnorm(xs, norm_mix[l]), sh_ms, sc_ms)
        if l % N_MIXERS == 0:
            dn = (dn_w_in[i], dn_conv[i], dn_a_log[i], dn_dt_bias[i], dn_norm[i], dn_w_out[i])
            s_zero = jnp.zeros((xp.shape[0], DN_HEADS, DN_DK, DN_DV), jnp.float32)
            mp, s_f, s_b = _deltanet(hp, s_zero, s_zero, *dn)
            ms, _, _ = _deltanet(hs, state_delta[:, i, 0].astype(jnp.float32),
                                 state_delta[:, i, 1].astype(jnp.float32), *dn)
            new_states.append(jnp.stack([s_f, s_b], axis=1).astype(state_delta.dtype))
        else:
            hy = (hy_w_in[i], hy_conv[i], hy_w1[i], hy_b1[i], hy_w2[i], hy_b2[i], hy_w3[i], hy_b3[i],
                  hy_w4[i], hy_freq[i], hy_bias[i], hy_w_out[i])
            mp = _hyena(hp, *hy)
            ms = _hyena(hs, *hy)
        xp = xp + g_mp * mp
        xs = xs + g_ms * ms
        moe = (moe_w_router[l], moe_b_router[l], moe_w_in[l], moe_b_in[l], moe_w_out[l], moe_b_out[l])
        xp = xp + g_fp * _moe(_modulate(_rmsnorm(xp, norm_ffn[l]), sh_fp, sc_fp), *moe)
        xs = xs + g_fs * _moe(_modulate(_rmsnorm(xs, norm_ffn[l]), sh_fs, sc_fs), *moe)
    y_prompt = _rmsnorm(xp, norm_final)
    y_sample = _rmsnorm(xs, norm_final)
    new_state_delta = jnp.stack(new_states, axis=1)
    return (y_prompt, y_sample, new_state_delta)
```

```python
import functools
import math

import numpy as np
import jax
import jax.numpy as jnp
from jax import lax
from jax.experimental import pallas as pl
from jax.experimental.pallas import tpu as pltpu

F32, BF16, I32 = jnp.float32, jnp.bfloat16, jnp.int32

V7X_VMEM_LIMIT_BYTES = 56 * 2**20
LANES = 128
BF16_ROWS = 16

GRID_W = 64
DN_CHUNK = 64
HEAD_DIM = 128
TOP_K = 4
RMS_EPS = 1e-6
L2_EPS = 1e-6
POS_BASE = 10000.0
SWIGLU_LIMIT = 7.0
SWIGLU_ALPHA = 1.702
HY_TARGET = 1e-2
HY_FAST = 0.3
HY_SLOW = 1.5

HIGHEST = lax.Precision.HIGHEST
NT_DIMS = (((1,), (1,)), ((), ()))
TN_DIMS = (((0,), (0,)), ((), ()))


def _params(*sem):
    return pltpu.CompilerParams(dimension_semantics=sem, vmem_limit_bytes=V7X_VMEM_LIMIT_BYTES)


def _tile(n, pref):
    if n <= pref:
        return n
    t = pref - pref % LANES
    while n % t:
        t -= LANES
    return t


def _silu(x):
    return x * jax.nn.sigmoid(x)


def _softplus(x):
    return jnp.maximum(x, 0.0) + jnp.log1p(jnp.exp(-jnp.abs(x)))


def _split_bf16(x):
    hi = x.astype(BF16)
    lo = (x - hi.astype(F32)).astype(BF16)
    return hi, lo


def _dot3(a, b):
    ah, al = _split_bf16(a)
    bh, bl = _split_bf16(b)
    return (jnp.dot(ah, bh, preferred_element_type=F32)
            + (jnp.dot(ah, bl, preferred_element_type=F32) + jnp.dot(al, bh, preferred_element_type=F32)))


def _modulation_kernel(c_ref, w_ref, b_ref, o_ref):
    s = _silu(c_ref[...]).astype(BF16)
    o_ref[0] = jnp.dot(s, w_ref[0].astype(BF16), preferred_element_type=F32) + b_ref[0]


def _modulation(cond, w_mod, b_mod):
    R, D = cond.shape
    L, _, N = w_mod.shape
    tn = _tile(N, 1024)
    return pl.pallas_call(
        _modulation_kernel,
        grid=(L, N // tn),
        in_specs=[pl.BlockSpec((R, D), lambda l, j: (0, 0)),
                  pl.BlockSpec((1, D, tn), lambda l, j: (l, 0, j)),
                  pl.BlockSpec((1, 1, tn), lambda l, j: (l, 0, j))],
        out_specs=pl.BlockSpec((1, R, tn), lambda l, j: (l, 0, j)),
        out_shape=jax.ShapeDtypeStruct((L, R, N), F32),
        compiler_params=_params("parallel", "parallel"),
        name="modulation",
    )(cond, w_mod, b_mod.reshape(L, 1, N))


def _fused_mm_kernel(prologue, n_a, n_v, n_m, has_res, has_side, rm_ref, *refs):
    del rm_ref
    a_refs, refs = refs[:n_a], refs[n_a:]
    v_refs, refs = refs[:n_v], refs[n_v:]
    m_refs, refs = refs[:n_m], refs[n_m:]
    w_ref, refs = refs[0], refs[1:]
    if has_res:
        res_ref, gate_ref, refs = refs[0], refs[1], refs[2:]
    if has_side:
        sw_ref, refs = refs[0], refs[1:]
    o_ref, refs = refs[0], refs[1:]
    if has_side:
        so_ref, refs = refs[0], refs[1:]
    a_scr = refs[0]

    @pl.when(pl.program_id(1) == 0)
    def _():
        a = prologue([r[...] for r in a_refs], [r[...] for r in v_refs], [r[0, 0] for r in m_refs])
        a_scr[...] = a.astype(BF16)
        if has_side:
            so_ref[...] = jnp.dot(a_scr[...], sw_ref[...], preferred_element_type=F32)

    acc = jnp.dot(a_scr[...], w_ref[...], preferred_element_type=F32)
    if has_res:
        acc = res_ref[...] + gate_ref[0, 0] * acc
    o_ref[...] = acc.astype(o_ref.dtype)


def _fused_mm(prologue, a_ins, vec_ins, mod_ins, w, rowmap, *, mod4=None, res=None, gate_chunk=None,
              side_w=None, out_dtype=BF16, tm, tn):
    M, K = a_ins[0].shape
    N = w.shape[1]
    tn = _tile(N, tn)
    in_specs, args = [], []
    for a in a_ins:
        in_specs.append(pl.BlockSpec((tm, K), lambda i, j, rm: (i, 0)))
        args.append(a)
    for v in vec_ins:
        in_specs.append(pl.BlockSpec((1, K), lambda i, j, rm: (0, 0)))
        args.append(v.reshape(1, K))
    for c in mod_ins:
        in_specs.append(pl.BlockSpec((1, 1, 1, K), lambda i, j, rm, c=c: (rm[i], c, 0, 0)))
        args.append(mod4)
    in_specs.append(pl.BlockSpec((K, tn), lambda i, j, rm: (0, j)))
    args.append(w)
    if res is not None:
        in_specs.append(pl.BlockSpec((tm, tn), lambda i, j, rm: (i, j)))
        args.append(res)
        in_specs.append(pl.BlockSpec((1, 1, 1, tn), lambda i, j, rm, c=gate_chunk: (rm[i], c, 0, j)))
        args.append(mod4)
    out_specs = [pl.BlockSpec((tm, tn), lambda i, j, rm: (i, j))]
    out_shape = [jax.ShapeDtypeStruct((M, N), out_dtype)]
    if side_w is not None:
        ns = side_w.shape[1]
        in_specs.append(pl.BlockSpec((K, ns), lambda i, j, rm: (0, 0)))
        args.append(side_w)
        out_specs.append(pl.BlockSpec((tm, ns), lambda i, j, rm: (i, 0)))
        out_shape.append(jax.ShapeDtypeStruct((M, ns), F32))
    kern = functools.partial(_fused_mm_kernel, prologue, len(a_ins), len(vec_ins), len(mod_ins),
                             res is not None, side_w is not None)
    outs = pl.pallas_call(
        kern,
        grid_spec=pltpu.PrefetchScalarGridSpec(
            num_scalar_prefetch=1, grid=(M // tm, N // tn), in_specs=in_specs, out_specs=out_specs,
            scratch_shapes=[pltpu.VMEM((tm, K), BF16)]),
        out_shape=out_shape,
        compiler_params=_params("parallel", "arbitrary"),
        name="fused_mm",
    )(rowmap, *args)
    return outs if side_w is not None else outs[0]


def _norm_mod_prologue(a, v, m):
    x, g, (shift, scale) = a[0], v[0], m
    y = x * lax.rsqrt(jnp.mean(x * x, axis=-1, keepdims=True) + RMS_EPS) * g
    return y * (1.0 + scale) + shift


def _identity_prologue(a, v, m):
    return a[0]


def _hyena_gate_prologue(a, v, m):
    x0, conv, s = (t.astype(F32) for t in a)
    return x0 * (conv + s * v[0])


def _conv3(x, prev_row, next_row, w, rid):
    t = x.shape[0]
    xm = jnp.where(rid == 0, prev_row, pltpu.roll(x, 1, 0))
    xp = jnp.where(rid == t - 1, next_row, pltpu.roll(x, t - 1, 0))
    return xm * w[0:1] + x * w[1:2] + xp * w[2:3]


def _halo_specs(tT, width, col_block):
    hb = tT // BF16_ROWS
    return [
        pl.BlockSpec((tT, width), lambda i, hp, hn: (i, col_block)),
        pl.BlockSpec((BF16_ROWS, width), lambda i, hp, hn: (jnp.maximum(i * hb - 1, 0), col_block)),
        pl.BlockSpec((BF16_ROWS, width), lambda i, hp, hn: ((i + 1) * hb * hn[i], col_block)),
    ]


def _dn_act_kernel(n_heads, hp_ref, hn_ref, x_ref, xp_ref, xn_ref, ba_ref, cw_ref, al_ref, dtb_ref,
                   qkv_ref, bg_ref):
    i = pl.program_id(0)
    hp = hp_ref[i].astype(F32)
    hn = hn_ref[i].astype(F32)
    tT = x_ref.shape[0]
    rid = lax.broadcasted_iota(I32, (tT, LANES), 0)
    for cb in range(3 * n_heads):
        sl = slice(cb * LANES, (cb + 1) * LANES)
        x = x_ref[:, sl].astype(F32)
        pr = xp_ref[:, sl].astype(F32)[BF16_ROWS - 1:BF16_ROWS] * hp
        nx = xn_ref[:, sl].astype(F32)[0:1] * hn
        y = _silu(_conv3(x, pr, nx, cw_ref[:, sl], rid))
        if cb < 2 * n_heads:
            y = y * lax.rsqrt(jnp.sum(y * y, axis=-1, keepdims=True) + L2_EPS)
            if cb < n_heads:
                y = y * (HEAD_DIM ** -0.5)
        qkv_ref[:, sl] = y.astype(BF16)

    ba = ba_ref[...]
    lane = lax.broadcasted_iota(I32, (tT, LANES), 1)
    beta = jax.nn.sigmoid(ba)
    g = -jnp.exp(al_ref[...]) * _softplus(ba + dtb_ref[...])
    r = lax.broadcasted_iota(I32, (tT, tT), 0)
    c = lax.broadcasted_iota(I32, (tT, tT), 1)
    same = jnp.right_shift(r, 6) == jnp.right_shift(c, 6)
    l_pre = jnp.where(same & (c <= r), 1.0, 0.0).astype(F32)
    l_suf = jnp.where(same & (c >= r), 1.0, 0.0).astype(F32)
    g_pre = jnp.dot(l_pre, g, precision=HIGHEST, preferred_element_type=F32)
    g_suf = jnp.dot(l_suf, g, precision=HIGHEST, preferred_element_type=F32)
    gc = jnp.where(lane < 3 * n_heads, g_pre, g_suf)
    bg_ref[...] = jnp.where(lane < 2 * n_heads, beta, gc)


def _dn_act(pm, ba, conv_w, a_log, dt_bias, has_prev, has_next, tT):
    M = pm.shape[0]
    cw = conv_w.shape[1]
    n_heads = cw // (3 * HEAD_DIM)
    pad = lambda v: jnp.zeros((1, LANES), F32).at[0, 2 * n_heads:4 * n_heads].set(v.reshape(-1).astype(F32))
    vec = pl.BlockSpec((1, LANES), lambda i, hp, hn: (0, 0))
    return pl.pallas_call(
        functools.partial(_dn_act_kernel, n_heads),
        grid_spec=pltpu.PrefetchScalarGridSpec(
            num_scalar_prefetch=2, grid=(M // tT,),
            in_specs=_halo_specs(tT, cw, 0) + [
                pl.BlockSpec((tT, LANES), lambda i, hp, hn: (i, 0)),
                pl.BlockSpec((3, cw), lambda i, hp, hn: (0, 0)), vec, vec],
            out_specs=[pl.BlockSpec((tT, cw), lambda i, hp, hn: (i, 0)),
                       pl.BlockSpec((tT, LANES), lambda i, hp, hn: (i, 0))]),
        out_shape=[jax.ShapeDtypeStruct((M, cw), BF16), jax.ShapeDtypeStruct((M, LANES), F32)],
        compiler_params=_params("parallel"),
        name="dn_act",
    )(has_prev, has_next, pm, pm, pm, ba, conv_w, pad(a_log), pad(dt_bias))


def _merge_masks(ri, ci, n, lower):
    masks = []
    s = 0
    while (1 << s) < n:
        same = jnp.right_shift(ri, s + 1) == jnp.right_shift(ci, s + 1)
        hi_r = jnp.bitwise_and(jnp.right_shift(ri, s), 1)
        hi_c = jnp.bitwise_and(jnp.right_shift(ci, s), 1)
        off = (hi_r == 1) & (hi_c == 0) if lower else (hi_r == 0) & (hi_c == 1)
        masks.append(same & off)
        s += 1
    return masks


def _unit_tri_inverse(a, eye, masks):
    t = eye - jnp.where(masks[0], a, 0.0)
    for m in masks[1:]:
        t = t - _dot3(_dot3(t, jnp.where(m, a, 0.0)), t)
    return t


def _delta_kernel(hb, n_chunks, has_s0, *refs):
    if has_s0:
        q_ref, k_ref, v_ref, z_ref, cols_ref, gct_ref, ng_ref, s0_ref, _alias, o_ref, sfin_ref = refs[:11]
        scr = refs[11:]
    else:
        q_ref, k_ref, v_ref, z_ref, cols_ref, gct_ref, ng_ref, o_ref, sfin_ref = refs[:9]
        scr = refs[9:]
    u_scr, wq_scr, qk_scr, kt_scr, egl_scr, s_scr, o_scr = scr
    C = DN_CHUNK
    n = n_chunks
    ri = lax.broadcasted_iota(I32, (C, C), 0)
    ci = lax.broadcasted_iota(I32, (C, C), 1)
    eye = (ri == ci).astype(F32)
    incl = (ri >= ci, ri <= ci)

    def prep(c, carry):
        rows = pl.ds(pl.multiple_of(c * C, C), C)
        masks = (_merge_masks(ri, ci, C, True), _merge_masks(ri, ci, C, False))
        for hh in range(hb):
            ls = slice(hh * HEAD_DIM, (hh + 1) * HEAD_DIM)
            qb, kb16, vb = q_ref[0, rows, ls], k_ref[0, rows, ls], v_ref[0, rows, ls]
            qf, kf, vf = qb.astype(F32), kb16.astype(F32), vb.astype(F32)
            qkt = lax.dot_general(qb, kb16, NT_DIMS, preferred_element_type=F32)
            for d in range(2):
                ch = hh * 2 + d
                bcol = cols_ref[0, 0, rows, hh * 4 + d:hh * 4 + d + 1]
                gcol = cols_ref[0, 0, rows, hh * 4 + 2 + d:hh * 4 + 3 + d]
                grow = gct_ref[0, hh, d, pl.ds(c, 1), :]
                gl = gcol[C - 1:C, :] if d == 0 else gcol[0:1, :]
                dm = jnp.where(incl[d], jnp.exp(jnp.where(incl[d], gcol - grow, 0.0)), 0.0)
                kbeta = kf * bcol
                a = lax.dot_general(kbeta.astype(BF16), kb16, NT_DIMS, preferred_element_type=F32) * dm
                tm_ = _unit_tri_inverse(a, eye, masks[d])
                eg = jnp.exp(gcol)
                rhs = jnp.concatenate([vf * bcol, kbeta * eg], axis=1)
                sol = _dot3(tm_, rhs)
                u_scr[ch, rows, :] = sol[:, :HEAD_DIM]
                wq_scr[ch, pl.ds(pl.multiple_of(c * 2 * C, 2 * C), C), :] = sol[:, HEAD_DIM:].astype(BF16)
                wq_scr[ch, pl.ds(pl.multiple_of(c * 2 * C + C, C), C), :] = (qf * eg).astype(BF16)
                qk_scr[ch, rows, :] = (qkt * dm).astype(BF16)
                kt_scr[ch, rows, :] = (kf * jnp.exp(gl - gcol)).astype(BF16)
                egl_scr[pl.ds(ch * n + c, 1), :] = jnp.broadcast_to(jnp.exp(gl), (1, LANES))
        return carry

    lax.fori_loop(0, n, prep, 0)

    for hh in range(hb):
        for d in range(2):
            s_scr[hh * 2 + d] = s0_ref[0, d, hh] if has_s0 else jnp.zeros((HEAD_DIM, HEAD_DIM), F32)

    def scan(accumulate, it, carry):
        for hh in range(hb):
            for d in range(2):
                ch = hh * 2 + d
                c = it if d == 0 else n - 1 - it
                rows = pl.ds(pl.multiple_of(c * C, C), C)
                s = s_scr[ch]
                sb = s.astype(BF16)
                r = jnp.dot(wq_scr[ch, pl.ds(pl.multiple_of(c * 2 * C, 2 * C), 2 * C), :], sb,
                            preferred_element_type=F32)
                vn = (u_scr[ch, rows, :] - r[:C]).astype(BF16)
                o = r[C:] + jnp.dot(qk_scr[ch, rows, :], vn, preferred_element_type=F32)
                s_scr[ch] = (s * egl_scr[pl.ds(ch * n + c, 1), :]
                             + lax.dot_general(kt_scr[ch, rows, :], vn, TN_DIMS, preferred_element_type=F32))
                if accumulate:
                    o_scr[hh, rows, :] += o
                else:
                    o_scr[hh, rows, :] = o
        return carry

    lax.fori_loop(0, n // 2, functools.partial(scan, False), 0)
    lax.fori_loop(n // 2, n, functools.partial(scan, True), 0)

    for hh in range(hb):
        for d in range(2):
            sfin_ref[0, d, hh] = s_scr[hh * 2 + d]

    T = n * C
    tr = min(T, 256)

    def gate(b, carry):
        rows = pl.ds(pl.multiple_of(b * tr, tr), tr)
        for hh in range(hb):
            ls = slice(hh * HEAD_DIM, (hh + 1) * HEAD_DIM)
            o = o_scr[hh, rows, :]
            o = o * lax.rsqrt(jnp.mean(o * o, axis=-1, keepdims=True) + RMS_EPS) * ng_ref[...]
            o_ref[0, rows, ls] = (o * _silu(z_ref[0, rows, ls].astype(F32))).astype(BF16)
        return carry

    lax.fori_loop(0, T // tr, gate, 0)


def _delta(qkv, pm, cols, gct, norm_g, s0, alias, *, n_batch, batch_off, T, hb):
    M, cw = qkv.shape
    H = cw // (3 * HEAD_DIM)
    n = T // DN_CHUNK
    W = hb * HEAD_DIM
    nb = H // hb
    qkv3 = qkv.reshape(M // T, T, cw)
    pm3 = pm.reshape(M // T, T, pm.shape[1])
    blk = lambda off: pl.BlockSpec((1, T, W), lambda b, j, off=off: (b + batch_off, 0, off * nb + j))
    in_specs = [blk(0), blk(1), blk(2), blk(3),
                pl.BlockSpec((1, 1, T, 4 * hb), lambda b, j: (b, j, 0, 0)),
                pl.BlockSpec((1, hb, 2, n, DN_CHUNK), lambda b, j: (b, j, 0, 0, 0)),
                pl.BlockSpec((1, HEAD_DIM), lambda b, j: (0, 0))]
    args = [qkv3, qkv3, qkv3, pm3, cols, gct, norm_g.reshape(1, HEAD_DIM)]
    aliases = {}
    if s0 is not None:
        in_specs.append(pl.BlockSpec((1, 2, hb, HEAD_DIM, HEAD_DIM), lambda b, j: (b, 0, j, 0, 0)))
        in_specs.append(pl.BlockSpec(memory_space=pl.ANY))
        args += [s0, alias.reshape(M // T, T, H * HEAD_DIM)]
        aliases = {len(args) - 1: 0}
    nch = 2 * hb
    out, sfin = pl.pallas_call(
        functools.partial(_delta_kernel, hb, n, s0 is not None),
        grid=(n_batch, nb),
        in_specs=in_specs,
        out_specs=[pl.BlockSpec((1, T, W), lambda b, j: (b + batch_off, 0, j)),
                   pl.BlockSpec((1, 2, hb, HEAD_DIM, HEAD_DIM), lambda b, j: (b, 0, j, 0, 0))],
        out_shape=[jax.ShapeDtypeStruct((M // T, T, H * HEAD_DIM), BF16),
                   jax.ShapeDtypeStruct((n_batch, 2, H, HEAD_DIM, HEAD_DIM), F32)],
        scratch_shapes=[pltpu.VMEM((nch, T, HEAD_DIM), F32),
                        pltpu.VMEM((nch, 2 * T, HEAD_DIM), BF16),
                        pltpu.VMEM((nch, T, DN_CHUNK), BF16),
                        pltpu.VMEM((nch, T, HEAD_DIM), BF16),
                        pltpu.VMEM((nch * n, LANES), F32),
                        pltpu.VMEM((nch, HEAD_DIM, HEAD_DIM), F32),
                        pltpu.VMEM((hb, T, HEAD_DIM), F32)],
        input_output_aliases=aliases,
        compiler_params=_params("parallel", "parallel"),
        name="delta_rule",
    )(*args)
    return out.reshape(M, H * HEAD_DIM), sfin


def _delta_side_inputs(bg, t0, n_batch, T, H, hb):
    b5 = bg[t0:t0 + n_batch * T, :4 * H].reshape(n_batch, T, 2, 2, H)
    cols = jnp.transpose(b5, (0, 4, 1, 2, 3)).reshape(n_batch, H // hb, hb, T, 4)
    cols = jnp.transpose(cols, (0, 1, 3, 2, 4)).reshape(n_batch, H // hb, T, 4 * hb)
    gct = jnp.transpose(b5[:, :, 1], (0, 3, 2, 1)).reshape(n_batch, H, 2, T // DN_CHUNK, DN_CHUNK)
    return cols, gct


def _hy_pre_kernel(D, hp_ref, hn_ref, x_ref, xp_ref, xn_ref, cw_ref, x0_ref, s_ref):
    i = pl.program_id(0)
    hp = hp_ref[i].astype(F32)
    hn = hn_ref[i].astype(F32)
    tT = x_ref.shape[0]
    W = min(D, 2 * LANES)
    rid = lax.broadcasted_iota(I32, (tT, W), 0)

    def conv(cb, part):
        sl = slice(part * D + cb * W, part * D + (cb + 1) * W)
        x = x_ref[:, sl].astype(F32)
        pr = xp_ref[:, sl].astype(F32)[BF16_ROWS - 1:BF16_ROWS] * hp
        nx = xn_ref[:, sl].astype(F32)[0:1] * hn
        return _conv3(x, pr, nx, cw_ref[:, sl], rid)

    for cb in range(D // W):
        sl = slice(cb * W, (cb + 1) * W)
        x0_ref[:, sl] = conv(cb, 0).astype(BF16)
        s_ref[:, sl] = (conv(cb, 1) * conv(cb, 2)).astype(BF16)


def _hy_pre(u, conv_w, has_prev, has_next, tT):
    M, W3 = u.shape
    D = W3 // 3
    out = pl.BlockSpec((tT, D), lambda i, hp, hn: (i, 0))
    return pl.pallas_call(
        functools.partial(_hy_pre_kernel, D),
        grid_spec=pltpu.PrefetchScalarGridSpec(
            num_scalar_prefetch=2, grid=(M // tT,),
            in_specs=_halo_specs(tT, W3, 0) + [pl.BlockSpec((3, W3), lambda i, hp, hn: (0, 0))],
            out_specs=[out, out]),
        out_shape=[jax.ShapeDtypeStruct((M, D), BF16)] * 2,
        compiler_params=_params("parallel"),
        name="hy_pre",
    )(has_prev, has_next, u, u, u, conv_w)


def _bmm_kernel(a_ref, b_ref, o_ref, acc_ref):
    k = pl.program_id(3)
    part = jnp.dot(a_ref[...], b_ref[0], preferred_element_type=F32)

    @pl.when(k == 0)
    def _():
        acc_ref[...] = part

    @pl.when(k > 0)
    def _():
        acc_ref[...] += part

    @pl.when(k == pl.num_programs(3) - 1)
    def _():
        o_ref[0] = acc_ref[...].astype(o_ref.dtype)


def _bmm(a, b, b_batch_off=0, n_batch=None, out_dtype=BF16, tm=1024, tn=1024, tk=2048):
    M, K = a.shape
    N = b.shape[2]
    nb = b.shape[0] if n_batch is None else n_batch
    tm, tn, tk = _tile(M, tm), _tile(N, tn), _tile(K, tk)
    return pl.pallas_call(
        _bmm_kernel,
        grid=(nb, M // tm, N // tn, K // tk),
        in_specs=[pl.BlockSpec((tm, tk), lambda i, m, n, k: (m, k)),
                  pl.BlockSpec((1, tk, tn), lambda i, m, n, k: (i + b_batch_off, k, n))],
        out_specs=pl.BlockSpec((1, tm, tn), lambda i, m, n, k: (i, m, n)),
        out_shape=jax.ShapeDtypeStruct((nb, M, N), out_dtype),
        scratch_shapes=[pltpu.VMEM((tm, tn), F32)],
        compiler_params=_params("parallel", "parallel", "parallel", "arbitrary"),
        name="dft_mm",
    )(a, b)


def _spec_prod_kernel(inv_n, s_ref, k_ref, y_ref):
    sc, ss = s_ref[0, 0].astype(F32), s_ref[0, 1].astype(F32)
    kc, ks = k_ref[0], k_ref[1]
    first = (lax.broadcasted_iota(I32, sc.shape, 0) == 0) & (pl.program_id(1) == 0)
    y_ref[0, 0] = (jnp.where(first, sc * kc, 2.0 * (sc * kc - ss * ks)) * inv_n).astype(y_ref.dtype)
    y_ref[0, 1] = (jnp.where(first, ss * ks, 2.0 * (sc * ks + ss * kc)) * inv_n).astype(y_ref.dtype)


def _spec_prod(sf, kf):
    B, _, T, D = sf.shape
    tr, tc = _tile(T, 256), _tile(D, 1024)
    return pl.pallas_call(
        functools.partial(_spec_prod_kernel, 1.0 / (2 * T)),
        grid=(B, T // tr, D // tc),
        in_specs=[pl.BlockSpec((1, 2, tr, tc), lambda b, i, j: (b, 0, i, j)),
                  pl.BlockSpec((2, tr, tc), lambda b, i, j: (0, i, j))],
        out_specs=pl.BlockSpec((1, 2, tr, tc), lambda b, i, j: (b, 0, i, j)),
        out_shape=jax.ShapeDtypeStruct(sf.shape, BF16),
        compiler_params=_params("parallel", "parallel", "parallel"),
        name="spec_prod",
    )(sf, kf)


def _dft_matrix(T):
    k = jnp.arange(T, dtype=I32)[:, None]
    t = jnp.arange(T, dtype=I32)[None, :]
    ang = ((k * t) % (2 * T)).astype(F32) * (math.pi / T)
    nyq = jnp.where(t % 2 == 0, 1.0, -1.0).astype(F32)
    sin = jnp.where(k == 0, nyq, jnp.sin(ang))
    return jnp.concatenate([jnp.cos(ang), sin], axis=0).astype(BF16)


def _hyena_filter_taps(L, w1, b1, w2, b2, w3, b3, w4, freq):
    D = w4.shape[1] // 2
    n_bands = (w1.shape[0] - 1) // 2
    pos = jnp.arange(L, dtype=F32)
    t = pos / max(L - 1, 1)
    bands = jnp.linspace(1e-4, n_bands - 1, n_bands, dtype=F32)
    ang = (2.0 * math.pi / L) * pos[:, None] * bands[None]
    feats = jnp.concatenate([t[:, None], jnp.cos(ang), -jnp.sin(ang)], axis=-1)
    zf = jnp.sin(freq[0] * (feats @ w1 + b1))
    zf = jnp.sin(freq[1] * (zf @ w2 + b2))
    zf = jnp.sin(freq[2] * (zf @ w3 + b3))
    filt = (zf @ w4).reshape(L, 2, D)
    deltas = jnp.abs(jnp.linspace(math.log(HY_TARGET) / HY_SLOW, math.log(HY_TARGET) / HY_FAST, D, dtype=F32))
    filt = filt * jnp.exp(-t[:, None, None] * deltas[None, None])
    f = filt[:, 0]
    b = filt[:, 1].at[0].set(0.0)
    scale = lax.rsqrt(jnp.sum(f * f, axis=0) + jnp.sum(b * b, axis=0) + 1e-6)
    return f * scale, b * scale


def _hyena_conv(s, n_batch, t0, T, taps):
    M, D = s.shape
    f, b = taps
    fwd = _dft_matrix(T)
    kf = _bmm(fwd, jnp.concatenate([f, b], axis=1).astype(BF16)[None], out_dtype=F32)[0]
    p, q = kf[:, :D].reshape(2, T, D), kf[:, D:].reshape(2, T, D)
    first = (jnp.arange(T) == 0)[:, None]
    kspec = jnp.stack([p[0] + q[0], jnp.where(first, p[1] + q[1], p[1] - q[1])])
    s3 = s.reshape(M // T, T, D)
    sf = _bmm(fwd, s3, b_batch_off=t0 // T, n_batch=n_batch)
    y = _spec_prod(sf.reshape(n_batch, 2, T, D), kspec).reshape(n_batch, 2 * T, D)
    conv = _bmm(fwd.T, y)
    return conv.reshape(n_batch * T, D)


def _router_kernel(n_experts, rm_ref, x_ref, g_ref, sh_ref, sc_ref, wr_ref, br_ref, h_ref, ti_ref, tg_ref):
    del rm_ref
    h = _norm_mod_prologue([x_ref[...]], [g_ref[...]], (sh_ref[0, 0], sc_ref[0, 0]))
    h_ref[...] = h.astype(BF16)
    logits = jnp.dot(h, wr_ref[...], precision=HIGHEST, preferred_element_type=F32) + br_ref[...]
    lane = lax.broadcasted_iota(I32, logits.shape, 1)
    lane_f = lane.astype(F32)
    neg = jnp.float32(-jnp.inf)
    l = jnp.where(lane < n_experts, logits, neg)
    ti = jnp.zeros(logits.shape, I32)
    tl = jnp.full(logits.shape, neg, F32)
    for r in range(TOP_K):
        m = jnp.max(l, axis=-1, keepdims=True)
        idx = jnp.min(jnp.where(l == m, lane_f, float(LANES)), axis=-1, keepdims=True).astype(I32)
        ti = jnp.where(lane == r, idx, ti)
        tl = jnp.where(lane == r, m, tl)
        l = jnp.where(lane == idx, neg, l)
    e = jnp.exp(tl - jnp.max(tl, axis=-1, keepdims=True))
    ti_ref[...] = ti
    tg_ref[...] = e / jnp.sum(e, axis=-1, keepdims=True)


def _router(x, g, mod4, rowmap, w_router, b_router, tm):
    M, D = x.shape
    E = w_router.shape[1]
    wr = jnp.zeros((D, LANES), F32).at[:, :E].set(w_router)
    br = jnp.zeros((1, LANES), F32).at[0, :E].set(b_router)
    row = lambda c: pl.BlockSpec((1, 1, 1, D), lambda i, rm, c=c: (rm[i], c, 0, 0))
    tile = lambda w: pl.BlockSpec((tm, w), lambda i, rm: (i, 0))
    return pl.pallas_call(
        functools.partial(_router_kernel, E),
        grid_spec=pltpu.PrefetchScalarGridSpec(
            num_scalar_prefetch=1, grid=(M // tm,),
            in_specs=[tile(D), pl.BlockSpec((1, D), lambda i, rm: (0, 0)), row(3), row(4),
                      pl.BlockSpec((D, LANES), lambda i, rm: (0, 0)),
                      pl.BlockSpec((1, LANES), lambda i, rm: (0, 0))],
            out_specs=[tile(D), tile(LANES), tile(LANES)]),
        out_shape=[jax.ShapeDtypeStruct((M, D), BF16), jax.ShapeDtypeStruct((M, LANES), I32),
                   jax.ShapeDtypeStruct((M, LANES), F32)],
        compiler_params=_params("parallel"),
        name="router",
    )(rowmap, x, g.reshape(1, D), mod4, mod4, wr, br)


def _moe_kernel(te_ref, nv_ref, x_ref, wg_ref, wu_ref, bg_ref, bu_ref, wo_ref, bo_ref, o_ref, acc_ref):
    del te_ref
    t, f = pl.program_id(0), pl.program_id(1)
    last = pl.num_programs(1) - 1
    valid = t < nv_ref[0]

    @pl.when(valid)
    def _():
        x = x_ref[...]
        g = jnp.dot(x, wg_ref[0], preferred_element_type=F32) + bg_ref[0]
        u = jnp.dot(x, wu_ref[0], preferred_element_type=F32) + bu_ref[0]
        g = jnp.minimum(g, SWIGLU_LIMIT)
        u = jnp.clip(u, -SWIGLU_LIMIT, SWIGLU_LIMIT)
        h = ((u + 1.0) * g * jax.nn.sigmoid(SWIGLU_ALPHA * g)).astype(BF16)
        part = jnp.dot(h, wo_ref[0], preferred_element_type=F32)

        @pl.when(f == 0)
        def _():
            acc_ref[...] = part + bo_ref[0]

        @pl.when(f > 0)
        def _():
            acc_ref[...] += part

        @pl.when(f == last)
        def _():
            o_ref[...] = acc_ref[...].astype(o_ref.dtype)

    @pl.when(jnp.logical_not(valid) & (f == last))
    def _():
        o_ref[...] = jnp.zeros(o_ref.shape, o_ref.dtype)


def _moe_experts(x_rows, tile_expert, n_valid, w_in, b_in, w_out, b_out, tm, tf):
    R, D = x_rows.shape
    E, _, F2 = w_in.shape
    F = F2 // 2
    tf = _tile(F, tf)
    nf = F // tf

    def fi(t, f, nv):
        return jnp.where(t < nv[0], f, nf - 1)

    def ti(t, nv):
        return jnp.minimum(t, jnp.maximum(nv[0] - 1, 0))

    return pl.pallas_call(
        _moe_kernel,
        grid_spec=pltpu.PrefetchScalarGridSpec(
            num_scalar_prefetch=2, grid=(R // tm, nf),
            in_specs=[pl.BlockSpec((tm, D), lambda t, f, te, nv: (ti(t, nv), 0)),
                      pl.BlockSpec((1, D, tf), lambda t, f, te, nv: (te[t], 0, fi(t, f, nv))),
                      pl.BlockSpec((1, D, tf), lambda t, f, te, nv: (te[t], 0, nf + fi(t, f, nv))),
                      pl.BlockSpec((1, 1, tf), lambda t, f, te, nv: (te[t], 0, fi(t, f, nv))),
                      pl.BlockSpec((1, 1, tf), lambda t, f, te, nv: (te[t], 0, nf + fi(t, f, nv))),
                      pl.BlockSpec((1, tf, D), lambda t, f, te, nv: (te[t], fi(t, f, nv), 0)),
                      pl.BlockSpec((1, 1, D), lambda t, f, te, nv: (te[t], 0, 0))],
            out_specs=pl.BlockSpec((tm, D), lambda t, f, te, nv: (t, 0)),
            scratch_shapes=[pltpu.VMEM((tm, D), F32)]),
        out_shape=jax.ShapeDtypeStruct((R, D), BF16),
        compiler_params=_params("parallel", "arbitrary"),
        name="moe_experts",
    )(tile_expert, n_valid, x_rows, w_in, w_in, b_in.reshape(E, 1, F2), b_in.reshape(E, 1, F2),
      w_out, b_out.reshape(E, 1, D))


def _moe_layer(x, norm_g, mod4, rowmap, tm_tok, w_router, b_router, w_in, b_in, w_out, b_out, tm=512, tf=512):
    M, D = x.shape
    E = w_router.shape[1]
    h, ti, tg = _router(x, norm_g, mod4, rowmap, w_router, b_router, tm_tok)
    top_idx, gates = ti[:, :TOP_K], tg[:, :TOP_K]
    A = M * TOP_K
    n_tiles = -(-A // tm) + E
    e_flat = top_idx.reshape(A)
    order = jnp.argsort(e_flat, stable=True).astype(I32)
    e_sorted = e_flat[order]
    counts = jnp.sum((e_flat[:, None] == jnp.arange(E, dtype=I32)[None]).astype(I32), axis=0)
    starts = jnp.cumsum(counts) - counts
    ptiles = (counts + tm - 1) // tm
    pend = jnp.cumsum(ptiles)
    pstart = pend - ptiles
    n_valid = pend[-1]
    tix = jnp.minimum(jnp.arange(n_tiles, dtype=I32), jnp.maximum(n_valid - 1, 0))
    tile_expert = jnp.minimum(jnp.searchsorted(pend, tix, side='right'), E - 1).astype(I32)
    rows = jnp.arange(n_tiles * tm, dtype=I32)
    e_row = tile_expert[rows // tm]
    off = rows - pstart[e_row] * tm
    src = order[jnp.clip(starts[e_row] + off, 0, A - 1)]
    row_tok = jnp.where((off < counts[e_row]) & (rows // tm < n_valid), src // TOP_K, 0)
    x_rows = h[row_tok]
    y_rows = _moe_experts(x_rows, tile_expert, n_valid.reshape(1).astype(I32),
                          w_in.astype(BF16), b_in, w_out.astype(BF16), b_out, tm, tf)
    dest_sorted = pstart[e_sorted] * tm + jnp.arange(A, dtype=I32) - starts[e_sorted]
    dest = jnp.zeros((A,), I32).at[order].set(dest_sorted).reshape(M, TOP_K)
    y = jnp.zeros((M, D), F32)
    for k in range(TOP_K):
        y = y + gates[:, k:k + 1] * y_rows[dest[:, k]].astype(F32)
    return y


def _final_norm_kernel(x_ref, g_ref, o_ref):
    x = x_ref[...]
    o_ref[...] = x * lax.rsqrt(jnp.mean(x * x, axis=-1, keepdims=True) + RMS_EPS) * g_ref[...]


def _final_norm(x, g, t0, rows, tm):
    D = x.shape[1]
    return pl.pallas_call(
        _final_norm_kernel,
        grid=(rows // tm,),
        in_specs=[pl.BlockSpec((tm, D), lambda i: (i + t0 // tm, 0)), pl.BlockSpec((1, D), lambda i: (0, 0))],
        out_specs=pl.BlockSpec((tm, D), lambda i: (i, 0)),
        out_shape=jax.ShapeDtypeStruct((rows, D), F32),
        compiler_params=_params("parallel"),
        name="final_norm",
    )(x, g.reshape(1, D))


def _grid_pos_embedding(T, D):
    rows = T // GRID_W
    row = jnp.repeat(jnp.arange(rows), GRID_W)
    col = jnp.tile(jnp.arange(GRID_W), rows)
    quarter = D // 4
    omega = 1.0 / (POS_BASE ** (jnp.arange(quarter, dtype=F32) / quarter))

    def axis_emb(p):
        ang = p.astype(F32)[:, None] * omega[None]
        return jnp.concatenate([jnp.sin(ang), jnp.cos(ang)], axis=-1)

    return jnp.concatenate([axis_emb(row), axis_emb(col)], axis=-1)


def kernel(x_prompt, x_sample, state_delta, c, c_ctx, w_mod, b_mod, norm_mix, norm_ffn, norm_final, dn_w_in, dn_conv, dn_a_log, dn_dt_bias, dn_norm, dn_w_out, hy_w_in, hy_conv, hy_w1, hy_b1, hy_w2, hy_b2, hy_w3, hy_b3, hy_w4, hy_freq, hy_bias, hy_w_out, moe_w_router, moe_b_router, moe_w_in, moe_b_in, moe_w_out, moe_b_out):
    Bp, Tp, D = x_prompt.shape
    Bs, Ts, _ = x_sample.shape
    depth = w_mod.shape[0]
    H = state_delta.shape[3]
    Mp, Ms = Bp * Tp, Bs * Ts
    M = Mp + Ms
    assert Mp % Ts == 0 and Ts % Tp == 0 and Tp % DN_CHUNK == 0, "token groups must tile each other"
    tT = Tp
    tm = min(512, Tp)

    xs = x_sample + _grid_pos_embedding(Ts, D)[None]
    x = jnp.concatenate([x_prompt.reshape(Mp, D), xs.reshape(Ms, D)], axis=0)
    tile_start = np.arange(M // tm) * tm
    rowmap = jnp.asarray(np.where(tile_start < Mp, 0, 1 + (tile_start - Mp) // Ts), I32)
    conv_start = np.arange(M // tT) * tT
    seq_len = np.where(conv_start < Mp, Tp, Ts)
    seq_pos = np.where(conv_start < Mp, conv_start % Tp, (conv_start - Mp) % Ts)
    has_prev = jnp.asarray(seq_pos > 0, I32)
    has_next = jnp.asarray(seq_pos + tT < seq_len, I32)

    n_cond = 1 + Bs
    r_pad = -(-n_cond // 8) * 8
    cond = jnp.zeros((r_pad, D), F32).at[0].set(c_ctx).at[1:n_cond].set(c)
    mod = _modulation(cond, w_mod, b_mod)

    new_states = []
    i_dn = i_hy = 0
    for l in range(depth):
        mod4 = mod[l].reshape(r_pad, 6, 1, D)
        if l % 2 == 0:
            i = i_dn
            i_dn += 1
            w_in = dn_w_in[i]
            n_main = 4 * H * HEAD_DIM
            w_ba = jnp.zeros((D, LANES), F32).at[:, :4 * H].set(w_in[:, n_main:]).astype(BF16)
            pm, ba = _fused_mm(_norm_mod_prologue, [x], [norm_mix[l]], [0, 1], w_in[:, :n_main].astype(BF16),
                               rowmap, mod4=mod4, side_w=w_ba, tm=tm, tn=1024)
            qkv, bg = _dn_act(pm, ba, dn_conv[i], dn_a_log[i], dn_dt_bias[i], has_prev, has_next, tT)
            hb_p = 2 if H % 2 == 0 else 1
            cols_p, gct_p = _delta_side_inputs(bg, 0, Bp, Tp, H, hb_p)
            cols_s, gct_s = _delta_side_inputs(bg, Mp, Bs, Ts, H, 1)
            og, s_fin = _delta(qkv, pm, cols_p, gct_p, dn_norm[i], None, None,
                               n_batch=Bp, batch_off=0, T=Tp, hb=hb_p)
            og, _ = _delta(qkv, pm, cols_s, gct_s, dn_norm[i], state_delta[:, i].astype(F32), og,
                           n_batch=Bs, batch_off=Mp // Ts, T=Ts, hb=1)
            new_states.append(s_fin.astype(state_delta.dtype))
            x = _fused_mm(_identity_prologue, [og], [], [], dn_w_out[i].astype(BF16), rowmap, mod4=mod4,
                          res=x, gate_chunk=2, out_dtype=F32, tm=tm, tn=1024)
        else:
            i = i_hy
            i_hy += 1
            u = _fused_mm(_norm_mod_prologue, [x], [norm_mix[l]], [0, 1], hy_w_in[i].astype(BF16),
                          rowmap, mod4=mod4, tm=tm, tn=1024)
            x0, s = _hy_pre(u, hy_conv[i], has_prev, has_next, tT)
            hy = (hy_w1[i], hy_b1[i], hy_w2[i], hy_b2[i], hy_w3[i], hy_b3[i], hy_w4[i], hy_freq[i])
            conv = jnp.concatenate([
                _hyena_conv(s, Bp, 0, Tp, _hyena_filter_taps(Tp, *hy)),
                _hyena_conv(s, Bs, Mp, Ts, _hyena_filter_taps(Ts, *hy))], axis=0)
            x = _fused_mm(_hyena_gate_prologue, [x0, conv, s], [hy_bias[i]], [], hy_w_out[i].astype(BF16),
                          rowmap, mod4=mod4, res=x, gate_chunk=2, out_dtype=F32, tm=tm, tn=1024)
        y = _moe_layer(x, norm_ffn[l], mod4, rowmap, tm, moe_w_router[l], moe_b_router[l],
                       moe_w_in[l], moe_b_in[l], moe_w_out[l], moe_b_out[l])
        gate_f = mod[l, :n_cond, 5 * D:]
        x = jnp.concatenate([
            x[:Mp] + gate_f[0][None] * y[:Mp],
            (x[Mp:].reshape(Bs, Ts, D) + gate_f[1:, None] * y[Mp:].reshape(Bs, Ts, D)).reshape(Ms, D)], axis=0)

    y_prompt = _final_norm(x, norm_final, 0, Mp, tm).reshape(Bp, Tp, D)
    y_sample = _final_norm(x, norm_final, Mp, Ms, tm).reshape(Bs, Ts, D)
    return y_prompt, y_sample, jnp.stack(new_states, axis=1)
```

```python
import functools
import math

import numpy as np
import jax
import jax.numpy as jnp
from jax import lax
from jax.experimental import pallas as pl
from jax.experimental.pallas import tpu as pltpu

F32, BF16, I32 = jnp.float32, jnp.bfloat16, jnp.int32

V7X_VMEM_LIMIT_BYTES = 56 * 2**20
LANES = 128
BF16_ROWS = 16

GRID_W = 64
DN_CHUNK = 64
HEAD_DIM = 128
TOP_K = 4
RMS_EPS = 1e-6
L2_EPS = 1e-6
POS_BASE = 10000.0
SWIGLU_LIMIT = 7.0
SWIGLU_ALPHA = 1.702
HY_TARGET = 1e-2
HY_FAST = 0.3
HY_SLOW = 1.5

HIGHEST = lax.Precision.HIGHEST
NT_DIMS = (((1,), (1,)), ((), ()))
TN_DIMS = (((0,), (0,)), ((), ()))


def _params(*sem):
    return pltpu.CompilerParams(dimension_semantics=sem, vmem_limit_bytes=V7X_VMEM_LIMIT_BYTES)


def _tile(n, pref):
    if n <= pref:
        return n
    t = pref - pref % LANES
    while n % t:
        t -= LANES
    return t


def _silu(x):
    return x * jax.nn.sigmoid(x)


def _softplus(x):
    return jnp.maximum(x, 0.0) + jnp.log1p(jnp.exp(-jnp.abs(x)))


def _split_bf16(x):
    hi = x.astype(BF16)
    lo = (x - hi.astype(F32)).astype(BF16)
    return hi, lo


def _dot3(a, b):
    ah, al = _split_bf16(a)
    bh, bl = _split_bf16(b)
    return (jnp.dot(ah, bh, preferred_element_type=F32)
            + (jnp.dot(ah, bl, preferred_element_type=F32) + jnp.dot(al, bh, preferred_element_type=F32)))


def _modulation_kernel(c_ref, w_ref, b_ref, o_ref):
    s = _silu(c_ref[...]).astype(BF16)
    o_ref[0] = jnp.dot(s, w_ref[0].astype(BF16), preferred_element_type=F32) + b_ref[0]


def _modulation(cond, w_mod, b_mod):
    R, D = cond.shape
    L, _, N = w_mod.shape
    tn = _tile(N, 1024)
    return pl.pallas_call(
        _modulation_kernel,
        grid=(L, N // tn),
        in_specs=[pl.BlockSpec((R, D), lambda l, j: (0, 0)),
                  pl.BlockSpec((1, D, tn), lambda l, j: (l, 0, j)),
                  pl.BlockSpec((1, 1, tn), lambda l, j: (l, 0, j))],
        out_specs=pl.BlockSpec((1, R, tn), lambda l, j: (l, 0, j)),
        out_shape=jax.ShapeDtypeStruct((L, R, N), F32),
        compiler_params=_params("parallel", "parallel"),
        name="modulation",
    )(cond, w_mod, b_mod.reshape(L, 1, N))


def _fused_mm_kernel(prologue, n_a, n_v, n_m, has_res, has_side, rm_ref, *refs):
    del rm_ref
    a_refs, refs = refs[:n_a], refs[n_a:]
    v_refs, refs = refs[:n_v], refs[n_v:]
    m_refs, refs = refs[:n_m], refs[n_m:]
    w_ref, refs = refs[0], refs[1:]
    if has_res:
        res_ref, gate_ref, refs = refs[0], refs[1], refs[2:]
    if has_side:
        sw_ref, refs = refs[0], refs[1:]
    o_ref, refs = refs[0], refs[1:]
    if has_side:
        so_ref, refs = refs[0], refs[1:]
    a_scr = refs[0]

    @pl.when(pl.program_id(1) == 0)
    def _():
        a = prologue([r[...] for r in a_refs], [r[...] for r in v_refs], [r[0, 0] for r in m_refs])
        a_scr[...] = a.astype(BF16)
        if has_side:
            so_ref[...] = jnp.dot(a_scr[...], sw_ref[...], preferred_element_type=F32)

    acc = jnp.dot(a_scr[...], w_ref[...], preferred_element_type=F32)
    if has_res:
        acc = res_ref[...] + gate_ref[0, 0] * acc
    o_ref[...] = acc.astype(o_ref.dtype)


def _fused_mm(prologue, a_ins, vec_ins, mod_ins, w, rowmap, *, mod4=None, res=None, gate_chunk=None,
              side_w=None, out_dtype=BF16, tm, tn):
    M, K = a_ins[0].shape
    N = w.shape[1]
    tn = _tile(N, tn)
    in_specs, args = [], []
    for a in a_ins:
        in_specs.append(pl.BlockSpec((tm, K), lambda i, j, rm: (i, 0)))
        args.append(a)
    for v in vec_ins:
        in_specs.append(pl.BlockSpec((1, K), lambda i, j, rm: (0, 0)))
        args.append(v.reshape(1, K))
    for c in mod_ins:
        in_specs.append(pl.BlockSpec((1, 1, 1, K), lambda i, j, rm, c=c: (rm[i], c, 0, 0)))
        args.append(mod4)
    in_specs.append(pl.BlockSpec((K, tn), lambda i, j, rm: (0, j)))
    args.append(w)
    if res is not None:
        in_specs.append(pl.BlockSpec((tm, tn), lambda i, j, rm: (i, j)))
        args.append(res)
        in_specs.append(pl.BlockSpec((1, 1, 1, tn), lambda i, j, rm, c=gate_chunk: (rm[i], c, 0, j)))
        args.append(mod4)
    out_specs = [pl.BlockSpec((tm, tn), lambda i, j, rm: (i, j))]
    out_shape = [jax.ShapeDtypeStruct((M, N), out_dtype)]
    if side_w is not None:
        ns = side_w.shape[1]
        in_specs.append(pl.BlockSpec((K, ns), lambda i, j, rm: (0, 0)))
        args.append(side_w)
        out_specs.append(pl.BlockSpec((tm, ns), lambda i, j, rm: (i, 0)))
        out_shape.append(jax.ShapeDtypeStruct((M, ns), F32))
    kern = functools.partial(_fused_mm_kernel, prologue, len(a_ins), len(vec_ins), len(mod_ins),
                             res is not None, side_w is not None)
    outs = pl.pallas_call(
        kern,
        grid_spec=pltpu.PrefetchScalarGridSpec(
            num_scalar_prefetch=1, grid=(M // tm, N // tn), in_specs=in_specs, out_specs=out_specs,
            scratch_shapes=[pltpu.VMEM((tm, K), BF16)]),
        out_shape=out_shape,
        compiler_params=_params("parallel", "arbitrary"),
        name="fused_mm",
    )(rowmap, *args)
    return outs if side_w is not None else outs[0]


def _norm_mod_prologue(a, v, m):
    x, g, (shift, scale) = a[0], v[0], m
    y = x * lax.rsqrt(jnp.mean(x * x, axis=-1, keepdims=True) + RMS_EPS) * g
    return y * (1.0 + scale) + shift


def _identity_prologue(a, v, m):
    return a[0]


def _hyena_gate_prologue(a, v, m):
    x0, conv, s = (t.astype(F32) for t in a)
    return x0 * (conv + s * v[0])


def _conv3(x, prev_row, next_row, w, rid):
    t = x.shape[0]
    xm = jnp.where(rid == 0, prev_row, pltpu.roll(x, 1, 0))
    xp = jnp.where(rid == t - 1, next_row, pltpu.roll(x, t - 1, 0))
    return xm * w[0:1] + x * w[1:2] + xp * w[2:3]


def _halo_specs(tT, width, col_block):
    hb = tT // BF16_ROWS
    return [
        pl.BlockSpec((tT, width), lambda i, hp, hn: (i, col_block)),
        pl.BlockSpec((BF16_ROWS, width), lambda i, hp, hn: (jnp.maximum(i * hb - 1, 0), col_block)),
        pl.BlockSpec((BF16_ROWS, width), lambda i, hp, hn: ((i + 1) * hb * hn[i], col_block)),
    ]


def _dn_act_kernel(n_heads, hp_ref, hn_ref, x_ref, xp_ref, xn_ref, ba_ref, cw_ref, al_ref, dtb_ref,
                   qkv_ref, bg_ref):
    i = pl.program_id(0)
    hp = hp_ref[i].astype(F32)
    hn = hn_ref[i].astype(F32)
    tT = x_ref.shape[0]
    rid = lax.broadcasted_iota(I32, (tT, LANES), 0)
    for cb in range(3 * n_heads):
        sl = slice(cb * LANES, (cb + 1) * LANES)
        x = x_ref[:, sl].astype(F32)
        pr = xp_ref[:, sl].astype(F32)[BF16_ROWS - 1:BF16_ROWS] * hp
        nx = xn_ref[:, sl].astype(F32)[0:1] * hn
        y = _silu(_conv3(x, pr, nx, cw_ref[:, sl], rid))
        if cb < 2 * n_heads:
            y = y * lax.rsqrt(jnp.sum(y * y, axis=-1, keepdims=True) + L2_EPS)
            if cb < n_heads:
                y = y * (HEAD_DIM ** -0.5)
        qkv_ref[:, sl] = y.astype(BF16)

    ba = ba_ref[...]
    lane = lax.broadcasted_iota(I32, (tT, LANES), 1)
    beta = jax.nn.sigmoid(ba)
    g = -jnp.exp(al_ref[...]) * _softplus(ba + dtb_ref[...])
    r = lax.broadcasted_iota(I32, (tT, tT), 0)
    c = lax.broadcasted_iota(I32, (tT, tT), 1)
    shift = DN_CHUNK.bit_length() - 1
    same = jnp.right_shift(r, shift) == jnp.right_shift(c, shift)
    l_pre = jnp.where(same & (c <= r), 1.0, 0.0).astype(F32)
    l_suf = jnp.where(same & (c >= r), 1.0, 0.0).astype(F32)
    g_pre = jnp.dot(l_pre, g, precision=HIGHEST, preferred_element_type=F32)
    g_suf = jnp.dot(l_suf, g, precision=HIGHEST, preferred_element_type=F32)
    gc = jnp.where(lane < 3 * n_heads, g_pre, g_suf)
    bg_ref[...] = jnp.where(lane < 2 * n_heads, beta, gc)


def _dn_act(pm, ba, conv_w, a_log, dt_bias, has_prev, has_next, tT):
    M = pm.shape[0]
    cw = conv_w.shape[1]
    n_heads = cw // (3 * HEAD_DIM)
    pad = lambda v: jnp.zeros((1, LANES), F32).at[0, 2 * n_heads:4 * n_heads].set(v.reshape(-1).astype(F32))
    vec = pl.BlockSpec((1, LANES), lambda i, hp, hn: (0, 0))
    return pl.pallas_call(
        functools.partial(_dn_act_kernel, n_heads),
        grid_spec=pltpu.PrefetchScalarGridSpec(
            num_scalar_prefetch=2, grid=(M // tT,),
            in_specs=_halo_specs(tT, cw, 0) + [
                pl.BlockSpec((tT, LANES), lambda i, hp, hn: (i, 0)),
                pl.BlockSpec((3, cw), lambda i, hp, hn: (0, 0)), vec, vec],
            out_specs=[pl.BlockSpec((tT, cw), lambda i, hp, hn: (i, 0)),
                       pl.BlockSpec((tT, LANES), lambda i, hp, hn: (i, 0))]),
        out_shape=[jax.ShapeDtypeStruct((M, cw), BF16), jax.ShapeDtypeStruct((M, LANES), F32)],
        compiler_params=_params("parallel"),
        name="dn_act",
    )(has_prev, has_next, pm, pm, pm, ba, conv_w, pad(a_log), pad(dt_bias))


def _merge_masks(ri, ci, n, lower):
    masks = []
    s = 0
    while (1 << s) < n:
        same = jnp.right_shift(ri, s + 1) == jnp.right_shift(ci, s + 1)
        hi_r = jnp.bitwise_and(jnp.right_shift(ri, s), 1)
        hi_c = jnp.bitwise_and(jnp.right_shift(ci, s), 1)
        off = (hi_r == 1) & (hi_c == 0) if lower else (hi_r == 0) & (hi_c == 1)
        masks.append(same & off)
        s += 1
    return masks


def _dot3_many(xs, ys):
    sx = [_split_bf16(x) for x in xs]
    sy = [_split_bf16(y) for y in ys]
    hh = [jnp.dot(x[0], y[0], preferred_element_type=F32) for x, y in zip(sx, sy)]
    hl = [jnp.dot(x[0], y[1], preferred_element_type=F32) for x, y in zip(sx, sy)]
    lh = [jnp.dot(x[1], y[0], preferred_element_type=F32) for x, y in zip(sx, sy)]
    return [a + (b + c) for a, b, c in zip(hh, hl, lh)]


def _unit_tri_inverse_many(mats, eye, masks):
    ts = [eye - jnp.where(m[0], a, 0.0) for a, m in zip(mats, masks)]
    for lvl in range(1, len(masks[0])):
        off = [jnp.where(m[lvl], a, 0.0) for a, m in zip(mats, masks)]
        upd = _dot3_many(_dot3_many(ts, off), ts)
        ts = [t - u for t, u in zip(ts, upd)]
    return ts


def _delta_kernel(hb, n_chunks, group, has_s0, *refs):
    if has_s0:
        q_ref, k_ref, v_ref, z_ref, cols_ref, gct_ref, ng_ref, s0_ref, _alias, o_ref, sfin_ref = refs[:11]
        scr = refs[11:]
    else:
        q_ref, k_ref, v_ref, z_ref, cols_ref, gct_ref, ng_ref, o_ref, sfin_ref = refs[:9]
        scr = refs[9:]
    u_scr, wq_scr, qk_scr, s_scr, o_scr = scr
    C = DN_CHUNK
    n = n_chunks
    ri = lax.broadcasted_iota(I32, (C, C), 0)
    ci = lax.broadcasted_iota(I32, (C, C), 1)
    eye = (ri == ci).astype(F32)
    incl = (ri >= ci, ri <= ci)

    def gate_cols(hh, d, rows):
        bcol = cols_ref[0, 0, rows, hh * 4 + d:hh * 4 + d + 1]
        gcol = cols_ref[0, 0, rows, hh * 4 + 2 + d:hh * 4 + 3 + d]
        gl = gcol[C - 1:C, :] if d == 0 else gcol[0:1, :]
        return bcol, gcol, gl

    def prep(it, carry):
        tri_masks = (_merge_masks(ri, ci, C, True), _merge_masks(ri, ci, C, False))
        where_, mats, masks, rhss = [], [], [], []
        for gi in range(group):
            c = it * group + gi
            rows = pl.ds(pl.multiple_of(c * C, C), C)
            for hh in range(hb):
                ls = slice(hh * HEAD_DIM, (hh + 1) * HEAD_DIM)
                qb, kb16, vb = q_ref[0, rows, ls], k_ref[0, rows, ls], v_ref[0, rows, ls]
                qf, kf, vf = qb.astype(F32), kb16.astype(F32), vb.astype(F32)
                qkt = lax.dot_general(qb, kb16, NT_DIMS, preferred_element_type=F32)
                for d in range(2):
                    ch = hh * 2 + d
                    bcol, gcol, _ = gate_cols(hh, d, rows)
                    grow = gct_ref[0, hh, d, pl.ds(c, 1), :]
                    dm = jnp.where(incl[d], jnp.exp(jnp.where(incl[d], gcol - grow, 0.0)), 0.0)
                    kbeta = kf * bcol
                    eg = jnp.exp(gcol)
                    mats.append(lax.dot_general(kbeta.astype(BF16), kb16, NT_DIMS, preferred_element_type=F32) * dm)
                    masks.append(tri_masks[d])
                    rhss.append(jnp.concatenate([vf * bcol, kbeta * eg], axis=1))
                    where_.append((ch, c, rows))
                    wq_scr[ch, pl.ds(pl.multiple_of(c * 2 * C + C, C), C), :] = (qf * eg).astype(BF16)
                    qk_scr[ch, rows, :] = (qkt * dm).astype(BF16)
        sols = _dot3_many(_unit_tri_inverse_many(mats, eye, masks), rhss)
        for (ch, c, rows), sol in zip(where_, sols):
            u_scr[ch, rows, :] = sol[:, :HEAD_DIM]
            wq_scr[ch, pl.ds(pl.multiple_of(c * 2 * C, 2 * C), C), :] = sol[:, HEAD_DIM:].astype(BF16)
        return carry

    lax.fori_loop(0, n // group, prep, 0)

    for hh in range(hb):
        for d in range(2):
            s_scr[hh * 2 + d] = s0_ref[0, d, hh] if has_s0 else jnp.zeros((HEAD_DIM, HEAD_DIM), F32)

    T = n * C
    tr = min(T, 256)

    def clear(b, carry):
        o_scr[:, pl.ds(pl.multiple_of(b * tr, tr), tr), :] = jnp.zeros((hb, tr, HEAD_DIM), F32)
        return carry

    lax.fori_loop(0, T // tr, clear, 0)

    def scan(it, carry):
        chains = [(hh, d) for hh in range(hb) for d in range(2)]
        cs = [it if d == 0 else n - 1 - it for _, d in chains]
        rows = [pl.ds(pl.multiple_of(c * C, C), C) for c in cs]
        ss = [s_scr[hh * 2 + d] for hh, d in chains]
        rs = [jnp.dot(wq_scr[hh * 2 + d, pl.ds(pl.multiple_of(c * 2 * C, 2 * C), 2 * C), :], s.astype(BF16),
                      preferred_element_type=F32) for (hh, d), c, s in zip(chains, cs, ss)]
        vns = [(u_scr[hh * 2 + d, rw, :] - r[:C]).astype(BF16) for (hh, d), rw, r in zip(chains, rows, rs)]
        for i, (hh, d) in enumerate(chains):
            _, gcol, gl = gate_cols(hh, d, rows[i])
            kt = (k_ref[0, rows[i], hh * HEAD_DIM:(hh + 1) * HEAD_DIM].astype(F32) * jnp.exp(gl - gcol)).astype(BF16)
            s_scr[hh * 2 + d] = (ss[i] * jnp.exp(gl)
                                 + lax.dot_general(kt, vns[i], TN_DIMS, preferred_element_type=F32))
        for i, (hh, d) in enumerate(chains):
            o = rs[i][C:] + jnp.dot(qk_scr[hh * 2 + d, rows[i], :], vns[i], preferred_element_type=F32)
            o_scr[hh, rows[i], :] += o
        return carry

    lax.fori_loop(0, n, scan, 0)

    for hh in range(hb):
        for d in range(2):
            sfin_ref[0, d, hh] = s_scr[hh * 2 + d]

    def gate(b, carry):
        rows = pl.ds(pl.multiple_of(b * tr, tr), tr)
        for hh in range(hb):
            ls = slice(hh * HEAD_DIM, (hh + 1) * HEAD_DIM)
            o = o_scr[hh, rows, :]
            o = o * lax.rsqrt(jnp.mean(o * o, axis=-1, keepdims=True) + RMS_EPS) * ng_ref[...]
            o_ref[0, rows, ls] = (o * _silu(z_ref[0, rows, ls].astype(F32))).astype(BF16)
        return carry

    lax.fori_loop(0, T // tr, gate, 0)


def _delta(qkv, pm, cols, gct, norm_g, s0, alias, *, n_batch, batch_off, T, hb):
    M, cw = qkv.shape
    H = cw // (3 * HEAD_DIM)
    n = T // DN_CHUNK
    W = hb * HEAD_DIM
    nb = H // hb
    qkv3 = qkv.reshape(M // T, T, cw)
    pm3 = pm.reshape(M // T, T, pm.shape[1])
    once = pl.Buffered(1)
    blk = lambda off: pl.BlockSpec((1, T, W), lambda b, j, off=off: (b + batch_off, 0, off * nb + j),
                                   pipeline_mode=once)
    in_specs = [blk(0), blk(1), blk(2), blk(3),
                pl.BlockSpec((1, 1, T, 4 * hb), lambda b, j: (b, j, 0, 0), pipeline_mode=once),
                pl.BlockSpec((1, hb, 2, n, DN_CHUNK), lambda b, j: (b, j, 0, 0, 0)),
                pl.BlockSpec((1, HEAD_DIM), lambda b, j: (0, 0))]
    args = [qkv3, qkv3, qkv3, pm3, cols, gct, norm_g.reshape(1, HEAD_DIM)]
    aliases = {}
    if s0 is not None:
        in_specs.append(pl.BlockSpec((1, 2, hb, HEAD_DIM, HEAD_DIM), lambda b, j: (b, 0, j, 0, 0)))
        in_specs.append(pl.BlockSpec(memory_space=pl.ANY))
        args += [s0, alias.reshape(M // T, T, H * HEAD_DIM)]
        aliases = {len(args) - 1: 0}
    nch = 2 * hb
    group = math.gcd(n, max(1, 8 // nch))
    out, sfin = pl.pallas_call(
        functools.partial(_delta_kernel, hb, n, group, s0 is not None),
        grid=(n_batch, nb),
        in_specs=in_specs,
        out_specs=[pl.BlockSpec((1, T, W), lambda b, j: (b + batch_off, 0, j)),
                   pl.BlockSpec((1, 2, hb, HEAD_DIM, HEAD_DIM), lambda b, j: (b, 0, j, 0, 0))],
        out_shape=[jax.ShapeDtypeStruct((M // T, T, H * HEAD_DIM), BF16),
                   jax.ShapeDtypeStruct((n_batch, 2, H, HEAD_DIM, HEAD_DIM), F32)],
        scratch_shapes=[pltpu.VMEM((nch, T, HEAD_DIM), F32),
                        pltpu.VMEM((nch, 2 * T, HEAD_DIM), BF16),
                        pltpu.VMEM((nch, T, DN_CHUNK), BF16),
                        pltpu.VMEM((nch, HEAD_DIM, HEAD_DIM), F32),
                        pltpu.VMEM((hb, T, HEAD_DIM), F32)],
        input_output_aliases=aliases,
        compiler_params=_params("parallel", "parallel"),
        name="delta_rule",
    )(*args)
    return out.reshape(M, H * HEAD_DIM), sfin


def _delta_side_inputs(bg, t0, n_batch, T, H, hb):
    b5 = bg[t0:t0 + n_batch * T, :4 * H].reshape(n_batch, T, 2, 2, H)
    cols = jnp.transpose(b5, (0, 4, 1, 2, 3)).reshape(n_batch, H // hb, hb, T, 4)
    cols = jnp.transpose(cols, (0, 1, 3, 2, 4)).reshape(n_batch, H // hb, T, 4 * hb)
    gct = jnp.transpose(b5[:, :, 1], (0, 3, 2, 1)).reshape(n_batch, H, 2, T // DN_CHUNK, DN_CHUNK)
    return cols, gct


def _hy_pre_kernel(D, hp_ref, hn_ref, x_ref, xp_ref, xn_ref, cw_ref, x0_ref, s_ref):
    i = pl.program_id(0)
    hp = hp_ref[i].astype(F32)
    hn = hn_ref[i].astype(F32)
    tT = x_ref.shape[0]
    W = min(D, 2 * LANES)
    rid = lax.broadcasted_iota(I32, (tT, W), 0)

    def conv(cb, part):
        sl = slice(part * D + cb * W, part * D + (cb + 1) * W)
        x = x_ref[:, sl].astype(F32)
        pr = xp_ref[:, sl].astype(F32)[BF16_ROWS - 1:BF16_ROWS] * hp
        nx = xn_ref[:, sl].astype(F32)[0:1] * hn
        return _conv3(x, pr, nx, cw_ref[:, sl], rid)

    for cb in range(D // W):
        sl = slice(cb * W, (cb + 1) * W)
        x0_ref[:, sl] = conv(cb, 0).astype(BF16)
        s_ref[:, sl] = (conv(cb, 1) * conv(cb, 2)).astype(BF16)


def _hy_pre(u, conv_w, has_prev, has_next, tT):
    M, W3 = u.shape
    D = W3 // 3
    out = pl.BlockSpec((tT, D), lambda i, hp, hn: (i, 0))
    return pl.pallas_call(
        functools.partial(_hy_pre_kernel, D),
        grid_spec=pltpu.PrefetchScalarGridSpec(
            num_scalar_prefetch=2, grid=(M // tT,),
            in_specs=_halo_specs(tT, W3, 0) + [pl.BlockSpec((3, W3), lambda i, hp, hn: (0, 0))],
            out_specs=[out, out]),
        out_shape=[jax.ShapeDtypeStruct((M, D), BF16)] * 2,
        compiler_params=_params("parallel"),
        name="hy_pre",
    )(has_prev, has_next, u, u, u, conv_w)


def _bmm_kernel(a_ref, b_ref, o_ref, acc_ref):
    k = pl.program_id(3)
    part = jnp.dot(a_ref[...], b_ref[0], preferred_element_type=F32)

    @pl.when(k == 0)
    def _():
        acc_ref[...] = part

    @pl.when(k > 0)
    def _():
        acc_ref[...] += part

    @pl.when(k == pl.num_programs(3) - 1)
    def _():
        o_ref[0] = acc_ref[...].astype(o_ref.dtype)


def _bmm(a, b, b_batch_off=0, n_batch=None, out_dtype=BF16, tm=1024, tn=1024, tk=2048):
    M, K = a.shape
    N = b.shape[2]
    nb = b.shape[0] if n_batch is None else n_batch
    tm, tn, tk = _tile(M, tm), _tile(N, tn), _tile(K, tk)
    return pl.pallas_call(
        _bmm_kernel,
        grid=(nb, M // tm, N // tn, K // tk),
        in_specs=[pl.BlockSpec((tm, tk), lambda i, m, n, k: (m, k)),
                  pl.BlockSpec((1, tk, tn), lambda i, m, n, k: (i + b_batch_off, k, n))],
        out_specs=pl.BlockSpec((1, tm, tn), lambda i, m, n, k: (i, m, n)),
        out_shape=jax.ShapeDtypeStruct((nb, M, N), out_dtype),
        scratch_shapes=[pltpu.VMEM((tm, tn), F32)],
        compiler_params=_params("parallel", "parallel", "parallel", "arbitrary"),
        name="dft_mm",
    )(a, b)


def _spec_prod_kernel(inv_n, s_ref, k_ref, y_ref):
    sc, ss = s_ref[0, 0].astype(F32), s_ref[0, 1].astype(F32)
    kc, ks = k_ref[0], k_ref[1]
    first = (lax.broadcasted_iota(I32, sc.shape, 0) == 0) & (pl.program_id(1) == 0)
    y_ref[0, 0] = (jnp.where(first, sc * kc, 2.0 * (sc * kc - ss * ks)) * inv_n).astype(y_ref.dtype)
    y_ref[0, 1] = (jnp.where(first, ss * ks, 2.0 * (sc * ks + ss * kc)) * inv_n).astype(y_ref.dtype)


def _spec_prod(sf, kf):
    B, _, T, D = sf.shape
    tr, tc = _tile(T, 256), _tile(D, 1024)
    return pl.pallas_call(
        functools.partial(_spec_prod_kernel, 1.0 / (2 * T)),
        grid=(B, T // tr, D // tc),
        in_specs=[pl.BlockSpec((1, 2, tr, tc), lambda b, i, j: (b, 0, i, j)),
                  pl.BlockSpec((2, tr, tc), lambda b, i, j: (0, i, j))],
        out_specs=pl.BlockSpec((1, 2, tr, tc), lambda b, i, j: (b, 0, i, j)),
        out_shape=jax.ShapeDtypeStruct(sf.shape, BF16),
        compiler_params=_params("parallel", "parallel", "parallel"),
        name="spec_prod",
    )(sf, kf)


def _dft_matrix(T):
    k = jnp.arange(T, dtype=I32)[:, None]
    t = jnp.arange(T, dtype=I32)[None, :]
    ang = ((k * t) % (2 * T)).astype(F32) * (math.pi / T)
    nyq = jnp.where(t % 2 == 0, 1.0, -1.0).astype(F32)
    sin = jnp.where(k == 0, nyq, jnp.sin(ang))
    return jnp.concatenate([jnp.cos(ang), sin], axis=0).astype(BF16)


def _hyena_filter_taps(L, w1, b1, w2, b2, w3, b3, w4, freq):
    D = w4.shape[1] // 2
    n_bands = (w1.shape[0] - 1) // 2
    pos = jnp.arange(L, dtype=F32)
    t = pos / max(L - 1, 1)
    bands = jnp.linspace(1e-4, n_bands - 1, n_bands, dtype=F32)
    ang = (2.0 * math.pi / L) * pos[:, None] * bands[None]
    feats = jnp.concatenate([t[:, None], jnp.cos(ang), -jnp.sin(ang)], axis=-1)
    zf = jnp.sin(freq[0] * (feats @ w1 + b1))
    zf = jnp.sin(freq[1] * (zf @ w2 + b2))
    zf = jnp.sin(freq[2] * (zf @ w3 + b3))
    filt = (zf @ w4).reshape(L, 2, D)
    deltas = jnp.abs(jnp.linspace(math.log(HY_TARGET) / HY_SLOW, math.log(HY_TARGET) / HY_FAST, D, dtype=F32))
    filt = filt * jnp.exp(-t[:, None, None] * deltas[None, None])
    f = filt[:, 0]
    b = filt[:, 1].at[0].set(0.0)
    scale = lax.rsqrt(jnp.sum(f * f, axis=0) + jnp.sum(b * b, axis=0) + 1e-6)
    return f * scale, b * scale


def _hyena_conv(s, n_batch, t0, T, taps):
    M, D = s.shape
    f, b = taps
    fwd = _dft_matrix(T)
    kf = _bmm(fwd, jnp.concatenate([f, b], axis=1).astype(BF16)[None], out_dtype=F32)[0]
    p, q = kf[:, :D].reshape(2, T, D), kf[:, D:].reshape(2, T, D)
    first = (jnp.arange(T) == 0)[:, None]
    kspec = jnp.stack([p[0] + q[0], jnp.where(first, p[1] + q[1], p[1] - q[1])])
    s3 = s.reshape(M // T, T, D)
    sf = _bmm(fwd, s3, b_batch_off=t0 // T, n_batch=n_batch)
    y = _spec_prod(sf.reshape(n_batch, 2, T, D), kspec).reshape(n_batch, 2 * T, D)
    conv = _bmm(fwd.T, y)
    return conv.reshape(n_batch * T, D)


def _router_kernel(n_experts, rm_ref, x_ref, g_ref, sh_ref, sc_ref, wr_ref, br_ref, h_ref, ti_ref, tg_ref):
    del rm_ref
    h = _norm_mod_prologue([x_ref[...]], [g_ref[...]], (sh_ref[0, 0], sc_ref[0, 0]))
    h_ref[...] = h.astype(BF16)
    logits = jnp.dot(h, wr_ref[...], precision=HIGHEST, preferred_element_type=F32) + br_ref[...]
    lane = lax.broadcasted_iota(I32, logits.shape, 1)
    lane_f = lane.astype(F32)
    neg = jnp.float32(-jnp.inf)
    l = jnp.where(lane < n_experts, logits, neg)
    ti = jnp.zeros(logits.shape, I32)
    tl = jnp.full(logits.shape, neg, F32)
    for r in range(TOP_K):
        m = jnp.max(l, axis=-1, keepdims=True)
        idx = jnp.min(jnp.where(l == m, lane_f, float(LANES)), axis=-1, keepdims=True).astype(I32)
        ti = jnp.where(lane == r, idx, ti)
        tl = jnp.where(lane == r, m, tl)
        l = jnp.where(lane == idx, neg, l)
    e = jnp.exp(tl - jnp.max(tl, axis=-1, keepdims=True))
    ti_ref[...] = ti
    tg_ref[...] = e / jnp.sum(e, axis=-1, keepdims=True)


def _router(x, g, mod4, rowmap, w_router, b_router, tm):
    M, D = x.shape
    E = w_router.shape[1]
    wr = jnp.zeros((D, LANES), F32).at[:, :E].set(w_router)
    br = jnp.zeros((1, LANES), F32).at[0, :E].set(b_router)
    row = lambda c: pl.BlockSpec((1, 1, 1, D), lambda i, rm, c=c: (rm[i], c, 0, 0))
    tile = lambda w: pl.BlockSpec((tm, w), lambda i, rm: (i, 0))
    return pl.pallas_call(
        functools.partial(_router_kernel, E),
        grid_spec=pltpu.PrefetchScalarGridSpec(
            num_scalar_prefetch=1, grid=(M // tm,),
            in_specs=[tile(D), pl.BlockSpec((1, D), lambda i, rm: (0, 0)), row(3), row(4),
                      pl.BlockSpec((D, LANES), lambda i, rm: (0, 0)),
                      pl.BlockSpec((1, LANES), lambda i, rm: (0, 0))],
            out_specs=[tile(D), tile(LANES), tile(LANES)]),
        out_shape=[jax.ShapeDtypeStruct((M, D), BF16), jax.ShapeDtypeStruct((M, LANES), I32),
                   jax.ShapeDtypeStruct((M, LANES), F32)],
        compiler_params=_params("parallel"),
        name="router",
    )(rowmap, x, g.reshape(1, D), mod4, mod4, wr, br)


def _moe_kernel(te_ref, nv_ref, x_ref, wg_ref, wu_ref, bg_ref, bu_ref, wo_ref, bo_ref, o_ref, acc_ref):
    del te_ref
    t, f = pl.program_id(0), pl.program_id(1)
    last = pl.num_programs(1) - 1
    valid = t < nv_ref[0]

    @pl.when(valid)
    def _():
        x = x_ref[...]
        g = jnp.dot(x, wg_ref[0], preferred_element_type=F32) + bg_ref[0]
        u = jnp.dot(x, wu_ref[0], preferred_element_type=F32) + bu_ref[0]
        g = jnp.minimum(g, SWIGLU_LIMIT)
        u = jnp.clip(u, -SWIGLU_LIMIT, SWIGLU_LIMIT)
        h = ((u + 1.0) * g * jax.nn.sigmoid(SWIGLU_ALPHA * g)).astype(BF16)
        part = jnp.dot(h, wo_ref[0], preferred_element_type=F32)

        @pl.when(f == 0)
        def _():
            acc_ref[...] = part + bo_ref[0]

        @pl.when(f > 0)
        def _():
            acc_ref[...] += part

        @pl.when(f == last)
        def _():
            o_ref[...] = acc_ref[...].astype(o_ref.dtype)

    @pl.when(jnp.logical_not(valid) & (f == last))
    def _():
        o_ref[...] = jnp.zeros(o_ref.shape, o_ref.dtype)


def _moe_experts(x_rows, tile_expert, n_valid, w_in, b_in, w_out, b_out, tm, tf):
    R, D = x_rows.shape
    E, _, F2 = w_in.shape
    F = F2 // 2
    tf = _tile(F, tf)
    nf = F // tf

    def fi(t, f, nv):
        return jnp.where(t < nv[0], f, nf - 1)

    def ti(t, nv):
        return jnp.minimum(t, jnp.maximum(nv[0] - 1, 0))

    return pl.pallas_call(
        _moe_kernel,
        grid_spec=pltpu.PrefetchScalarGridSpec(
            num_scalar_prefetch=2, grid=(R // tm, nf),
            in_specs=[pl.BlockSpec((tm, D), lambda t, f, te, nv: (ti(t, nv), 0)),
                      pl.BlockSpec((1, D, tf), lambda t, f, te, nv: (te[t], 0, fi(t, f, nv))),
                      pl.BlockSpec((1, D, tf), lambda t, f, te, nv: (te[t], 0, nf + fi(t, f, nv))),
                      pl.BlockSpec((1, 1, tf), lambda t, f, te, nv: (te[t], 0, fi(t, f, nv))),
                      pl.BlockSpec((1, 1, tf), lambda t, f, te, nv: (te[t], 0, nf + fi(t, f, nv))),
                      pl.BlockSpec((1, tf, D), lambda t, f, te, nv: (te[t], fi(t, f, nv), 0)),
                      pl.BlockSpec((1, 1, D), lambda t, f, te, nv: (te[t], 0, 0))],
            out_specs=pl.BlockSpec((tm, D), lambda t, f, te, nv: (t, 0)),
            scratch_shapes=[pltpu.VMEM((tm, D), F32)]),
        out_shape=jax.ShapeDtypeStruct((R, D), BF16),
        compiler_params=_params("parallel", "arbitrary"),
        name="moe_experts",
    )(tile_expert, n_valid, x_rows, w_in, w_in, b_in.reshape(E, 1, F2), b_in.reshape(E, 1, F2),
      w_out, b_out.reshape(E, 1, D))


def _moe_layer(x, norm_g, mod4, rowmap, tm_tok, w_router, b_router, w_in, b_in, w_out, b_out, tm=512, tf=512):
    M, D = x.shape
    E = w_router.shape[1]
    h, ti, tg = _router(x, norm_g, mod4, rowmap, w_router, b_router, tm_tok)
    top_idx, gates = ti[:, :TOP_K], tg[:, :TOP_K]
    A = M * TOP_K
    n_tiles = -(-A // tm) + E
    e_flat = top_idx.reshape(A)
    order = jnp.argsort(e_flat, stable=True).astype(I32)
    e_sorted = e_flat[order]
    counts = jnp.sum((e_flat[:, None] == jnp.arange(E, dtype=I32)[None]).astype(I32), axis=0)
    starts = jnp.cumsum(counts) - counts
    ptiles = (counts + tm - 1) // tm
    pend = jnp.cumsum(ptiles)
    pstart = pend - ptiles
    n_valid = pend[-1]
    tix = jnp.minimum(jnp.arange(n_tiles, dtype=I32), jnp.maximum(n_valid - 1, 0))
    tile_expert = jnp.minimum(jnp.searchsorted(pend, tix, side='right'), E - 1).astype(I32)
    rows = jnp.arange(n_tiles * tm, dtype=I32)
    e_row = tile_expert[rows // tm]
    off = rows - pstart[e_row] * tm
    src = order[jnp.clip(starts[e_row] + off, 0, A - 1)]
    row_tok = jnp.where((off < counts[e_row]) & (rows // tm < n_valid), src // TOP_K, 0)
    x_rows = h[row_tok]
    y_rows = _moe_experts(x_rows, tile_expert, n_valid.reshape(1).astype(I32),
                          w_in.astype(BF16), b_in, w_out.astype(BF16), b_out, tm, tf)
    dest_sorted = pstart[e_sorted] * tm + jnp.arange(A, dtype=I32) - starts[e_sorted]
    dest = jnp.zeros((A,), I32).at[order].set(dest_sorted).reshape(M, TOP_K)
    y = jnp.zeros((M, D), F32)
    for k in range(TOP_K):
        y = y + gates[:, k:k + 1] * y_rows[dest[:, k]].astype(F32)
    return y


def _final_norm_kernel(x_ref, g_ref, o_ref):
    x = x_ref[...]
    o_ref[...] = x * lax.rsqrt(jnp.mean(x * x, axis=-1, keepdims=True) + RMS_EPS) * g_ref[...]


def _final_norm(x, g, t0, rows, tm):
    D = x.shape[1]
    return pl.pallas_call(
        _final_norm_kernel,
        grid=(rows // tm,),
        in_specs=[pl.BlockSpec((tm, D), lambda i: (i + t0 // tm, 0)), pl.BlockSpec((1, D), lambda i: (0, 0))],
        out_specs=pl.BlockSpec((tm, D), lambda i: (i, 0)),
        out_shape=jax.ShapeDtypeStruct((rows, D), F32),
        compiler_params=_params("parallel"),
        name="final_norm",
    )(x, g.reshape(1, D))


def _grid_pos_embedding(T, D):
    rows = T // GRID_W
    row = jnp.repeat(jnp.arange(rows), GRID_W)
    col = jnp.tile(jnp.arange(GRID_W), rows)
    quarter = D // 4
    omega = 1.0 / (POS_BASE ** (jnp.arange(quarter, dtype=F32) / quarter))

    def axis_emb(p):
        ang = p.astype(F32)[:, None] * omega[None]
        return jnp.concatenate([jnp.sin(ang), jnp.cos(ang)], axis=-1)

    return jnp.concatenate([axis_emb(row), axis_emb(col)], axis=-1)


def kernel(x_prompt, x_sample, state_delta, c, c_ctx, w_mod, b_mod, norm_mix, norm_ffn, norm_final, dn_w_in, dn_conv, dn_a_log, dn_dt_bias, dn_norm, dn_w_out, hy_w_in, hy_conv, hy_w1, hy_b1, hy_w2, hy_b2, hy_w3, hy_b3, hy_w4, hy_freq, hy_bias, hy_w_out, moe_w_router, moe_b_router, moe_w_in, moe_b_in, moe_w_out, moe_b_out):
    Bp, Tp, D = x_prompt.shape
    Bs, Ts, _ = x_sample.shape
    depth = w_mod.shape[0]
    H = state_delta.shape[3]
    Mp, Ms = Bp * Tp, Bs * Ts
    M = Mp + Ms
    assert Mp % Ts == 0 and Ts % Tp == 0 and Tp % DN_CHUNK == 0, "token groups must tile each other"
    tT = Tp
    tm = min(512, Tp)

    xs = x_sample + _grid_pos_embedding(Ts, D)[None]
    x = jnp.concatenate([x_prompt.reshape(Mp, D), xs.reshape(Ms, D)], axis=0)
    tile_start = np.arange(M // tm) * tm
    rowmap = jnp.asarray(np.where(tile_start < Mp, 0, 1 + (tile_start - Mp) // Ts), I32)
    conv_start = np.arange(M // tT) * tT
    seq_len = np.where(conv_start < Mp, Tp, Ts)
    seq_pos = np.where(conv_start < Mp, conv_start % Tp, (conv_start - Mp) % Ts)
    has_prev = jnp.asarray(seq_pos > 0, I32)
    has_next = jnp.asarray(seq_pos + tT < seq_len, I32)

    n_cond = 1 + Bs
    r_pad = -(-n_cond // 8) * 8
    cond = jnp.zeros((r_pad, D), F32).at[0].set(c_ctx).at[1:n_cond].set(c)
    mod = _modulation(cond, w_mod, b_mod)

    new_states = []
    i_dn = i_hy = 0
    for l in range(depth):
        mod4 = mod[l].reshape(r_pad, 6, 1, D)
        if l % 2 == 0:
            i = i_dn
            i_dn += 1
            w_in = dn_w_in[i]
            n_main = 4 * H * HEAD_DIM
            w_ba = jnp.zeros((D, LANES), F32).at[:, :4 * H].set(w_in[:, n_main:]).astype(BF16)
            pm, ba = _fused_mm(_norm_mod_prologue, [x], [norm_mix[l]], [0, 1], w_in[:, :n_main].astype(BF16),
                               rowmap, mod4=mod4, side_w=w_ba, tm=tm, tn=1024)
            qkv, bg = _dn_act(pm, ba, dn_conv[i], dn_a_log[i], dn_dt_bias[i], has_prev, has_next, tT)
            hb = 2 if H % 2 == 0 else 1
            cols_p, gct_p = _delta_side_inputs(bg, 0, Bp, Tp, H, hb)
            cols_s, gct_s = _delta_side_inputs(bg, Mp, Bs, Ts, H, hb)
            og, s_fin = _delta(qkv, pm, cols_p, gct_p, dn_norm[i], None, None,
                               n_batch=Bp, batch_off=0, T=Tp, hb=hb)
            og, _ = _delta(qkv, pm, cols_s, gct_s, dn_norm[i], state_delta[:, i].astype(F32), og,
                           n_batch=Bs, batch_off=Mp // Ts, T=Ts, hb=hb)
            new_states.append(s_fin.astype(state_delta.dtype))
            x = _fused_mm(_identity_prologue, [og], [], [], dn_w_out[i].astype(BF16), rowmap, mod4=mod4,
                          res=x, gate_chunk=2, out_dtype=F32, tm=tm, tn=1024)
        else:
            i = i_hy
            i_hy += 1
            u = _fused_mm(_norm_mod_prologue, [x], [norm_mix[l]], [0, 1], hy_w_in[i].astype(BF16),
                          rowmap, mod4=mod4, tm=tm, tn=1024)
            x0, s = _hy_pre(u, hy_conv[i], has_prev, has_next, tT)
            hy = (hy_w1[i], hy_b1[i], hy_w2[i], hy_b2[i], hy_w3[i], hy_b3[i], hy_w4[i], hy_freq[i])
            conv = jnp.concatenate([
                _hyena_conv(s, Bp, 0, Tp, _hyena_filter_taps(Tp, *hy)),
                _hyena_conv(s, Bs, Mp, Ts, _hyena_filter_taps(Ts, *hy))], axis=0)
            x = _fused_mm(_hyena_gate_prologue, [x0, conv, s], [hy_bias[i]], [], hy_w_out[i].astype(BF16),
                          rowmap, mod4=mod4, res=x, gate_chunk=2, out_dtype=F32, tm=tm, tn=1024)
        y = _moe_layer(x, norm_ffn[l], mod4, rowmap, tm, moe_w_router[l], moe_b_router[l],
                       moe_w_in[l], moe_b_in[l], moe_w_out[l], moe_b_out[l])
        gate_f = mod[l, :n_cond, 5 * D:]
        x = jnp.concatenate([
            x[:Mp] + gate_f[0][None] * y[:Mp],
            (x[Mp:].reshape(Bs, Ts, D) + gate_f[1:, None] * y[Mp:].reshape(Bs, Ts, D)).reshape(Ms, D)], axis=0)

    y_prompt = _final_norm(x, norm_final, 0, Mp, tm).reshape(Bp, Tp, D)
    y_sample = _final_norm(x, norm_final, Mp, Ms, tm).reshape(Bs, Ts, D)
    return y_prompt, y_sample, jnp.stack(new_states, axis=1)
```

```python
import functools
import math

import numpy as np
import jax
import jax.numpy as jnp
from jax import lax
from jax.experimental import pallas as pl
from jax.experimental.pallas import tpu as pltpu

F32, BF16, I32 = jnp.float32, jnp.bfloat16, jnp.int32

V7X_VMEM_LIMIT_BYTES = 56 * 2**20
LANES = 128
BF16_ROWS = 16

GRID_W = 64
DN_CHUNK = 128
HEAD_DIM = 128
TOP_K = 4
RMS_EPS = 1e-6
L2_EPS = 1e-6
POS_BASE = 10000.0
SWIGLU_LIMIT = 7.0
SWIGLU_ALPHA = 1.702
HY_TARGET = 1e-2
HY_FAST = 0.3
HY_SLOW = 1.5

HIGHEST = lax.Precision.HIGHEST
NT_DIMS = (((1,), (1,)), ((), ()))
TN_DIMS = (((0,), (0,)), ((), ()))


def _params(*sem):
    return pltpu.CompilerParams(dimension_semantics=sem, vmem_limit_bytes=V7X_VMEM_LIMIT_BYTES)


def _tile(n, pref):
    if n <= pref:
        return n
    t = pref - pref % LANES
    while n % t:
        t -= LANES
    return t


def _silu(x):
    return x * jax.nn.sigmoid(x)


def _softplus(x):
    return jnp.maximum(x, 0.0) + jnp.log1p(jnp.exp(-jnp.abs(x)))


def _split_bf16(x):
    hi = x.astype(BF16)
    lo = (x - hi.astype(F32)).astype(BF16)
    return hi, lo


def _dot3(a, b):
    ah, al = _split_bf16(a)
    bh, bl = _split_bf16(b)
    return (jnp.dot(ah, bh, preferred_element_type=F32)
            + (jnp.dot(ah, bl, preferred_element_type=F32) + jnp.dot(al, bh, preferred_element_type=F32)))


def _modulation_kernel(c_ref, w_ref, b_ref, o_ref):
    s = _silu(c_ref[...]).astype(BF16)
    o_ref[0] = jnp.dot(s, w_ref[0].astype(BF16), preferred_element_type=F32) + b_ref[0]


def _modulation(cond, w_mod, b_mod):
    R, D = cond.shape
    L, _, N = w_mod.shape
    tn = _tile(N, 1024)
    return pl.pallas_call(
        _modulation_kernel,
        grid=(L, N // tn),
        in_specs=[pl.BlockSpec((R, D), lambda l, j: (0, 0)),
                  pl.BlockSpec((1, D, tn), lambda l, j: (l, 0, j)),
                  pl.BlockSpec((1, 1, tn), lambda l, j: (l, 0, j))],
        out_specs=pl.BlockSpec((1, R, tn), lambda l, j: (l, 0, j)),
        out_shape=jax.ShapeDtypeStruct((L, R, N), F32),
        compiler_params=_params("parallel", "parallel"),
        name="modulation",
    )(cond, w_mod, b_mod.reshape(L, 1, N))


def _fused_mm_kernel(prologue, n_a, n_v, n_m, has_res, has_side, rm_ref, *refs):
    del rm_ref
    a_refs, refs = refs[:n_a], refs[n_a:]
    v_refs, refs = refs[:n_v], refs[n_v:]
    m_refs, refs = refs[:n_m], refs[n_m:]
    w_ref, refs = refs[0], refs[1:]
    if has_res:
        res_ref, gate_ref, refs = refs[0], refs[1], refs[2:]
    if has_side:
        sw_ref, refs = refs[0], refs[1:]
    o_ref, refs = refs[0], refs[1:]
    if has_side:
        so_ref, refs = refs[0], refs[1:]
    a_scr = refs[0]

    row_tile = pl.program_id(0)

    @pl.when(pl.program_id(1) == 0)
    def _():
        a = prologue([r[...] for r in a_refs], [r[...] for r in v_refs], [r[0, 0] for r in m_refs], row_tile)
        a_scr[...] = a.astype(BF16)
        if has_side:
            so_ref[...] = jnp.dot(a_scr[...], sw_ref[...], preferred_element_type=F32)

    acc = jnp.dot(a_scr[...], w_ref[...], preferred_element_type=F32)
    if has_res:
        acc = res_ref[...] + gate_ref[0, 0] * acc
    o_ref[...] = acc.astype(o_ref.dtype)


def _fused_mm(prologue, a_ins, vec_ins, mod_ins, w, rowmap, *, mod4=None, res=None, gate_chunk=None,
              side_w=None, a_rest=None, out_dtype=BF16, tm, tn):
    a_split = None if a_rest is None else a_ins[0].shape[0]
    K = a_ins[0].shape[1]
    M = rowmap.shape[0] * tm
    N = w.shape[1]
    tn = _tile(N, tn)
    in_specs, args = [], []
    for a in a_ins:
        in_specs.append(pl.BlockSpec((tm, K), lambda i, j, rm: (i, 0)))
        args.append(a)
    if a_split is not None:
        n_first = a_split // tm
        in_specs[0] = pl.BlockSpec((tm, K), lambda i, j, rm: (jnp.minimum(i, n_first - 1), 0))
        in_specs.insert(1, pl.BlockSpec((tm, K), lambda i, j, rm: (jnp.maximum(i - n_first, 0), 0)))
        args.insert(1, a_rest)
    for v in vec_ins:
        in_specs.append(pl.BlockSpec((1, K), lambda i, j, rm: (0, 0)))
        args.append(v.reshape(1, K))
    for c in mod_ins:
        in_specs.append(pl.BlockSpec((1, 1, 1, K), lambda i, j, rm, c=c: (rm[i], c, 0, 0)))
        args.append(mod4)
    in_specs.append(pl.BlockSpec((K, tn), lambda i, j, rm: (0, j)))
    args.append(w)
    if res is not None:
        in_specs.append(pl.BlockSpec((tm, tn), lambda i, j, rm: (i, j)))
        args.append(res)
        in_specs.append(pl.BlockSpec((1, 1, 1, tn), lambda i, j, rm, c=gate_chunk: (rm[i], c, 0, j)))
        args.append(mod4)
    out_specs = [pl.BlockSpec((tm, tn), lambda i, j, rm: (i, j))]
    out_shape = [jax.ShapeDtypeStruct((M, N), out_dtype)]
    if side_w is not None:
        ns = side_w.shape[1]
        in_specs.append(pl.BlockSpec((K, ns), lambda i, j, rm: (0, 0)))
        args.append(side_w)
        out_specs.append(pl.BlockSpec((tm, ns), lambda i, j, rm: (i, 0)))
        out_shape.append(jax.ShapeDtypeStruct((M, ns), F32))
    kern = functools.partial(_fused_mm_kernel, prologue, len(a_ins) + (a_rest is not None), len(vec_ins),
                             len(mod_ins), res is not None, side_w is not None)
    outs = pl.pallas_call(
        kern,
        grid_spec=pltpu.PrefetchScalarGridSpec(
            num_scalar_prefetch=1, grid=(M // tm, N // tn), in_specs=in_specs, out_specs=out_specs,
            scratch_shapes=[pltpu.VMEM((tm, K), BF16)]),
        out_shape=out_shape,
        compiler_params=_params("parallel", "arbitrary"),
        name="fused_mm",
    )(rowmap, *args)
    return outs if side_w is not None else outs[0]


def _norm_mod_prologue(a, v, m, row_tile=None):
    x, g, (shift, scale) = a[0], v[0], m
    y = x * lax.rsqrt(jnp.mean(x * x, axis=-1, keepdims=True) + RMS_EPS) * g
    return y * (1.0 + scale) + shift


def _two_part_prologue(n_first_tiles, a, v, m, row_tile):
    return jnp.where(row_tile < n_first_tiles, a[0], a[1])


def _hyena_gate_prologue(n_first_tiles, a, v, m, row_tile):
    conv = jnp.where(row_tile < n_first_tiles, a[0], a[1]).astype(F32)
    x0, s = a[2].astype(F32), a[3].astype(F32)
    return x0 * (conv + s * v[0])


def _conv3(x, prev_row, next_row, w, rid):
    t = x.shape[0]
    xm = jnp.where(rid == 0, prev_row, pltpu.roll(x, 1, 0))
    xp = jnp.where(rid == t - 1, next_row, pltpu.roll(x, t - 1, 0))
    return xm * w[0:1] + x * w[1:2] + xp * w[2:3]


def _halo_specs(tT, width, col_block):
    hb = tT // BF16_ROWS
    return [
        pl.BlockSpec((tT, width), lambda i, hp, hn: (i, col_block)),
        pl.BlockSpec((BF16_ROWS, width), lambda i, hp, hn: (jnp.maximum(i * hb - 1, 0), col_block)),
        pl.BlockSpec((BF16_ROWS, width), lambda i, hp, hn: ((i + 1) * hb * hn[i], col_block)),
    ]


def _dn_act_kernel(n_heads, hp_ref, hn_ref, x_ref, xp_ref, xn_ref, ba_ref, cw_ref, al_ref, dtb_ref,
                   qkv_ref, bg_ref):
    i = pl.program_id(0)
    hp = hp_ref[i].astype(F32)
    hn = hn_ref[i].astype(F32)
    tT = x_ref.shape[0]
    rid = lax.broadcasted_iota(I32, (tT, LANES), 0)
    for cb in range(3 * n_heads):
        sl = slice(cb * LANES, (cb + 1) * LANES)
        x = x_ref[:, sl].astype(F32)
        pr = xp_ref[:, sl].astype(F32)[BF16_ROWS - 1:BF16_ROWS] * hp
        nx = xn_ref[:, sl].astype(F32)[0:1] * hn
        y = _silu(_conv3(x, pr, nx, cw_ref[:, sl], rid))
        if cb < 2 * n_heads:
            y = y * lax.rsqrt(jnp.sum(y * y, axis=-1, keepdims=True) + L2_EPS)
            if cb < n_heads:
                y = y * (HEAD_DIM ** -0.5)
        qkv_ref[:, sl] = y.astype(BF16)

    ba = ba_ref[...]
    lane = lax.broadcasted_iota(I32, (tT, LANES), 1)
    beta = jax.nn.sigmoid(ba)
    g = -jnp.exp(al_ref[...]) * _softplus(ba + dtb_ref[...])
    r = lax.broadcasted_iota(I32, (tT, tT), 0)
    c = lax.broadcasted_iota(I32, (tT, tT), 1)
    shift = DN_CHUNK.bit_length() - 1
    same = jnp.right_shift(r, shift) == jnp.right_shift(c, shift)
    l_pre = jnp.where(same & (c <= r), 1.0, 0.0).astype(F32)
    l_suf = jnp.where(same & (c >= r), 1.0, 0.0).astype(F32)
    g_pre = jnp.dot(l_pre, g, precision=HIGHEST, preferred_element_type=F32)
    g_suf = jnp.dot(l_suf, g, precision=HIGHEST, preferred_element_type=F32)
    gc = jnp.where(lane < 3 * n_heads, g_pre, g_suf)
    bg_ref[...] = jnp.where(lane < 2 * n_heads, beta, gc)


def _dn_act(pm, ba, conv_w, a_log, dt_bias, has_prev, has_next, tT):
    M = pm.shape[0]
    cw = conv_w.shape[1]
    n_heads = cw // (3 * HEAD_DIM)
    pad = lambda v: jnp.zeros((1, LANES), F32).at[0, 2 * n_heads:4 * n_heads].set(v.reshape(-1).astype(F32))
    vec = pl.BlockSpec((1, LANES), lambda i, hp, hn: (0, 0))
    return pl.pallas_call(
        functools.partial(_dn_act_kernel, n_heads),
        grid_spec=pltpu.PrefetchScalarGridSpec(
            num_scalar_prefetch=2, grid=(M // tT,),
            in_specs=_halo_specs(tT, cw, 0) + [
                pl.BlockSpec((tT, LANES), lambda i, hp, hn: (i, 0)),
                pl.BlockSpec((3, cw), lambda i, hp, hn: (0, 0)), vec, vec],
            out_specs=[pl.BlockSpec((tT, cw), lambda i, hp, hn: (i, 0)),
                       pl.BlockSpec((tT, LANES), lambda i, hp, hn: (i, 0))]),
        out_shape=[jax.ShapeDtypeStruct((M, cw), BF16), jax.ShapeDtypeStruct((M, LANES), F32)],
        compiler_params=_params("parallel"),
        name="dn_act",
    )(has_prev, has_next, pm, pm, pm, ba, conv_w, pad(a_log), pad(dt_bias))


def _merge_masks(ri, ci, n, lower):
    masks = []
    s = 0
    while (1 << s) < n:
        same = jnp.right_shift(ri, s + 1) == jnp.right_shift(ci, s + 1)
        hi_r = jnp.bitwise_and(jnp.right_shift(ri, s), 1)
        hi_c = jnp.bitwise_and(jnp.right_shift(ci, s), 1)
        off = (hi_r == 1) & (hi_c == 0) if lower else (hi_r == 0) & (hi_c == 1)
        masks.append(same & off)
        s += 1
    return masks


def _dot3_many(xs, ys):
    sx = [_split_bf16(x) for x in xs]
    sy = [_split_bf16(y) for y in ys]
    hh = [jnp.dot(x[0], y[0], preferred_element_type=F32) for x, y in zip(sx, sy)]
    hl = [jnp.dot(x[0], y[1], preferred_element_type=F32) for x, y in zip(sx, sy)]
    lh = [jnp.dot(x[1], y[0], preferred_element_type=F32) for x, y in zip(sx, sy)]
    return [a + (b + c) for a, b, c in zip(hh, hl, lh)]


def _dot1_many(xs, ys):
    return [jnp.dot(x.astype(BF16), y.astype(BF16), preferred_element_type=F32) for x, y in zip(xs, ys)]


def _unit_tri_inverse_many(mats, eye, masks, stricts):
    ts = [eye - jnp.where(m[0], a, 0.0) for a, m in zip(mats, masks)]
    for lvl in range(1, len(masks[0])):
        off = [jnp.where(m[lvl], a, 0.0) for a, m in zip(mats, masks)]
        upd = _dot1_many(_dot1_many(ts, off), ts)
        ts = [t - u for t, u in zip(ts, upd)]
    full = [eye + jnp.where(st, a, 0.0) for a, st in zip(mats, stricts)]
    res = [eye - p for p in _dot3_many(full, ts)]
    return [t + c for t, c in zip(ts, _dot1_many(ts, res))]


def _delta_kernel(hb, n_chunks, group, has_s0, *refs):
    if has_s0:
        q_ref, k_ref, v_ref, z_ref, cols_ref, gct_ref, ng_ref, s0_ref, o_ref, sfin_ref = refs[:10]
        scr = refs[10:]
    else:
        q_ref, k_ref, v_ref, z_ref, cols_ref, gct_ref, ng_ref, o_ref, sfin_ref = refs[:9]
        scr = refs[9:]
    u_scr, wq_scr, qk_scr, s_scr, o_scr = scr
    C = DN_CHUNK
    n = n_chunks
    ri = lax.broadcasted_iota(I32, (C, C), 0)
    ci = lax.broadcasted_iota(I32, (C, C), 1)
    eye = (ri == ci).astype(F32)
    incl = (ri >= ci, ri <= ci)

    def gate_cols(hh, d, rows):
        bcol = cols_ref[0, 0, rows, hh * 4 + d:hh * 4 + d + 1]
        gcol = cols_ref[0, 0, rows, hh * 4 + 2 + d:hh * 4 + 3 + d]
        gl = gcol[C - 1:C, :] if d == 0 else gcol[0:1, :]
        return bcol, gcol, gl

    def prep(it, carry):
        tri_masks = (_merge_masks(ri, ci, C, True), _merge_masks(ri, ci, C, False))
        tri_strict = (ri > ci, ri < ci)
        where_, mats, masks, stricts, rhss = [], [], [], [], []
        for gi in range(group):
            c = it * group + gi
            rows = pl.ds(pl.multiple_of(c * C, C), C)
            for hh in range(hb):
                ls = slice(hh * HEAD_DIM, (hh + 1) * HEAD_DIM)
                qb, kb16, vb = q_ref[0, rows, ls], k_ref[0, rows, ls], v_ref[0, rows, ls]
                qf, kf, vf = qb.astype(F32), kb16.astype(F32), vb.astype(F32)
                qkt = lax.dot_general(qb, kb16, NT_DIMS, preferred_element_type=F32)
                for d in range(2):
                    ch = hh * 2 + d
                    bcol, gcol, _ = gate_cols(hh, d, rows)
                    grow = gct_ref[0, hh, d, pl.ds(c, 1), :]
                    dm = jnp.where(incl[d], jnp.exp(jnp.where(incl[d], gcol - grow, 0.0)), 0.0)
                    kbeta = kf * bcol
                    eg = jnp.exp(gcol)
                    mats.append(lax.dot_general(kbeta.astype(BF16), kb16, NT_DIMS, preferred_element_type=F32) * dm)
                    masks.append(tri_masks[d])
                    stricts.append(tri_strict[d])
                    rhss.append(jnp.concatenate([vf * bcol, kbeta * eg], axis=1))
                    where_.append((ch, c, rows))
                    wq_scr[ch, pl.ds(pl.multiple_of(c * 2 * C + C, C), C), :] = (qf * eg).astype(BF16)
                    qk_scr[ch, rows, :] = (qkt * dm).astype(BF16)
        sols = _dot3_many(_unit_tri_inverse_many(mats, eye, masks, stricts), rhss)
        for (ch, c, rows), sol in zip(where_, sols):
            u_scr[ch, rows, :] = sol[:, :HEAD_DIM]
            wq_scr[ch, pl.ds(pl.multiple_of(c * 2 * C, 2 * C), C), :] = sol[:, HEAD_DIM:].astype(BF16)
        return carry

    lax.fori_loop(0, n // group, prep, 0)

    for hh in range(hb):
        for d in range(2):
            s_scr[hh * 2 + d] = s0_ref[0, d, hh] if has_s0 else jnp.zeros((HEAD_DIM, HEAD_DIM), F32)

    T = n * C
    tr = min(T, 256)

    def clear(b, carry):
        o_scr[:, pl.ds(pl.multiple_of(b * tr, tr), tr), :] = jnp.zeros((hb, tr, HEAD_DIM), F32)
        return carry

    lax.fori_loop(0, T // tr, clear, 0)

    def scan(it, carry):
        chains = [(hh, d) for hh in range(hb) for d in range(2)]
        cs = [it if d == 0 else n - 1 - it for _, d in chains]
        rows = [pl.ds(pl.multiple_of(c * C, C), C) for c in cs]
        ss = [s_scr[hh * 2 + d] for hh, d in chains]
        rs = [jnp.dot(wq_scr[hh * 2 + d, pl.ds(pl.multiple_of(c * 2 * C, 2 * C), 2 * C), :], s.astype(BF16),
                      preferred_element_type=F32) for (hh, d), c, s in zip(chains, cs, ss)]
        vns = [(u_scr[hh * 2 + d, rw, :] - r[:C]).astype(BF16) for (hh, d), rw, r in zip(chains, rows, rs)]
        for i, (hh, d) in enumerate(chains):
            _, gcol, gl = gate_cols(hh, d, rows[i])
            kt = (k_ref[0, rows[i], hh * HEAD_DIM:(hh + 1) * HEAD_DIM].astype(F32) * jnp.exp(gl - gcol)).astype(BF16)
            s_scr[hh * 2 + d] = (ss[i] * jnp.exp(gl)
                                 + lax.dot_general(kt, vns[i], TN_DIMS, preferred_element_type=F32))
        for i, (hh, d) in enumerate(chains):
            o = rs[i][C:] + jnp.dot(qk_scr[hh * 2 + d, rows[i], :], vns[i], preferred_element_type=F32)
            o_scr[hh, rows[i], :] += o
        return carry

    lax.fori_loop(0, n, scan, 0)

    for hh in range(hb):
        for d in range(2):
            sfin_ref[0, d, hh] = s_scr[hh * 2 + d]

    def gate(b, carry):
        rows = pl.ds(pl.multiple_of(b * tr, tr), tr)
        for hh in range(hb):
            ls = slice(hh * HEAD_DIM, (hh + 1) * HEAD_DIM)
            o = o_scr[hh, rows, :]
            o = o * lax.rsqrt(jnp.mean(o * o, axis=-1, keepdims=True) + RMS_EPS) * ng_ref[...]
            o_ref[0, rows, ls] = (o * _silu(z_ref[0, rows, ls].astype(F32))).astype(BF16)
        return carry

    lax.fori_loop(0, T // tr, gate, 0)


def _delta(qkv, pm, cols, gct, norm_g, s0, *, n_batch, batch_off, T, hb):
    M, cw = qkv.shape
    H = cw // (3 * HEAD_DIM)
    n = T // DN_CHUNK
    W = hb * HEAD_DIM
    nb = H // hb
    qkv3 = qkv.reshape(M // T, T, cw)
    pm3 = pm.reshape(M // T, T, pm.shape[1])
    once = pl.Buffered(1)
    blk = lambda off: pl.BlockSpec((1, T, W), lambda b, j, off=off: (b + batch_off, 0, off * nb + j),
                                   pipeline_mode=once)
    in_specs = [blk(0), blk(1), blk(2), blk(3),
                pl.BlockSpec((1, 1, T, 4 * hb), lambda b, j: (b, j, 0, 0), pipeline_mode=once),
                pl.BlockSpec((1, hb, 2, n, DN_CHUNK), lambda b, j: (b, j, 0, 0, 0)),
                pl.BlockSpec((1, HEAD_DIM), lambda b, j: (0, 0))]
    args = [qkv3, qkv3, qkv3, pm3, cols, gct, norm_g.reshape(1, HEAD_DIM)]
    if s0 is not None:
        in_specs.append(pl.BlockSpec((1, 2, hb, HEAD_DIM, HEAD_DIM), lambda b, j: (b, 0, j, 0, 0)))
        args.append(s0)
    nch = 2 * hb
    group = math.gcd(n, max(1, 8 // nch))
    out, sfin = pl.pallas_call(
        functools.partial(_delta_kernel, hb, n, group, s0 is not None),
        grid=(n_batch, nb),
        in_specs=in_specs,
        out_specs=[pl.BlockSpec((1, T, W), lambda b, j: (b, 0, j)),
                   pl.BlockSpec((1, 2, hb, HEAD_DIM, HEAD_DIM), lambda b, j: (b, 0, j, 0, 0))],
        out_shape=[jax.ShapeDtypeStruct((n_batch, T, H * HEAD_DIM), BF16),
                   jax.ShapeDtypeStruct((n_batch, 2, H, HEAD_DIM, HEAD_DIM), F32)],
        scratch_shapes=[pltpu.VMEM((nch, T, HEAD_DIM), F32),
                        pltpu.VMEM((nch, 2 * T, HEAD_DIM), BF16),
                        pltpu.VMEM((nch, T, DN_CHUNK), BF16),
                        pltpu.VMEM((nch, HEAD_DIM, HEAD_DIM), F32),
                        pltpu.VMEM((hb, T, HEAD_DIM), F32)],
        compiler_params=_params("parallel", "parallel"),
        name="delta_rule",
    )(*args)
    return out.reshape(n_batch * T, H * HEAD_DIM), sfin


def _delta_side_inputs(bg, t0, n_batch, T, H, hb):
    b5 = bg[t0:t0 + n_batch * T, :4 * H].reshape(n_batch, T, 2, 2, H)
    cols = jnp.transpose(b5, (0, 4, 1, 2, 3)).reshape(n_batch, H // hb, hb, T, 4)
    cols = jnp.transpose(cols, (0, 1, 3, 2, 4)).reshape(n_batch, H // hb, T, 4 * hb)
    gct = jnp.transpose(b5[:, :, 1], (0, 3, 2, 1)).reshape(n_batch, H, 2, T // DN_CHUNK, DN_CHUNK)
    return cols, gct


def _hy_pre_kernel(D, hp_ref, hn_ref, x_ref, xp_ref, xn_ref, cw_ref, x0_ref, s_ref):
    i = pl.program_id(0)
    hp = hp_ref[i].astype(F32)
    hn = hn_ref[i].astype(F32)
    tT = x_ref.shape[0]
    W = min(D, 2 * LANES)
    rid = lax.broadcasted_iota(I32, (tT, W), 0)

    def conv(cb, part):
        sl = slice(part * D + cb * W, part * D + (cb + 1) * W)
        x = x_ref[:, sl].astype(F32)
        pr = xp_ref[:, sl].astype(F32)[BF16_ROWS - 1:BF16_ROWS] * hp
        nx = xn_ref[:, sl].astype(F32)[0:1] * hn
        return _conv3(x, pr, nx, cw_ref[:, sl], rid)

    for cb in range(D // W):
        sl = slice(cb * W, (cb + 1) * W)
        x0_ref[:, sl] = conv(cb, 0).astype(BF16)
        s_ref[:, sl] = (conv(cb, 1) * conv(cb, 2)).astype(BF16)


def _hy_pre(u, conv_w, has_prev, has_next, tT):
    M, W3 = u.shape
    D = W3 // 3
    out = pl.BlockSpec((tT, D), lambda i, hp, hn: (i, 0))
    return pl.pallas_call(
        functools.partial(_hy_pre_kernel, D),
        grid_spec=pltpu.PrefetchScalarGridSpec(
            num_scalar_prefetch=2, grid=(M // tT,),
            in_specs=_halo_specs(tT, W3, 0) + [pl.BlockSpec((3, W3), lambda i, hp, hn: (0, 0))],
            out_specs=[out, out]),
        out_shape=[jax.ShapeDtypeStruct((M, D), BF16)] * 2,
        compiler_params=_params("parallel"),
        name="hy_pre",
    )(has_prev, has_next, u, u, u, conv_w)


def _bmm_kernel(a_ref, b_ref, o_ref, acc_ref):
    k = pl.program_id(3)
    part = jnp.dot(a_ref[...], b_ref[0], preferred_element_type=F32)

    @pl.when(k == 0)
    def _():
        acc_ref[...] = part

    @pl.when(k > 0)
    def _():
        acc_ref[...] += part

    @pl.when(k == pl.num_programs(3) - 1)
    def _():
        o_ref[0] = acc_ref[...].astype(o_ref.dtype)


def _bmm(a, b, b_batch_off=0, n_batch=None, out_dtype=BF16, tm=1024, tn=1024, tk=2048):
    M, K = a.shape
    N = b.shape[2]
    nb = b.shape[0] if n_batch is None else n_batch
    tm, tn, tk = _tile(M, tm), _tile(N, tn), _tile(K, tk)
    return pl.pallas_call(
        _bmm_kernel,
        grid=(nb, M // tm, N // tn, K // tk),
        in_specs=[pl.BlockSpec((tm, tk), lambda i, m, n, k: (m, k)),
                  pl.BlockSpec((1, tk, tn), lambda i, m, n, k: (i + b_batch_off, k, n))],
        out_specs=pl.BlockSpec((1, tm, tn), lambda i, m, n, k: (i, m, n)),
        out_shape=jax.ShapeDtypeStruct((nb, M, N), out_dtype),
        scratch_shapes=[pltpu.VMEM((tm, tn), F32)],
        compiler_params=_params("parallel", "parallel", "parallel", "arbitrary"),
        name="dft_mm",
    )(a, b)


def _spec_prod_kernel(inv_n, s_ref, k_ref, y_ref):
    sc, ss = s_ref[0, 0].astype(F32), s_ref[0, 1].astype(F32)
    kc, ks = k_ref[0], k_ref[1]
    first = (lax.broadcasted_iota(I32, sc.shape, 0) == 0) & (pl.program_id(1) == 0)
    y_ref[0, 0] = (jnp.where(first, sc * kc, 2.0 * (sc * kc - ss * ks)) * inv_n).astype(y_ref.dtype)
    y_ref[0, 1] = (jnp.where(first, ss * ks, 2.0 * (sc * ks + ss * kc)) * inv_n).astype(y_ref.dtype)


def _spec_prod(sf, kf):
    B, _, T, D = sf.shape
    tr, tc = _tile(T, 256), _tile(D, 1024)
    return pl.pallas_call(
        functools.partial(_spec_prod_kernel, 1.0 / (2 * T)),
        grid=(B, T // tr, D // tc),
        in_specs=[pl.BlockSpec((1, 2, tr, tc), lambda b, i, j: (b, 0, i, j)),
                  pl.BlockSpec((2, tr, tc), lambda b, i, j: (0, i, j))],
        out_specs=pl.BlockSpec((1, 2, tr, tc), lambda b, i, j: (b, 0, i, j)),
        out_shape=jax.ShapeDtypeStruct(sf.shape, BF16),
        compiler_params=_params("parallel", "parallel", "parallel"),
        name="spec_prod",
    )(sf, kf)


def _dft_matrix(T):
    k = jnp.arange(T, dtype=I32)[:, None]
    t = jnp.arange(T, dtype=I32)[None, :]
    ang = ((k * t) % (2 * T)).astype(F32) * (math.pi / T)
    nyq = jnp.where(t % 2 == 0, 1.0, -1.0).astype(F32)
    sin = jnp.where(k == 0, nyq, jnp.sin(ang))
    return jnp.concatenate([jnp.cos(ang), sin], axis=0).astype(BF16)


def _hyena_filter_taps(L, w1, b1, w2, b2, w3, b3, w4, freq):
    D = w4.shape[1] // 2
    n_bands = (w1.shape[0] - 1) // 2
    pos = jnp.arange(L, dtype=F32)
    t = pos / max(L - 1, 1)
    bands = jnp.linspace(1e-4, n_bands - 1, n_bands, dtype=F32)
    ang = (2.0 * math.pi / L) * pos[:, None] * bands[None]
    feats = jnp.concatenate([t[:, None], jnp.cos(ang), -jnp.sin(ang)], axis=-1)
    zf = jnp.sin(freq[0] * (feats @ w1 + b1))
    zf = jnp.sin(freq[1] * (zf @ w2 + b2))
    zf = jnp.sin(freq[2] * (zf @ w3 + b3))
    filt = (zf @ w4).reshape(L, 2, D)
    deltas = jnp.abs(jnp.linspace(math.log(HY_TARGET) / HY_SLOW, math.log(HY_TARGET) / HY_FAST, D, dtype=F32))
    filt = filt * jnp.exp(-t[:, None, None] * deltas[None, None])
    f = filt[:, 0]
    b = filt[:, 1].at[0].set(0.0)
    scale = lax.rsqrt(jnp.sum(f * f, axis=0) + jnp.sum(b * b, axis=0) + 1e-6)
    return f * scale, b * scale


def _hyena_conv(s, n_batch, t0, T, taps):
    M, D = s.shape
    f, b = taps
    fwd = _dft_matrix(T)
    kf = _bmm(fwd, jnp.concatenate([f, b], axis=1).astype(BF16)[None], out_dtype=F32)[0]
    p, q = kf[:, :D].reshape(2, T, D), kf[:, D:].reshape(2, T, D)
    first = (jnp.arange(T) == 0)[:, None]
    kspec = jnp.stack([p[0] + q[0], jnp.where(first, p[1] + q[1], p[1] - q[1])])
    s3 = s.reshape(M // T, T, D)
    sf = _bmm(fwd, s3, b_batch_off=t0 // T, n_batch=n_batch)
    y = _spec_prod(sf.reshape(n_batch, 2, T, D), kspec).reshape(n_batch, 2 * T, D)
    conv = _bmm(fwd.T, y)
    return conv.reshape(n_batch * T, D)


def _router_kernel(n_experts, rm_ref, x_ref, g_ref, sh_ref, sc_ref, wr_ref, br_ref,
                   h_ref, ti_ref, tg_ref, rk_ref, cnt_ref, base_scr):
    del rm_ref

    @pl.when(pl.program_id(0) == 0)
    def _():
        base_scr[...] = jnp.zeros(base_scr.shape, F32)

    h = _norm_mod_prologue([x_ref[...]], [g_ref[...]], (sh_ref[0, 0], sc_ref[0, 0]))
    h_ref[...] = h
    logits = jnp.dot(h, wr_ref[...], precision=HIGHEST, preferred_element_type=F32) + br_ref[...]
    tm = logits.shape[0]
    lane = lax.broadcasted_iota(I32, logits.shape, 1)
    lane_f = lane.astype(F32)
    neg = jnp.float32(-jnp.inf)
    l = jnp.where(lane < n_experts, logits, neg)
    ti = jnp.zeros(logits.shape, I32)
    tl = jnp.full(logits.shape, neg, F32)
    onehot = jnp.zeros(logits.shape, F32)
    picks = []
    for r in range(TOP_K):
        m = jnp.max(l, axis=-1, keepdims=True)
        idx = jnp.min(jnp.where(l == m, lane_f, float(LANES)), axis=-1, keepdims=True).astype(I32)
        ti = jnp.where(lane == r, idx, ti)
        tl = jnp.where(lane == r, m, tl)
        l = jnp.where(lane == idx, neg, l)
        onehot = jnp.where(lane == idx, 1.0, onehot)
        picks.append(idx)
    e = jnp.exp(tl - jnp.max(tl, axis=-1, keepdims=True))
    ti_ref[...] = ti
    tg_ref[...] = e / jnp.sum(e, axis=-1, keepdims=True)

    r_i = lax.broadcasted_iota(I32, (tm, tm), 0)
    c_i = lax.broadcasted_iota(I32, (tm, tm), 1)
    earlier = jnp.where(c_i < r_i, 1.0, 0.0).astype(BF16)
    before = jnp.dot(earlier, onehot.astype(BF16), preferred_element_type=F32) + base_scr[...]
    rk = jnp.zeros(logits.shape, F32)
    for r in range(TOP_K):
        mine = jnp.sum(jnp.where(lane == picks[r], before, 0.0), axis=-1, keepdims=True)
        rk = jnp.where(lane == r, mine, rk)
    rk_ref[...] = rk.astype(I32)
    base_scr[...] += jnp.sum(onehot, axis=0, keepdims=True)
    cnt_ref[...] = base_scr[...].astype(I32)


def _router(x, g, mod4, rowmap, w_router, b_router, tm):
    M, D = x.shape
    E = w_router.shape[1]
    wr = jnp.zeros((D, LANES), F32).at[:, :E].set(w_router)
    br = jnp.zeros((1, LANES), F32).at[0, :E].set(b_router)
    row = lambda c: pl.BlockSpec((1, 1, 1, D), lambda i, rm, c=c: (rm[i], c, 0, 0))
    tile = lambda w: pl.BlockSpec((tm, w), lambda i, rm: (i, 0))
    return pl.pallas_call(
        functools.partial(_router_kernel, E),
        grid_spec=pltpu.PrefetchScalarGridSpec(
            num_scalar_prefetch=1, grid=(M // tm,),
            in_specs=[tile(D), pl.BlockSpec((1, D), lambda i, rm: (0, 0)), row(3), row(4),
                      pl.BlockSpec((D, LANES), lambda i, rm: (0, 0)),
                      pl.BlockSpec((1, LANES), lambda i, rm: (0, 0))],
            out_specs=[tile(D), tile(LANES), tile(LANES), tile(LANES),
                       pl.BlockSpec((1, LANES), lambda i, rm: (0, 0))],
            scratch_shapes=[pltpu.VMEM((1, LANES), F32)]),
        out_shape=[jax.ShapeDtypeStruct((M, D), F32), jax.ShapeDtypeStruct((M, LANES), I32),
                   jax.ShapeDtypeStruct((M, LANES), F32), jax.ShapeDtypeStruct((M, LANES), I32),
                   jax.ShapeDtypeStruct((1, LANES), I32)],
        compiler_params=_params("arbitrary"),
        name="router",
    )(rowmap, x, g.reshape(1, D), mod4, mod4, wr, br)


def _dest_kernel(ps_ref, ti_ref, rk_ref, o_ref):
    ti = ti_ref[...]
    start = jnp.zeros(ti.shape, I32)
    for e in range(ps_ref.shape[0]):
        start = jnp.where(ti == e, ps_ref[e], start)
    o_ref[...] = start + rk_ref[...]


def _dest_rows(seg_start, ti, rk, tm):
    M = ti.shape[0]
    tile = pl.BlockSpec((tm, LANES), lambda i, ps: (i, 0))
    return pl.pallas_call(
        _dest_kernel,
        grid_spec=pltpu.PrefetchScalarGridSpec(num_scalar_prefetch=1, grid=(M // tm,),
                                               in_specs=[tile, tile], out_specs=tile),
        out_shape=jax.ShapeDtypeStruct((M, LANES), I32),
        compiler_params=_params("parallel"),
        name="moe_dest",
    )(seg_start, ti, rk)


def _load_tile_dest(dest_ref, dest_smem, sem):
    cp = pltpu.make_async_copy(dest_ref, dest_smem, sem)
    cp.start()
    cp.wait()


def _zero_pad_rows(pad_start_ref, pad_len_ref, o_ref, zero_ref, sem, start):
    def copy(pos, p):
        cp = pltpu.make_async_copy(zero_ref.at[pl.ds(0, p)], o_ref.at[pl.ds(pos, p)], sem)
        if start:
            cp.start()
        else:
            cp.wait()

    def per_expert(e, carry):
        length = pad_len_ref[e]
        pos = pad_start_ref[e] + length
        p = zero_ref.shape[0] // 2
        while p >= 1:
            hit = jnp.bitwise_and(length, p) != 0
            pos = pos - jnp.where(hit, p, 0)

            @pl.when(hit)
            def _(pos=pos, p=p):
                if p >= 8:
                    copy(pl.multiple_of(pos, 8), p)
                else:
                    for q in range(p):
                        copy(pos + q, 1)

            p //= 2
        return carry

    lax.fori_loop(0, pad_start_ref.shape[0], per_expert, 0)


def _dispatch_kernel(pad_start_ref, pad_len_ref, tail_ref, dest_ref, h_ref, o_ref, dest_smem, zero_ref, sem):
    tm = h_ref.shape[0]
    tz = zero_ref.shape[0]

    @pl.when(pl.program_id(0) == 0)
    def _():
        zero_ref[...] = jnp.zeros(zero_ref.shape, F32)

        def tail_copy(t):
            return pltpu.make_async_copy(zero_ref, o_ref.at[pl.ds(pl.multiple_of(t * tz, tz), tz)], sem.at[2])

        def tail_start(t, carry):
            tail_copy(t).start()
            return carry

        def tail_wait(t, carry):
            tail_copy(t).wait()
            return carry

        _zero_pad_rows(pad_start_ref, pad_len_ref, o_ref, zero_ref, sem.at[2], True)
        lax.fori_loop(tail_ref[0], tail_ref[1], tail_start, 0)
        _zero_pad_rows(pad_start_ref, pad_len_ref, o_ref, zero_ref, sem.at[2], False)
        lax.fori_loop(tail_ref[0], tail_ref[1], tail_wait, 0)

    _load_tile_dest(dest_ref, dest_smem, sem.at[0])

    def row_copy(j):
        dst = dest_smem[jnp.right_shift(j, 7), jnp.bitwise_and(j, LANES - 1)]
        return pltpu.make_async_copy(h_ref.at[pl.ds(jnp.right_shift(j, 2), 1)], o_ref.at[pl.ds(dst, 1)], sem.at[1])

    def start(j, carry):
        row_copy(j).start()
        return carry

    def wait(j, carry):
        row_copy(j).wait()
        return carry

    lax.fori_loop(0, tm * TOP_K, start, 0)
    lax.fori_loop(0, tm * TOP_K, wait, 0)


def _dispatch(h, dest2d, pad_start, pad_len, tail, n_rows, tm, tm_rows):
    M, D = h.shape
    nd = tm * TOP_K // LANES
    return pl.pallas_call(
        _dispatch_kernel,
        grid_spec=pltpu.PrefetchScalarGridSpec(
            num_scalar_prefetch=3, grid=(M // tm,),
            in_specs=[pl.BlockSpec((nd, LANES), lambda i, ps, pn, tl: (i, 0)),
                      pl.BlockSpec((tm, D), lambda i, ps, pn, tl: (i, 0))],
            out_specs=pl.BlockSpec(memory_space=pl.ANY),
            scratch_shapes=[pltpu.SMEM((nd, LANES), I32), pltpu.VMEM((tm_rows, D), F32),
                            pltpu.SemaphoreType.DMA((3,))]),
        out_shape=jax.ShapeDtypeStruct((n_rows, D), F32),
        compiler_params=_params("arbitrary"),
        name="moe_dispatch",
    )(pad_start, pad_len, tail, dest2d, h)


def _combine_kernel(final, rm_ref, dest_ref, y_ref, tg_ref, x_ref, gate_ref, ng_ref, *refs):
    del rm_ref
    n_out = 1 if final is None else 2
    o_ref = refs[0] if final is None else refs[:2]
    buf, dest_smem, sem = refs[n_out:]
    tm = x_ref.shape[0]
    _load_tile_dest(dest_ref, dest_smem, sem.at[0])

    def row_copy(j):
        src = dest_smem[jnp.right_shift(j, 7), jnp.bitwise_and(j, LANES - 1)]
        return pltpu.make_async_copy(y_ref.at[pl.ds(src, 1)],
                                     buf.at[jnp.bitwise_and(j, TOP_K - 1), pl.ds(jnp.right_shift(j, 2), 1)], sem.at[1])

    def start(j, carry):
        row_copy(j).start()
        return carry

    def wait(j, carry):
        row_copy(j).wait()
        return carry

    lax.fori_loop(0, tm * TOP_K, start, 0)
    lax.fori_loop(0, tm * TOP_K, wait, 0)
    tg = tg_ref[...]
    acc = tg[:, 0:1] * buf[0]
    for k in range(1, TOP_K):
        acc = acc + tg[:, k:k + 1] * buf[k]
    x = x_ref[...] + gate_ref[0, 0] * acc
    if final is None:
        o_ref[...] = x
    else:
        x = x * lax.rsqrt(jnp.mean(x * x, axis=-1, keepdims=True) + RMS_EPS) * ng_ref[...]
        first_ref, second_ref = o_ref

        @pl.when(pl.program_id(0) < final)
        def _():
            first_ref[...] = x

        @pl.when(pl.program_id(0) >= final)
        def _():
            second_ref[...] = x


def _combine(y_rows, dest2d, tg, x, mod4, rowmap, norm_g, split_rows, tm):
    M, D = x.shape
    nd = tm * TOP_K // LANES
    if split_rows is None:
        final = None
        out_specs = pl.BlockSpec((tm, D), lambda i, rm: (i, 0))
        out_shape = jax.ShapeDtypeStruct((M, D), F32)
    else:
        final = split_rows // tm
        out_specs = [pl.BlockSpec((tm, D), lambda i, rm: (jnp.minimum(i, final - 1), 0)),
                     pl.BlockSpec((tm, D), lambda i, rm: (jnp.maximum(i - final, 0), 0))]
        out_shape = [jax.ShapeDtypeStruct((split_rows, D), F32), jax.ShapeDtypeStruct((M - split_rows, D), F32)]
    return _combine_call(final, out_specs, out_shape, y_rows, dest2d, tg, x, mod4, rowmap, norm_g, tm, nd)


def _combine_call(final, out_specs, out_shape, y_rows, dest2d, tg, x, mod4, rowmap, norm_g, tm, nd):
    M, D = x.shape
    return pl.pallas_call(
        functools.partial(_combine_kernel, final),
        grid_spec=pltpu.PrefetchScalarGridSpec(
            num_scalar_prefetch=1, grid=(M // tm,),
            in_specs=[pl.BlockSpec((nd, LANES), lambda i, rm: (i, 0)),
                      pl.BlockSpec(memory_space=pl.ANY),
                      pl.BlockSpec((tm, LANES), lambda i, rm: (i, 0)),
                      pl.BlockSpec((tm, D), lambda i, rm: (i, 0)),
                      pl.BlockSpec((1, 1, 1, D), lambda i, rm: (rm[i], 5, 0, 0)),
                      pl.BlockSpec((1, D), lambda i, rm: (0, 0))],
            out_specs=out_specs,
            scratch_shapes=[pltpu.VMEM((TOP_K, tm, D), F32), pltpu.SMEM((nd, LANES), I32),
                            pltpu.SemaphoreType.DMA((2,))]),
        out_shape=out_shape,
        compiler_params=_params("arbitrary"),
        name="moe_combine",
    )(rowmap, dest2d, y_rows, tg, x, mod4, norm_g.reshape(1, D))


def _moe_kernel(te_ref, nv_ref, x_ref, wg_ref, wu_ref, bg_ref, bu_ref, wo_ref, bo_ref, o_ref,
                acc_ref, xb_ref):
    del te_ref
    t, f = pl.program_id(0), pl.program_id(1)
    last = pl.num_programs(1) - 1
    valid = t < nv_ref[0]

    @pl.when(valid & (f == 0))
    def _():
        xb_ref[...] = x_ref[...].astype(BF16)

    @pl.when(valid)
    def _():
        x = xb_ref[...]
        g = jnp.dot(x, wg_ref[0], preferred_element_type=F32) + bg_ref[0]
        u = jnp.dot(x, wu_ref[0], preferred_element_type=F32) + bu_ref[0]
        g = jnp.minimum(g, SWIGLU_LIMIT)
        u = jnp.clip(u, -SWIGLU_LIMIT, SWIGLU_LIMIT)
        h = ((u + 1.0) * g * jax.nn.sigmoid(SWIGLU_ALPHA * g)).astype(BF16)
        part = jnp.dot(h, wo_ref[0], preferred_element_type=F32)

        @pl.when(f == 0)
        def _():
            acc_ref[...] = part + bo_ref[0]

        @pl.when(f > 0)
        def _():
            acc_ref[...] += part

        @pl.when(f == last)
        def _():
            o_ref[...] = acc_ref[...].astype(o_ref.dtype)

    @pl.when(jnp.logical_not(valid) & (f == last))
    def _():
        o_ref[...] = jnp.zeros(o_ref.shape, o_ref.dtype)


def _moe_experts(x_rows, tile_expert, n_valid, w_in, b_in, w_out, b_out, tm, tf):
    R, D = x_rows.shape
    E, _, F2 = w_in.shape
    F = F2 // 2
    tf = _tile(F, tf)
    nf = F // tf

    def fi(t, f, nv):
        return jnp.where(t < nv[0], f, nf - 1)

    def ti(t, nv):
        return jnp.minimum(t, jnp.maximum(nv[0] - 1, 0))

    return pl.pallas_call(
        _moe_kernel,
        grid_spec=pltpu.PrefetchScalarGridSpec(
            num_scalar_prefetch=2, grid=(R // tm, nf),
            in_specs=[pl.BlockSpec((tm, D), lambda t, f, te, nv: (ti(t, nv), 0)),
                      pl.BlockSpec((1, D, tf), lambda t, f, te, nv: (te[t], 0, fi(t, f, nv))),
                      pl.BlockSpec((1, D, tf), lambda t, f, te, nv: (te[t], 0, nf + fi(t, f, nv))),
                      pl.BlockSpec((1, 1, tf), lambda t, f, te, nv: (te[t], 0, fi(t, f, nv))),
                      pl.BlockSpec((1, 1, tf), lambda t, f, te, nv: (te[t], 0, nf + fi(t, f, nv))),
                      pl.BlockSpec((1, tf, D), lambda t, f, te, nv: (te[t], fi(t, f, nv), 0)),
                      pl.BlockSpec((1, 1, D), lambda t, f, te, nv: (te[t], 0, 0))],
            out_specs=pl.BlockSpec((tm, D), lambda t, f, te, nv: (t, 0)),
            scratch_shapes=[pltpu.VMEM((tm, D), F32), pltpu.VMEM((tm, D), BF16)]),
        out_shape=jax.ShapeDtypeStruct((R, D), F32),
        compiler_params=_params("parallel", "arbitrary"),
        name="moe_experts",
    )(tile_expert, n_valid, x_rows, w_in, w_in, b_in.reshape(E, 1, F2), b_in.reshape(E, 1, F2),
      w_out, b_out.reshape(E, 1, D))


def _moe_layer(x, norm_g, mod4, rowmap, tm_tok, w_router, b_router, w_in, b_in, w_out, b_out,
               final_g=None, split_rows=None, tm=512, tf=512):
    M, D = x.shape
    E = w_router.shape[1]
    h, ti, tg, rk, cnt = _router(x, norm_g, mod4, rowmap, w_router, b_router, tm_tok)
    A = M * TOP_K
    n_tiles = -(-A // tm) + E
    counts = cnt[0, :E]
    ptiles = (counts + tm - 1) // tm
    pend = jnp.cumsum(ptiles)
    pstart = pend - ptiles
    n_valid = pend[-1]
    tix = jnp.minimum(jnp.arange(n_tiles, dtype=I32), jnp.maximum(n_valid - 1, 0))
    tile_expert = jnp.minimum(jnp.searchsorted(pend, tix, side='right'), E - 1).astype(I32)
    dest = _dest_rows((pstart * tm).astype(I32), ti, rk, tm_tok)
    dest2d = dest[:, :TOP_K].reshape(A // LANES, LANES)
    tail = jnp.stack([n_valid, jnp.asarray(n_tiles, n_valid.dtype)]).astype(I32)
    x_rows = _dispatch(h, dest2d, (pstart * tm + counts).astype(I32), (ptiles * tm - counts).astype(I32),
                       tail, n_tiles * tm, tm_tok, tm)
    y_rows = _moe_experts(x_rows, tile_expert, n_valid.reshape(1).astype(I32),
                          w_in.astype(BF16), b_in, w_out.astype(BF16), b_out, tm, tf)
    return _combine(y_rows, dest2d, tg, x, mod4, rowmap, norm_g if final_g is None else final_g,
                    split_rows if final_g is not None else None, tm_tok)


def _final_norm_kernel(x_ref, g_ref, o_ref):
    x = x_ref[...]
    o_ref[...] = x * lax.rsqrt(jnp.mean(x * x, axis=-1, keepdims=True) + RMS_EPS) * g_ref[...]


def _final_norm(x, g, t0, rows, tm):
    D = x.shape[1]
    return pl.pallas_call(
        _final_norm_kernel,
        grid=(rows // tm,),
        in_specs=[pl.BlockSpec((tm, D), lambda i: (i + t0 // tm, 0)), pl.BlockSpec((1, D), lambda i: (0, 0))],
        out_specs=pl.BlockSpec((tm, D), lambda i: (i, 0)),
        out_shape=jax.ShapeDtypeStruct((rows, D), F32),
        compiler_params=_params("parallel"),
        name="final_norm",
    )(x, g.reshape(1, D))


def _grid_pos_embedding(T, D):
    rows = T // GRID_W
    row = jnp.repeat(jnp.arange(rows), GRID_W)
    col = jnp.tile(jnp.arange(GRID_W), rows)
    quarter = D // 4
    omega = 1.0 / (POS_BASE ** (jnp.arange(quarter, dtype=F32) / quarter))

    def axis_emb(p):
        ang = p.astype(F32)[:, None] * omega[None]
        return jnp.concatenate([jnp.sin(ang), jnp.cos(ang)], axis=-1)

    return jnp.concatenate([axis_emb(row), axis_emb(col)], axis=-1)


def kernel(x_prompt, x_sample, state_delta, c, c_ctx, w_mod, b_mod, norm_mix, norm_ffn, norm_final, dn_w_in, dn_conv, dn_a_log, dn_dt_bias, dn_norm, dn_w_out, hy_w_in, hy_conv, hy_w1, hy_b1, hy_w2, hy_b2, hy_w3, hy_b3, hy_w4, hy_freq, hy_bias, hy_w_out, moe_w_router, moe_b_router, moe_w_in, moe_b_in, moe_w_out, moe_b_out):
    Bp, Tp, D = x_prompt.shape
    Bs, Ts, _ = x_sample.shape
    depth = w_mod.shape[0]
    H = state_delta.shape[3]
    Mp, Ms = Bp * Tp, Bs * Ts
    M = Mp + Ms
    assert Mp % Ts == 0 and Ts % Tp == 0 and Tp % DN_CHUNK == 0, "token groups must tile each other"
    tT = Tp
    tm_big = _tile(math.gcd(Mp, Ts), 1024)
    tm_mid = _tile(math.gcd(Mp, Ts), 512)
    tm_tok = min(256, Tp)

    def rowmap_for(tm):
        tile_start = np.arange(M // tm) * tm
        return jnp.asarray(np.where(tile_start < Mp, 0, 1 + (tile_start - Mp) // Ts), I32)

    rm_big, rm_mid, rm_tok = rowmap_for(tm_big), rowmap_for(tm_mid), rowmap_for(tm_tok)

    xs = x_sample + _grid_pos_embedding(Ts, D)[None]
    x = jnp.concatenate([x_prompt.reshape(Mp, D), xs.reshape(Ms, D)], axis=0)
    conv_start = np.arange(M // tT) * tT
    seq_len = np.where(conv_start < Mp, Tp, Ts)
    seq_pos = np.where(conv_start < Mp, conv_start % Tp, (conv_start - Mp) % Ts)
    has_prev = jnp.asarray(seq_pos > 0, I32)
    has_next = jnp.asarray(seq_pos + tT < seq_len, I32)

    n_cond = 1 + Bs
    r_pad = -(-n_cond // 8) * 8
    cond = jnp.zeros((r_pad, D), F32).at[0].set(c_ctx).at[1:n_cond].set(c)
    mod = _modulation(cond, w_mod, b_mod)

    new_states = []
    i_dn = i_hy = 0
    for l in range(depth):
        mod4 = mod[l].reshape(r_pad, 6, 1, D)
        if l % 2 == 0:
            i = i_dn
            i_dn += 1
            w_in = dn_w_in[i]
            n_main = 4 * H * HEAD_DIM
            w_ba = jnp.zeros((D, LANES), F32).at[:, :4 * H].set(w_in[:, n_main:]).astype(BF16)
            pm, ba = _fused_mm(_norm_mod_prologue, [x], [norm_mix[l]], [0, 1], w_in[:, :n_main].astype(BF16),
                               rm_big, mod4=mod4, side_w=w_ba, tm=tm_big, tn=1024)
            qkv, bg = _dn_act(pm, ba, dn_conv[i], dn_a_log[i], dn_dt_bias[i], has_prev, has_next, tT)
            hb = 2 if H % 2 == 0 else 1
            cols_p, gct_p = _delta_side_inputs(bg, 0, Bp, Tp, H, hb)
            cols_s, gct_s = _delta_side_inputs(bg, Mp, Bs, Ts, H, hb)
            og_p, s_fin = _delta(qkv, pm, cols_p, gct_p, dn_norm[i], None,
                                 n_batch=Bp, batch_off=0, T=Tp, hb=hb)
            og_s, _ = _delta(qkv, pm, cols_s, gct_s, dn_norm[i], state_delta[:, i].astype(F32),
                             n_batch=Bs, batch_off=Mp // Ts, T=Ts, hb=hb)
            new_states.append(s_fin.astype(state_delta.dtype))
            x = _fused_mm(functools.partial(_two_part_prologue, Mp // tm_big), [og_p], [], [],
                          dn_w_out[i].astype(BF16), rm_big, mod4=mod4, a_rest=og_s,
                          res=x, gate_chunk=2, out_dtype=F32, tm=tm_big, tn=1024)
        else:
            i = i_hy
            i_hy += 1
            u = _fused_mm(_norm_mod_prologue, [x], [norm_mix[l]], [0, 1], hy_w_in[i].astype(BF16),
                          rm_big, mod4=mod4, tm=tm_big, tn=1024)
            x0, s = _hy_pre(u, hy_conv[i], has_prev, has_next, tT)
            hy = (hy_w1[i], hy_b1[i], hy_w2[i], hy_b2[i], hy_w3[i], hy_b3[i], hy_w4[i], hy_freq[i])
            conv_p = _hyena_conv(s, Bp, 0, Tp, _hyena_filter_taps(Tp, *hy))
            conv_s = _hyena_conv(s, Bs, Mp, Ts, _hyena_filter_taps(Ts, *hy))
            x = _fused_mm(functools.partial(_hyena_gate_prologue, Mp // tm_mid), [conv_p, x0, s], [hy_bias[i]], [],
                          hy_w_out[i].astype(BF16), rm_mid, mod4=mod4, a_rest=conv_s,
                          res=x, gate_chunk=2, out_dtype=F32, tm=tm_mid, tn=1024)
        last = l == depth - 1
        x = _moe_layer(x, norm_ffn[l], mod4, rm_tok, tm_tok, moe_w_router[l], moe_b_router[l],
                       moe_w_in[l], moe_b_in[l], moe_w_out[l], moe_b_out[l],
                       final_g=norm_final if last else None, split_rows=Mp)

    y_prompt, y_sample = x
    return y_prompt.reshape(Bp, Tp, D), y_sample.reshape(Bs, Ts, D), jnp.stack(new_states, axis=1)
```

```python
import functools
import math

import numpy as np
import jax
import jax.numpy as jnp
from jax import lax
from jax.experimental import pallas as pl
from jax.experimental.pallas import tpu as pltpu

F32, BF16, I32 = jnp.float32, jnp.bfloat16, jnp.int32

V7X_VMEM_LIMIT_BYTES = 56 * 2**20
LANES = 128
BF16_ROWS = 16

GRID_W = 64
DN_CHUNK = 128
HEAD_DIM = 128
TOP_K = 4
RMS_EPS = 1e-6
L2_EPS = 1e-6
POS_BASE = 10000.0
SWIGLU_LIMIT = 7.0
SWIGLU_ALPHA = 1.702
HY_TARGET = 1e-2
HY_FAST = 0.3
HY_SLOW = 1.5

HIGHEST = lax.Precision.HIGHEST
NT_DIMS = (((1,), (1,)), ((), ()))
TN_DIMS = (((0,), (0,)), ((), ()))


def _params(*sem):
    return pltpu.CompilerParams(dimension_semantics=sem, vmem_limit_bytes=V7X_VMEM_LIMIT_BYTES)


def _tile(n, pref):
    if n <= pref:
        return n
    t = pref - pref % LANES
    while n % t:
        t -= LANES
    return t


def _silu(x):
    return x * jax.nn.sigmoid(x)


def _softplus(x):
    return jnp.maximum(x, 0.0) + jnp.log1p(jnp.exp(-jnp.abs(x)))


def _split_bf16(x):
    hi = x.astype(BF16)
    lo = (x - hi.astype(F32)).astype(BF16)
    return hi, lo


def _dot3(a, b):
    ah, al = _split_bf16(a)
    bh, bl = _split_bf16(b)
    return (jnp.dot(ah, bh, preferred_element_type=F32)
            + (jnp.dot(ah, bl, preferred_element_type=F32) + jnp.dot(al, bh, preferred_element_type=F32)))


def _modulation_kernel(c_ref, w_ref, b_ref, o_ref):
    s = _silu(c_ref[...]).astype(BF16)
    o_ref[0] = jnp.dot(s, w_ref[0].astype(BF16), preferred_element_type=F32) + b_ref[0]


def _modulation(cond, w_mod, b_mod):
    R, D = cond.shape
    L, _, N = w_mod.shape
    tn = _tile(N, 1024)
    return pl.pallas_call(
        _modulation_kernel,
        grid=(L, N // tn),
        in_specs=[pl.BlockSpec((R, D), lambda l, j: (0, 0)),
                  pl.BlockSpec((1, D, tn), lambda l, j: (l, 0, j)),
                  pl.BlockSpec((1, 1, tn), lambda l, j: (l, 0, j))],
        out_specs=pl.BlockSpec((1, R, tn), lambda l, j: (l, 0, j)),
        out_shape=jax.ShapeDtypeStruct((L, R, N), F32),
        compiler_params=_params("parallel", "parallel"),
        name="modulation",
    )(cond, w_mod, b_mod.reshape(L, 1, N))


def _fused_mm_kernel(prologue, n_a, n_v, n_m, has_res, has_side, rm_ref, *refs):
    del rm_ref
    a_refs, refs = refs[:n_a], refs[n_a:]
    v_refs, refs = refs[:n_v], refs[n_v:]
    m_refs, refs = refs[:n_m], refs[n_m:]
    w_ref, refs = refs[0], refs[1:]
    if has_res:
        res_ref, gate_ref, refs = refs[0], refs[1], refs[2:]
    if has_side:
        sw_ref, refs = refs[0], refs[1:]
    o_ref, refs = refs[0], refs[1:]
    if has_side:
        so_ref, refs = refs[0], refs[1:]
    a_scr = refs[0]

    row_tile = pl.program_id(0)

    @pl.when(pl.program_id(1) == 0)
    def _():
        a = prologue([r[...] for r in a_refs], [r[...] for r in v_refs], [r[0, 0] for r in m_refs], row_tile)
        a_scr[...] = a.astype(BF16)
        if has_side:
            so_ref[...] = jnp.dot(a_scr[...], sw_ref[...], preferred_element_type=F32)

    acc = jnp.dot(a_scr[...], w_ref[...], preferred_element_type=F32)
    if has_res:
        acc = res_ref[...] + gate_ref[0, 0] * acc
    o_ref[...] = acc.astype(o_ref.dtype)


def _fused_mm(prologue, a_ins, vec_ins, mod_ins, w, rowmap, *, mod4=None, res=None, gate_chunk=None,
              side_w=None, a_rest=None, out_dtype=BF16, tm, tn):
    a_split = None if a_rest is None else a_ins[0].shape[0]
    K = a_ins[0].shape[1]
    M = rowmap.shape[0] * tm
    N = w.shape[1]
    tn = _tile(N, tn)
    in_specs, args = [], []
    for a in a_ins:
        in_specs.append(pl.BlockSpec((tm, K), lambda i, j, rm: (i, 0)))
        args.append(a)
    if a_split is not None:
        n_first = a_split // tm
        in_specs[0] = pl.BlockSpec((tm, K), lambda i, j, rm: (jnp.minimum(i, n_first - 1), 0))
        in_specs.insert(1, pl.BlockSpec((tm, K), lambda i, j, rm: (jnp.maximum(i - n_first, 0), 0)))
        args.insert(1, a_rest)
    for v in vec_ins:
        in_specs.append(pl.BlockSpec((1, K), lambda i, j, rm: (0, 0)))
        args.append(v.reshape(1, K))
    for c in mod_ins:
        in_specs.append(pl.BlockSpec((1, 1, 1, K), lambda i, j, rm, c=c: (rm[i], c, 0, 0)))
        args.append(mod4)
    in_specs.append(pl.BlockSpec((K, tn), lambda i, j, rm: (0, j)))
    args.append(w)
    if res is not None:
        in_specs.append(pl.BlockSpec((tm, tn), lambda i, j, rm: (i, j)))
        args.append(res)
        in_specs.append(pl.BlockSpec((1, 1, 1, tn), lambda i, j, rm, c=gate_chunk: (rm[i], c, 0, j)))
        args.append(mod4)
    out_specs = [pl.BlockSpec((tm, tn), lambda i, j, rm: (i, j))]
    out_shape = [jax.ShapeDtypeStruct((M, N), out_dtype)]
    if side_w is not None:
        ns = side_w.shape[1]
        in_specs.append(pl.BlockSpec((K, ns), lambda i, j, rm: (0, 0)))
        args.append(side_w)
        out_specs.append(pl.BlockSpec((tm, ns), lambda i, j, rm: (i, 0)))
        out_shape.append(jax.ShapeDtypeStruct((M, ns), F32))
    kern = functools.partial(_fused_mm_kernel, prologue, len(a_ins) + (a_rest is not None), len(vec_ins),
                             len(mod_ins), res is not None, side_w is not None)
    outs = pl.pallas_call(
        kern,
        grid_spec=pltpu.PrefetchScalarGridSpec(
            num_scalar_prefetch=1, grid=(M // tm, N // tn), in_specs=in_specs, out_specs=out_specs,
            scratch_shapes=[pltpu.VMEM((tm, K), BF16)]),
        out_shape=out_shape,
        compiler_params=_params("parallel", "arbitrary"),
        name="fused_mm",
    )(rowmap, *args)
    return outs if side_w is not None else outs[0]


def _norm_mod_prologue(a, v, m, row_tile=None):
    x, g, (shift, scale) = a[0], v[0], m
    y = x * lax.rsqrt(jnp.mean(x * x, axis=-1, keepdims=True) + RMS_EPS) * g
    return y * (1.0 + scale) + shift


def _two_part_prologue(n_first_tiles, a, v, m, row_tile):
    return jnp.where(row_tile < n_first_tiles, a[0], a[1])


def _hyena_gate_prologue(n_first_tiles, a, v, m, row_tile):
    conv = jnp.where(row_tile < n_first_tiles, a[0], a[1]).astype(F32)
    x0, s = a[2].astype(F32), a[3].astype(F32)
    return x0 * (conv + s * v[0])


def _conv3(x, prev_row, next_row, w, rid):
    t = x.shape[0]
    xm = jnp.where(rid == 0, prev_row, pltpu.roll(x, 1, 0))
    xp = jnp.where(rid == t - 1, next_row, pltpu.roll(x, t - 1, 0))
    return xm * w[0:1] + x * w[1:2] + xp * w[2:3]


def _halo_specs(tT, width, col_block):
    hb = tT // BF16_ROWS
    return [
        pl.BlockSpec((tT, width), lambda i, hp, hn: (i, col_block)),
        pl.BlockSpec((BF16_ROWS, width), lambda i, hp, hn: (jnp.maximum(i * hb - 1, 0), col_block)),
        pl.BlockSpec((BF16_ROWS, width), lambda i, hp, hn: ((i + 1) * hb * hn[i], col_block)),
    ]


def _dn_act_kernel(n_heads, hp_ref, hn_ref, x_ref, xp_ref, xn_ref, ba_ref, cw_ref, al_ref, dtb_ref,
                   qkv_ref, bg_ref):
    i = pl.program_id(0)
    hp = hp_ref[i].astype(F32)
    hn = hn_ref[i].astype(F32)
    tT = x_ref.shape[0]
    rid = lax.broadcasted_iota(I32, (tT, LANES), 0)
    for cb in range(3 * n_heads):
        sl = slice(cb * LANES, (cb + 1) * LANES)
        x = x_ref[:, sl].astype(F32)
        pr = xp_ref[:, sl].astype(F32)[BF16_ROWS - 1:BF16_ROWS] * hp
        nx = xn_ref[:, sl].astype(F32)[0:1] * hn
        y = _silu(_conv3(x, pr, nx, cw_ref[:, sl], rid))
        if cb < 2 * n_heads:
            y = y * lax.rsqrt(jnp.sum(y * y, axis=-1, keepdims=True) + L2_EPS)
            if cb < n_heads:
                y = y * (HEAD_DIM ** -0.5)
        qkv_ref[:, sl] = y.astype(BF16)

    ba = ba_ref[...]
    lane = lax.broadcasted_iota(I32, (tT, LANES), 1)
    beta = jax.nn.sigmoid(ba)
    g = -jnp.exp(al_ref[...]) * _softplus(ba + dtb_ref[...])
    r = lax.broadcasted_iota(I32, (tT, tT), 0)
    c = lax.broadcasted_iota(I32, (tT, tT), 1)
    shift = DN_CHUNK.bit_length() - 1
    same = jnp.right_shift(r, shift) == jnp.right_shift(c, shift)
    l_pre = jnp.where(same & (c <= r), 1.0, 0.0).astype(F32)
    l_suf = jnp.where(same & (c >= r), 1.0, 0.0).astype(F32)
    g_pre = jnp.dot(l_pre, g, precision=HIGHEST, preferred_element_type=F32)
    g_suf = jnp.dot(l_suf, g, precision=HIGHEST, preferred_element_type=F32)
    gc = jnp.where(lane < 3 * n_heads, g_pre, g_suf)
    bg_ref[...] = jnp.where(lane < 2 * n_heads, beta, gc)


def _dn_act(pm, ba, conv_w, a_log, dt_bias, has_prev, has_next, tT):
    M = pm.shape[0]
    cw = conv_w.shape[1]
    n_heads = cw // (3 * HEAD_DIM)
    pad = lambda v: jnp.zeros((1, LANES), F32).at[0, 2 * n_heads:4 * n_heads].set(v.reshape(-1).astype(F32))
    vec = pl.BlockSpec((1, LANES), lambda i, hp, hn: (0, 0))
    return pl.pallas_call(
        functools.partial(_dn_act_kernel, n_heads),
        grid_spec=pltpu.PrefetchScalarGridSpec(
            num_scalar_prefetch=2, grid=(M // tT,),
            in_specs=_halo_specs(tT, cw, 0) + [
                pl.BlockSpec((tT, LANES), lambda i, hp, hn: (i, 0)),
                pl.BlockSpec((3, cw), lambda i, hp, hn: (0, 0)), vec, vec],
            out_specs=[pl.BlockSpec((tT, cw), lambda i, hp, hn: (i, 0)),
                       pl.BlockSpec((tT, LANES), lambda i, hp, hn: (i, 0))]),
        out_shape=[jax.ShapeDtypeStruct((M, cw), BF16), jax.ShapeDtypeStruct((M, LANES), F32)],
        compiler_params=_params("parallel"),
        name="dn_act",
    )(has_prev, has_next, pm, pm, pm, ba, conv_w, pad(a_log), pad(dt_bias))


def _merge_masks(ri, ci, n, lower):
    masks = []
    s = 0
    while (1 << s) < n:
        same = jnp.right_shift(ri, s + 1) == jnp.right_shift(ci, s + 1)
        hi_r = jnp.bitwise_and(jnp.right_shift(ri, s), 1)
        hi_c = jnp.bitwise_and(jnp.right_shift(ci, s), 1)
        off = (hi_r == 1) & (hi_c == 0) if lower else (hi_r == 0) & (hi_c == 1)
        masks.append(same & off)
        s += 1
    return masks


def _dot3_many(xs, ys):
    sx = [_split_bf16(x) for x in xs]
    sy = [_split_bf16(y) for y in ys]
    hh = [jnp.dot(x[0], y[0], preferred_element_type=F32) for x, y in zip(sx, sy)]
    hl = [jnp.dot(x[0], y[1], preferred_element_type=F32) for x, y in zip(sx, sy)]
    lh = [jnp.dot(x[1], y[0], preferred_element_type=F32) for x, y in zip(sx, sy)]
    return [a + (b + c) for a, b, c in zip(hh, hl, lh)]


def _dot1_many(xs, ys):
    return [jnp.dot(x.astype(BF16), y.astype(BF16), preferred_element_type=F32) for x, y in zip(xs, ys)]


def _unit_tri_inverse_many(mats, eye, masks, stricts):
    ts = [eye - jnp.where(m[0], a, 0.0) for a, m in zip(mats, masks)]
    for lvl in range(1, len(masks[0])):
        off = [jnp.where(m[lvl], a, 0.0) for a, m in zip(mats, masks)]
        upd = _dot1_many(_dot1_many(ts, off), ts)
        ts = [t - u for t, u in zip(ts, upd)]
    full = [eye + jnp.where(st, a, 0.0) for a, st in zip(mats, stricts)]
    res = [eye - p for p in _dot3_many(full, ts)]
    return [t + c for t, c in zip(ts, _dot1_many(ts, res))]


def _delta_kernel(hb, n_chunks, group, has_s0, *refs):
    if has_s0:
        q_ref, k_ref, v_ref, z_ref, cols_ref, gct_ref, ng_ref, s0_ref, o_ref, sfin_ref = refs[:10]
        scr = refs[10:]
    else:
        q_ref, k_ref, v_ref, z_ref, cols_ref, gct_ref, ng_ref, o_ref, sfin_ref = refs[:9]
        scr = refs[9:]
    u_scr, wq_scr, qk_scr, s_scr, o_scr = scr
    C = DN_CHUNK
    n = n_chunks
    ri = lax.broadcasted_iota(I32, (C, C), 0)
    ci = lax.broadcasted_iota(I32, (C, C), 1)
    eye = (ri == ci).astype(F32)
    incl = (ri >= ci, ri <= ci)

    def gate_cols(hh, d, rows):
        bcol = cols_ref[0, 0, rows, hh * 4 + d:hh * 4 + d + 1]
        gcol = cols_ref[0, 0, rows, hh * 4 + 2 + d:hh * 4 + 3 + d]
        gl = gcol[C - 1:C, :] if d == 0 else gcol[0:1, :]
        return bcol, gcol, gl

    def prep(it, carry):
        tri_masks = (_merge_masks(ri, ci, C, True), _merge_masks(ri, ci, C, False))
        tri_strict = (ri > ci, ri < ci)
        where_, mats, masks, stricts, rhss = [], [], [], [], []
        for gi in range(group):
            c = it * group + gi
            rows = pl.ds(pl.multiple_of(c * C, C), C)
            for hh in range(hb):
                ls = slice(hh * HEAD_DIM, (hh + 1) * HEAD_DIM)
                qb, kb16, vb = q_ref[0, rows, ls], k_ref[0, rows, ls], v_ref[0, rows, ls]
                qf, kf, vf = qb.astype(F32), kb16.astype(F32), vb.astype(F32)
                qkt = lax.dot_general(qb, kb16, NT_DIMS, preferred_element_type=F32)
                for d in range(2):
                    ch = hh * 2 + d
                    bcol, gcol, _ = gate_cols(hh, d, rows)
                    grow = gct_ref[0, hh, d, pl.ds(c, 1), :]
                    dm = jnp.where(incl[d], jnp.exp(jnp.where(incl[d], gcol - grow, 0.0)), 0.0)
                    kbeta = kf * bcol
                    eg = jnp.exp(gcol)
                    mats.append(lax.dot_general(kbeta.astype(BF16), kb16, NT_DIMS, preferred_element_type=F32) * dm)
                    masks.append(tri_masks[d])
                    stricts.append(tri_strict[d])
                    rhss.append(jnp.concatenate([vf * bcol, kbeta * eg], axis=1))
                    where_.append((ch, c, rows))
                    wq_scr[ch, pl.ds(pl.multiple_of(c * 2 * C + C, C), C), :] = (qf * eg).astype(BF16)
                    qk_scr[ch, rows, :] = (qkt * dm).astype(BF16)
        sols = _dot3_many(_unit_tri_inverse_many(mats, eye, masks, stricts), rhss)
        for (ch, c, rows), sol in zip(where_, sols):
            u_scr[ch, rows, :] = sol[:, :HEAD_DIM]
            wq_scr[ch, pl.ds(pl.multiple_of(c * 2 * C, 2 * C), C), :] = sol[:, HEAD_DIM:].astype(BF16)
        return carry

    lax.fori_loop(0, n // group, prep, 0)

    for hh in range(hb):
        for d in range(2):
            s_scr[hh * 2 + d] = s0_ref[0, d, hh] if has_s0 else jnp.zeros((HEAD_DIM, HEAD_DIM), F32)

    T = n * C
    tr = min(T, 256)

    def clear(b, carry):
        o_scr[:, pl.ds(pl.multiple_of(b * tr, tr), tr), :] = jnp.zeros((hb, tr, HEAD_DIM), F32)
        return carry

    lax.fori_loop(0, T // tr, clear, 0)

    def scan(it, carry):
        chains = [(hh, d) for hh in range(hb) for d in range(2)]
        cs = [it if d == 0 else n - 1 - it for _, d in chains]
        rows = [pl.ds(pl.multiple_of(c * C, C), C) for c in cs]
        ss = [s_scr[hh * 2 + d] for hh, d in chains]
        rs = [jnp.dot(wq_scr[hh * 2 + d, pl.ds(pl.multiple_of(c * 2 * C, 2 * C), 2 * C), :], s.astype(BF16),
                      preferred_element_type=F32) for (hh, d), c, s in zip(chains, cs, ss)]
        vns = [(u_scr[hh * 2 + d, rw, :] - r[:C]).astype(BF16) for (hh, d), rw, r in zip(chains, rows, rs)]
        for i, (hh, d) in enumerate(chains):
            _, gcol, gl = gate_cols(hh, d, rows[i])
            kt = (k_ref[0, rows[i], hh * HEAD_DIM:(hh + 1) * HEAD_DIM].astype(F32) * jnp.exp(gl - gcol)).astype(BF16)
            s_scr[hh * 2 + d] = (ss[i] * jnp.exp(gl)
                                 + lax.dot_general(kt, vns[i], TN_DIMS, preferred_element_type=F32))
        for i, (hh, d) in enumerate(chains):
            o = rs[i][C:] + jnp.dot(qk_scr[hh * 2 + d, rows[i], :], vns[i], preferred_element_type=F32)
            o_scr[hh, rows[i], :] += o
        return carry

    lax.fori_loop(0, n, scan, 0)

    for hh in range(hb):
        for d in range(2):
            sfin_ref[0, d, hh] = s_scr[hh * 2 + d]

    def gate(b, carry):
        rows = pl.ds(pl.multiple_of(b * tr, tr), tr)
        for hh in range(hb):
            ls = slice(hh * HEAD_DIM, (hh + 1) * HEAD_DIM)
            o = o_scr[hh, rows, :]
            o = o * lax.rsqrt(jnp.mean(o * o, axis=-1, keepdims=True) + RMS_EPS) * ng_ref[...]
            o_ref[0, rows, ls] = (o * _silu(z_ref[0, rows, ls].astype(F32))).astype(BF16)
        return carry

    lax.fori_loop(0, T // tr, gate, 0)


def _delta(qkv, pm, cols, gct, norm_g, s0, *, n_batch, batch_off, T, hb):
    M, cw = qkv.shape
    H = cw // (3 * HEAD_DIM)
    n = T // DN_CHUNK
    W = hb * HEAD_DIM
    nb = H // hb
    qkv3 = qkv.reshape(M // T, T, cw)
    pm3 = pm.reshape(M // T, T, pm.shape[1])
    once = pl.Buffered(1)
    blk = lambda off: pl.BlockSpec((1, T, W), lambda b, j, off=off: (b + batch_off, 0, off * nb + j),
                                   pipeline_mode=once)
    in_specs = [blk(0), blk(1), blk(2), blk(3),
                pl.BlockSpec((1, 1, T, 4 * hb), lambda b, j: (b, j, 0, 0), pipeline_mode=once),
                pl.BlockSpec((1, hb, 2, n, DN_CHUNK), lambda b, j: (b, j, 0, 0, 0)),
                pl.BlockSpec((1, HEAD_DIM), lambda b, j: (0, 0))]
    args = [qkv3, qkv3, qkv3, pm3, cols, gct, norm_g.reshape(1, HEAD_DIM)]
    if s0 is not None:
        in_specs.append(pl.BlockSpec((1, 2, hb, HEAD_DIM, HEAD_DIM), lambda b, j: (b, 0, j, 0, 0)))
        args.append(s0)
    nch = 2 * hb
    group = math.gcd(n, max(1, 8 // nch))
    out, sfin = pl.pallas_call(
        functools.partial(_delta_kernel, hb, n, group, s0 is not None),
        grid=(n_batch, nb),
        in_specs=in_specs,
        out_specs=[pl.BlockSpec((1, T, W), lambda b, j: (b, 0, j)),
                   pl.BlockSpec((1, 2, hb, HEAD_DIM, HEAD_DIM), lambda b, j: (b, 0, j, 0, 0))],
        out_shape=[jax.ShapeDtypeStruct((n_batch, T, H * HEAD_DIM), BF16),
                   jax.ShapeDtypeStruct((n_batch, 2, H, HEAD_DIM, HEAD_DIM), F32)],
        scratch_shapes=[pltpu.VMEM((nch, T, HEAD_DIM), F32),
                        pltpu.VMEM((nch, 2 * T, HEAD_DIM), BF16),
                        pltpu.VMEM((nch, T, DN_CHUNK), BF16),
                        pltpu.VMEM((nch, HEAD_DIM, HEAD_DIM), F32),
                        pltpu.VMEM((hb, T, HEAD_DIM), F32)],
        compiler_params=_params("parallel", "parallel"),
        name="delta_rule",
    )(*args)
    return out.reshape(n_batch * T, H * HEAD_DIM), sfin


def _delta_side_inputs(bg, t0, n_batch, T, H, hb):
    b5 = bg[t0:t0 + n_batch * T, :4 * H].reshape(n_batch, T, 2, 2, H)
    cols = jnp.transpose(b5, (0, 4, 1, 2, 3)).reshape(n_batch, H // hb, hb, T, 4)
    cols = jnp.transpose(cols, (0, 1, 3, 2, 4)).reshape(n_batch, H // hb, T, 4 * hb)
    gct = jnp.transpose(b5[:, :, 1], (0, 3, 2, 1)).reshape(n_batch, H, 2, T // DN_CHUNK, DN_CHUNK)
    return cols, gct


def _hy_pre_kernel(D, hp_ref, hn_ref, x_ref, xp_ref, xn_ref, cw_ref, x0_ref, s_ref):
    i = pl.program_id(0)
    hp = hp_ref[i].astype(F32)
    hn = hn_ref[i].astype(F32)
    tT = x_ref.shape[0]
    W = min(D, 2 * LANES)
    rid = lax.broadcasted_iota(I32, (tT, W), 0)

    def conv(cb, part):
        sl = slice(part * D + cb * W, part * D + (cb + 1) * W)
        x = x_ref[:, sl].astype(F32)
        pr = xp_ref[:, sl].astype(F32)[BF16_ROWS - 1:BF16_ROWS] * hp
        nx = xn_ref[:, sl].astype(F32)[0:1] * hn
        return _conv3(x, pr, nx, cw_ref[:, sl], rid)

    for cb in range(D // W):
        sl = slice(cb * W, (cb + 1) * W)
        x0_ref[:, sl] = conv(cb, 0).astype(BF16)
        s_ref[:, sl] = (conv(cb, 1) * conv(cb, 2)).astype(BF16)


def _hy_pre(u, conv_w, has_prev, has_next, tT):
    M, W3 = u.shape
    D = W3 // 3
    out = pl.BlockSpec((tT, D), lambda i, hp, hn: (i, 0))
    return pl.pallas_call(
        functools.partial(_hy_pre_kernel, D),
        grid_spec=pltpu.PrefetchScalarGridSpec(
            num_scalar_prefetch=2, grid=(M // tT,),
            in_specs=_halo_specs(tT, W3, 0) + [pl.BlockSpec((3, W3), lambda i, hp, hn: (0, 0))],
            out_specs=[out, out]),
        out_shape=[jax.ShapeDtypeStruct((M, D), BF16)] * 2,
        compiler_params=_params("parallel"),
        name="hy_pre",
    )(has_prev, has_next, u, u, u, conv_w)


def _bmm_kernel(a_ref, b_ref, o_ref, acc_ref):
    k = pl.program_id(3)
    part = jnp.dot(a_ref[...], b_ref[0], preferred_element_type=F32)

    @pl.when(k == 0)
    def _():
        acc_ref[...] = part

    @pl.when(k > 0)
    def _():
        acc_ref[...] += part

    @pl.when(k == pl.num_programs(3) - 1)
    def _():
        o_ref[0] = acc_ref[...].astype(o_ref.dtype)


def _bmm(a, b, b_batch_off=0, n_batch=None, out_dtype=BF16, tm=1024, tn=1024, tk=2048):
    M, K = a.shape
    N = b.shape[2]
    nb = b.shape[0] if n_batch is None else n_batch
    tm, tn, tk = _tile(M, tm), _tile(N, tn), _tile(K, tk)
    return pl.pallas_call(
        _bmm_kernel,
        grid=(nb, M // tm, N // tn, K // tk),
        in_specs=[pl.BlockSpec((tm, tk), lambda i, m, n, k: (m, k)),
                  pl.BlockSpec((1, tk, tn), lambda i, m, n, k: (i + b_batch_off, k, n))],
        out_specs=pl.BlockSpec((1, tm, tn), lambda i, m, n, k: (i, m, n)),
        out_shape=jax.ShapeDtypeStruct((nb, M, N), out_dtype),
        scratch_shapes=[pltpu.VMEM((tm, tn), F32)],
        compiler_params=_params("parallel", "parallel", "parallel", "arbitrary"),
        name="dft_mm",
    )(a, b)


def _spec_prod_kernel(inv_n, s_ref, k_ref, y_ref):
    sc, ss = s_ref[0, 0].astype(F32), s_ref[0, 1].astype(F32)
    kc, ks = k_ref[0], k_ref[1]
    first = (lax.broadcasted_iota(I32, sc.shape, 0) == 0) & (pl.program_id(1) == 0)
    y_ref[0, 0] = (jnp.where(first, sc * kc, 2.0 * (sc * kc - ss * ks)) * inv_n).astype(y_ref.dtype)
    y_ref[0, 1] = (jnp.where(first, ss * ks, 2.0 * (sc * ks + ss * kc)) * inv_n).astype(y_ref.dtype)


def _spec_prod(sf, kf):
    B, _, T, D = sf.shape
    tr, tc = _tile(T, 256), _tile(D, 1024)
    return pl.pallas_call(
        functools.partial(_spec_prod_kernel, 1.0 / (2 * T)),
        grid=(B, T // tr, D // tc),
        in_specs=[pl.BlockSpec((1, 2, tr, tc), lambda b, i, j: (b, 0, i, j)),
                  pl.BlockSpec((2, tr, tc), lambda b, i, j: (0, i, j))],
        out_specs=pl.BlockSpec((1, 2, tr, tc), lambda b, i, j: (b, 0, i, j)),
        out_shape=jax.ShapeDtypeStruct(sf.shape, BF16),
        compiler_params=_params("parallel", "parallel", "parallel"),
        name="spec_prod",
    )(sf, kf)


def _dft_matrix(T):
    k = jnp.arange(T, dtype=I32)[:, None]
    t = jnp.arange(T, dtype=I32)[None, :]
    ang = ((k * t) % (2 * T)).astype(F32) * (math.pi / T)
    nyq = jnp.where(t % 2 == 0, 1.0, -1.0).astype(F32)
    sin = jnp.where(k == 0, nyq, jnp.sin(ang))
    return jnp.concatenate([jnp.cos(ang), sin], axis=0).astype(BF16)


def _hyena_filter_taps(L, w1, b1, w2, b2, w3, b3, w4, freq):
    D = w4.shape[1] // 2
    n_bands = (w1.shape[0] - 1) // 2
    pos = jnp.arange(L, dtype=F32)
    t = pos / max(L - 1, 1)
    bands = jnp.linspace(1e-4, n_bands - 1, n_bands, dtype=F32)
    ang = (2.0 * math.pi / L) * pos[:, None] * bands[None]
    feats = jnp.concatenate([t[:, None], jnp.cos(ang), -jnp.sin(ang)], axis=-1)
    zf = jnp.sin(freq[0] * (feats @ w1 + b1))
    zf = jnp.sin(freq[1] * (zf @ w2 + b2))
    zf = jnp.sin(freq[2] * (zf @ w3 + b3))
    filt = (zf @ w4).reshape(L, 2, D)
    deltas = jnp.abs(jnp.linspace(math.log(HY_TARGET) / HY_SLOW, math.log(HY_TARGET) / HY_FAST, D, dtype=F32))
    filt = filt * jnp.exp(-t[:, None, None] * deltas[None, None])
    f = filt[:, 0]
    b = filt[:, 1].at[0].set(0.0)
    scale = lax.rsqrt(jnp.sum(f * f, axis=0) + jnp.sum(b * b, axis=0) + 1e-6)
    return f * scale, b * scale


def _hyena_conv(s, n_batch, t0, T, taps):
    M, D = s.shape
    f, b = taps
    fwd = _dft_matrix(T)
    kf = _bmm(fwd, jnp.concatenate([f, b], axis=1).astype(BF16)[None], out_dtype=F32)[0]
    p, q = kf[:, :D].reshape(2, T, D), kf[:, D:].reshape(2, T, D)
    first = (jnp.arange(T) == 0)[:, None]
    kspec = jnp.stack([p[0] + q[0], jnp.where(first, p[1] + q[1], p[1] - q[1])])
    s3 = s.reshape(M // T, T, D)
    sf = _bmm(fwd, s3, b_batch_off=t0 // T, n_batch=n_batch)
    y = _spec_prod(sf.reshape(n_batch, 2, T, D), kspec).reshape(n_batch, 2 * T, D)
    conv = _bmm(fwd.T, y)
    return conv.reshape(n_batch * T, D)


def _store_token_major(ref, x):
    t, d = x.shape
    nb = d // LANES
    for j in range(nb):
        ref[pl.ds(j, t, stride=nb), :] = x[:, j * LANES:(j + 1) * LANES]


def _load_token_major(ref, t, j, lead=()):
    nb = ref.shape[-2] // t
    return ref[lead + (pl.ds(j, t, stride=nb), slice(None))]


def _router_kernel(n_experts, rm_ref, x_ref, g_ref, sh_ref, sc_ref, wr_ref, br_ref,
                   h_ref, ti_ref, tg_ref, rk_ref, cnt_ref, base_scr):
    del rm_ref

    @pl.when(pl.program_id(0) == 0)
    def _():
        base_scr[...] = jnp.zeros(base_scr.shape, F32)

    h = _norm_mod_prologue([x_ref[...]], [g_ref[...]], (sh_ref[0, 0], sc_ref[0, 0]))
    _store_token_major(h_ref, h)
    logits = jnp.dot(h, wr_ref[...], precision=HIGHEST, preferred_element_type=F32) + br_ref[...]
    tm = logits.shape[0]
    lane = lax.broadcasted_iota(I32, logits.shape, 1)
    lane_f = lane.astype(F32)
    neg = jnp.float32(-jnp.inf)
    l = jnp.where(lane < n_experts, logits, neg)
    ti = jnp.zeros(logits.shape, I32)
    tl = jnp.full(logits.shape, neg, F32)
    onehot = jnp.zeros(logits.shape, F32)
    picks = []
    for r in range(TOP_K):
        m = jnp.max(l, axis=-1, keepdims=True)
        idx = jnp.min(jnp.where(l == m, lane_f, float(LANES)), axis=-1, keepdims=True).astype(I32)
        ti = jnp.where(lane == r, idx, ti)
        tl = jnp.where(lane == r, m, tl)
        l = jnp.where(lane == idx, neg, l)
        onehot = jnp.where(lane == idx, 1.0, onehot)
        picks.append(idx)
    e = jnp.exp(tl - jnp.max(tl, axis=-1, keepdims=True))
    ti_ref[...] = ti
    tg_ref[...] = e / jnp.sum(e, axis=-1, keepdims=True)

    r_i = lax.broadcasted_iota(I32, (tm, tm), 0)
    c_i = lax.broadcasted_iota(I32, (tm, tm), 1)
    earlier = jnp.where(c_i < r_i, 1.0, 0.0).astype(BF16)
    before = jnp.dot(earlier, onehot.astype(BF16), preferred_element_type=F32) + base_scr[...]
    rk = jnp.zeros(logits.shape, F32)
    for r in range(TOP_K):
        mine = jnp.sum(jnp.where(lane == picks[r], before, 0.0), axis=-1, keepdims=True)
        rk = jnp.where(lane == r, mine, rk)
    rk_ref[...] = rk.astype(I32)
    base_scr[...] += jnp.sum(onehot, axis=0, keepdims=True)
    cnt_ref[...] = base_scr[...].astype(I32)


def _router(x, g, mod4, rowmap, w_router, b_router, tm):
    M, D = x.shape
    E = w_router.shape[1]
    wr = jnp.zeros((D, LANES), F32).at[:, :E].set(w_router)
    br = jnp.zeros((1, LANES), F32).at[0, :E].set(b_router)
    row = lambda c: pl.BlockSpec((1, 1, 1, D), lambda i, rm, c=c: (rm[i], c, 0, 0))
    tile = lambda w: pl.BlockSpec((tm, w), lambda i, rm: (i, 0))
    return pl.pallas_call(
        functools.partial(_router_kernel, E),
        grid_spec=pltpu.PrefetchScalarGridSpec(
            num_scalar_prefetch=1, grid=(M // tm,),
            in_specs=[tile(D), pl.BlockSpec((1, D), lambda i, rm: (0, 0)), row(3), row(4),
                      pl.BlockSpec((D, LANES), lambda i, rm: (0, 0)),
                      pl.BlockSpec((1, LANES), lambda i, rm: (0, 0))],
            out_specs=[pl.BlockSpec((tm * D // LANES, LANES), lambda i, rm: (i, 0)),
                       tile(LANES), tile(LANES), tile(LANES),
                       pl.BlockSpec((1, LANES), lambda i, rm: (0, 0))],
            scratch_shapes=[pltpu.VMEM((1, LANES), F32)]),
        out_shape=[jax.ShapeDtypeStruct((M * D // LANES, LANES), F32), jax.ShapeDtypeStruct((M, LANES), I32),
                   jax.ShapeDtypeStruct((M, LANES), F32), jax.ShapeDtypeStruct((M, LANES), I32),
                   jax.ShapeDtypeStruct((1, LANES), I32)],
        compiler_params=_params("arbitrary"),
        name="router",
    )(rowmap, x, g.reshape(1, D), mod4, mod4, wr, br)


def _dest_kernel(ps_ref, ti_ref, rk_ref, o_ref):
    ti = ti_ref[...]
    start = jnp.zeros(ti.shape, I32)
    for e in range(ps_ref.shape[0]):
        start = jnp.where(ti == e, ps_ref[e], start)
    o_ref[...] = start + rk_ref[...]


def _dest_rows(seg_start, ti, rk, tm):
    M = ti.shape[0]
    tile = pl.BlockSpec((tm, LANES), lambda i, ps: (i, 0))
    return pl.pallas_call(
        _dest_kernel,
        grid_spec=pltpu.PrefetchScalarGridSpec(num_scalar_prefetch=1, grid=(M // tm,),
                                               in_specs=[tile, tile], out_specs=tile),
        out_shape=jax.ShapeDtypeStruct((M, LANES), I32),
        compiler_params=_params("parallel"),
        name="moe_dest",
    )(seg_start, ti, rk)


def _load_tile_dest(dest_ref, dest_smem, sem):
    cp = pltpu.make_async_copy(dest_ref, dest_smem, sem)
    cp.start()
    cp.wait()


def _zero_pad_rows(nb, pad_start_ref, pad_len_ref, o_ref, zero_ref, sem, start):
    def copy(pos, p):
        cp = pltpu.make_async_copy(zero_ref.at[pl.ds(0, p * nb)],
                                   o_ref.at[pl.ds(pl.multiple_of(pos * nb, nb), p * nb)], sem)
        if start:
            cp.start()
        else:
            cp.wait()

    def per_expert(e, carry):
        pos, length = pad_start_ref[e], pad_len_ref[e]
        p = zero_ref.shape[0] // nb // 2
        while p >= 1:
            hit = jnp.bitwise_and(length, p) != 0

            @pl.when(hit)
            def _(pos=pos, p=p):
                copy(pos, p)

            pos = pos + jnp.where(hit, p, 0)
            p //= 2
        return carry

    lax.fori_loop(0, pad_start_ref.shape[0], per_expert, 0)


def _dispatch_kernel(nb, pad_start_ref, pad_len_ref, tail_ref, dest_ref, h_ref, o_ref, dest_smem, zero_ref, sem):
    tm = h_ref.shape[0] // nb
    tz = zero_ref.shape[0]

    @pl.when(pl.program_id(0) == 0)
    def _():
        zero_ref[...] = jnp.zeros(zero_ref.shape, F32)

        def tail_copy(t):
            return pltpu.make_async_copy(zero_ref, o_ref.at[pl.ds(pl.multiple_of(t * tz, tz), tz)], sem.at[2])

        def tail_start(t, carry):
            tail_copy(t).start()
            return carry

        def tail_wait(t, carry):
            tail_copy(t).wait()
            return carry

        _zero_pad_rows(nb, pad_start_ref, pad_len_ref, o_ref, zero_ref, sem.at[2], True)
        lax.fori_loop(tail_ref[0], tail_ref[1], tail_start, 0)
        _zero_pad_rows(nb, pad_start_ref, pad_len_ref, o_ref, zero_ref, sem.at[2], False)
        lax.fori_loop(tail_ref[0], tail_ref[1], tail_wait, 0)

    _load_tile_dest(dest_ref, dest_smem, sem.at[0])

    def row_copy(j):
        dst = dest_smem[jnp.right_shift(j, 7), jnp.bitwise_and(j, LANES - 1)]
        return pltpu.make_async_copy(h_ref.at[pl.ds(pl.multiple_of(jnp.right_shift(j, 2) * nb, nb), nb)],
                                     o_ref.at[pl.ds(pl.multiple_of(dst * nb, nb), nb)], sem.at[1])

    def start(j, carry):
        row_copy(j).start()
        return carry

    def wait(j, carry):
        row_copy(j).wait()
        return carry

    lax.fori_loop(0, tm * TOP_K, start, 0, unroll=8)
    lax.fori_loop(0, tm * TOP_K, wait, 0, unroll=8)


def _dispatch(h, nb, dest2d, pad_start, pad_len, tail, n_rows, tm, tm_rows):
    M = h.shape[0] // nb
    nd = tm * TOP_K // LANES
    return pl.pallas_call(
        functools.partial(_dispatch_kernel, nb),
        grid_spec=pltpu.PrefetchScalarGridSpec(
            num_scalar_prefetch=3, grid=(M // tm,),
            in_specs=[pl.BlockSpec((nd, LANES), lambda i, ps, pn, tl: (i, 0)),
                      pl.BlockSpec((tm * nb, LANES), lambda i, ps, pn, tl: (i, 0))],
            out_specs=pl.BlockSpec(memory_space=pl.ANY),
            scratch_shapes=[pltpu.SMEM((nd, LANES), I32), pltpu.VMEM((tm_rows * nb, LANES), F32),
                            pltpu.SemaphoreType.DMA((3,))]),
        out_shape=jax.ShapeDtypeStruct((n_rows * nb, LANES), F32),
        compiler_params=_params("arbitrary"),
        name="moe_dispatch",
    )(pad_start, pad_len, tail, dest2d, h)


def _combine_kernel(final, rm_ref, dest_ref, y_ref, tg_ref, x_ref, gate_ref, ng_ref, *refs):
    del rm_ref
    n_out = 1 if final is None else 2
    o_ref = refs[0] if final is None else refs[:2]
    buf, dest_smem, sem = refs[n_out:]
    tm = x_ref.shape[0]
    _load_tile_dest(dest_ref, dest_smem, sem.at[0])

    nb = x_ref.shape[1] // LANES

    def row_copy(j):
        src = dest_smem[jnp.right_shift(j, 7), jnp.bitwise_and(j, LANES - 1)]
        return pltpu.make_async_copy(
            y_ref.at[pl.ds(pl.multiple_of(src * nb, nb), nb)],
            buf.at[jnp.bitwise_and(j, TOP_K - 1), pl.ds(pl.multiple_of(jnp.right_shift(j, 2) * nb, nb), nb)],
            sem.at[1])

    def start(j, carry):
        row_copy(j).start()
        return carry

    def wait(j, carry):
        row_copy(j).wait()
        return carry

    lax.fori_loop(0, tm * TOP_K, start, 0, unroll=8)
    lax.fori_loop(0, tm * TOP_K, wait, 0, unroll=8)
    tg = tg_ref[...]
    blocks = []
    for j in range(nb):
        acc = tg[:, 0:1] * _load_token_major(buf, tm, j, (0,))
        for k in range(1, TOP_K):
            acc = acc + tg[:, k:k + 1] * _load_token_major(buf, tm, j, (k,))
        blocks.append(acc)
    x = x_ref[...] + gate_ref[0, 0] * jnp.concatenate(blocks, axis=1)
    if final is None:
        o_ref[...] = x
    else:
        x = x * lax.rsqrt(jnp.mean(x * x, axis=-1, keepdims=True) + RMS_EPS) * ng_ref[...]
        first_ref, second_ref = o_ref

        @pl.when(pl.program_id(0) < final)
        def _():
            first_ref[...] = x

        @pl.when(pl.program_id(0) >= final)
        def _():
            second_ref[...] = x


def _combine(y_rows, dest2d, tg, x, mod4, rowmap, norm_g, split_rows, tm):
    M, D = x.shape
    nd = tm * TOP_K // LANES
    if split_rows is None:
        final = None
        out_specs = pl.BlockSpec((tm, D), lambda i, rm: (i, 0))
        out_shape = jax.ShapeDtypeStruct((M, D), F32)
    else:
        final = split_rows // tm
        out_specs = [pl.BlockSpec((tm, D), lambda i, rm: (jnp.minimum(i, final - 1), 0)),
                     pl.BlockSpec((tm, D), lambda i, rm: (jnp.maximum(i - final, 0), 0))]
        out_shape = [jax.ShapeDtypeStruct((split_rows, D), F32), jax.ShapeDtypeStruct((M - split_rows, D), F32)]
    return _combine_call(final, out_specs, out_shape, y_rows, dest2d, tg, x, mod4, rowmap, norm_g, tm, nd)


def _combine_call(final, out_specs, out_shape, y_rows, dest2d, tg, x, mod4, rowmap, norm_g, tm, nd):
    M, D = x.shape
    return pl.pallas_call(
        functools.partial(_combine_kernel, final),
        grid_spec=pltpu.PrefetchScalarGridSpec(
            num_scalar_prefetch=1, grid=(M // tm,),
            in_specs=[pl.BlockSpec((nd, LANES), lambda i, rm: (i, 0)),
                      pl.BlockSpec(memory_space=pl.ANY),
                      pl.BlockSpec((tm, LANES), lambda i, rm: (i, 0)),
                      pl.BlockSpec((tm, D), lambda i, rm: (i, 0)),
                      pl.BlockSpec((1, 1, 1, D), lambda i, rm: (rm[i], 5, 0, 0)),
                      pl.BlockSpec((1, D), lambda i, rm: (0, 0))],
            out_specs=out_specs,
            scratch_shapes=[pltpu.VMEM((TOP_K, tm * D // LANES, LANES), F32), pltpu.SMEM((nd, LANES), I32),
                            pltpu.SemaphoreType.DMA((2,))]),
        out_shape=out_shape,
        compiler_params=_params("arbitrary"),
        name="moe_combine",
    )(rowmap, dest2d, y_rows, tg, x, mod4, norm_g.reshape(1, D))


def _moe_kernel(te_ref, nv_ref, x_ref, wg_ref, wu_ref, bg_ref, bu_ref, wo_ref, bo_ref, o_ref,
                acc_ref, xb_ref):
    del te_ref
    t, f = pl.program_id(0), pl.program_id(1)
    last = pl.num_programs(1) - 1
    valid = t < nv_ref[0]

    tm, d = xb_ref.shape
    nb = d // LANES

    @pl.when(valid & (f == 0))
    def _():
        for j in range(nb):
            xb_ref[:, j * LANES:(j + 1) * LANES] = _load_token_major(x_ref, tm, j).astype(BF16)

    @pl.when(valid)
    def _():
        x = xb_ref[...]
        g = jnp.dot(x, wg_ref[0], preferred_element_type=F32) + bg_ref[0]
        u = jnp.dot(x, wu_ref[0], preferred_element_type=F32) + bu_ref[0]
        g = jnp.minimum(g, SWIGLU_LIMIT)
        u = jnp.clip(u, -SWIGLU_LIMIT, SWIGLU_LIMIT)
        h = ((u + 1.0) * g * jax.nn.sigmoid(SWIGLU_ALPHA * g)).astype(BF16)
        part = jnp.dot(h, wo_ref[0], preferred_element_type=F32)

        @pl.when(f == 0)
        def _():
            acc_ref[...] = part + bo_ref[0]

        @pl.when(f > 0)
        def _():
            acc_ref[...] += part

        @pl.when(f == last)
        def _():
            _store_token_major(o_ref, acc_ref[...])

    @pl.when(jnp.logical_not(valid) & (f == last))
    def _():
        o_ref[...] = jnp.zeros(o_ref.shape, o_ref.dtype)


def _moe_experts(x_rows, tile_expert, n_valid, w_in, b_in, w_out, b_out, tm, tf):
    E, D, F2 = w_in.shape
    nb = D // LANES
    R = x_rows.shape[0] // nb
    F = F2 // 2
    tf = _tile(F, tf)
    nf = F // tf

    def fi(t, f, nv):
        return jnp.where(t < nv[0], f, nf - 1)

    def ti(t, nv):
        return jnp.minimum(t, jnp.maximum(nv[0] - 1, 0))

    return pl.pallas_call(
        _moe_kernel,
        grid_spec=pltpu.PrefetchScalarGridSpec(
            num_scalar_prefetch=2, grid=(R // tm, nf),
            in_specs=[pl.BlockSpec((tm * nb, LANES), lambda t, f, te, nv: (ti(t, nv), 0)),
                      pl.BlockSpec((1, D, tf), lambda t, f, te, nv: (te[t], 0, fi(t, f, nv))),
                      pl.BlockSpec((1, D, tf), lambda t, f, te, nv: (te[t], 0, nf + fi(t, f, nv))),
                      pl.BlockSpec((1, 1, tf), lambda t, f, te, nv: (te[t], 0, fi(t, f, nv))),
                      pl.BlockSpec((1, 1, tf), lambda t, f, te, nv: (te[t], 0, nf + fi(t, f, nv))),
                      pl.BlockSpec((1, tf, D), lambda t, f, te, nv: (te[t], fi(t, f, nv), 0)),
                      pl.BlockSpec((1, 1, D), lambda t, f, te, nv: (te[t], 0, 0))],
            out_specs=pl.BlockSpec((tm * nb, LANES), lambda t, f, te, nv: (t, 0)),
            scratch_shapes=[pltpu.VMEM((tm, D), F32), pltpu.VMEM((tm, D), BF16)]),
        out_shape=jax.ShapeDtypeStruct((R * nb, LANES), F32),
        compiler_params=_params("parallel", "arbitrary"),
        name="moe_experts",
    )(tile_expert, n_valid, x_rows, w_in, w_in, b_in.reshape(E, 1, F2), b_in.reshape(E, 1, F2),
      w_out, b_out.reshape(E, 1, D))


def _moe_layer(x, norm_g, mod4, rowmap, tm_tok, w_router, b_router, w_in, b_in, w_out, b_out,
               final_g=None, split_rows=None, tm=512, tf=1024):
    M, D = x.shape
    E = w_router.shape[1]
    h, ti, tg, rk, cnt = _router(x, norm_g, mod4, rowmap, w_router, b_router, tm_tok)
    A = M * TOP_K
    n_tiles = -(-A // tm) + E
    counts = cnt[0, :E]
    ptiles = (counts + tm - 1) // tm
    pend = jnp.cumsum(ptiles)
    pstart = pend - ptiles
    n_valid = pend[-1]
    tix = jnp.minimum(jnp.arange(n_tiles, dtype=I32), jnp.maximum(n_valid - 1, 0))
    tile_expert = jnp.minimum(jnp.searchsorted(pend, tix, side='right'), E - 1).astype(I32)
    dest = _dest_rows((pstart * tm).astype(I32), ti, rk, tm_tok)
    dest2d = dest[:, :TOP_K].reshape(A // LANES, LANES)
    tail = jnp.stack([n_valid, jnp.asarray(n_tiles, n_valid.dtype)]).astype(I32)
    x_rows = _dispatch(h, D // LANES, dest2d, (pstart * tm + counts).astype(I32), (ptiles * tm - counts).astype(I32),
                       tail, n_tiles * tm, tm_tok, tm)
    y_rows = _moe_experts(x_rows, tile_expert, n_valid.reshape(1).astype(I32),
                          w_in.astype(BF16), b_in, w_out.astype(BF16), b_out, tm, tf)
    return _combine(y_rows, dest2d, tg, x, mod4, rowmap, norm_g if final_g is None else final_g,
                    split_rows if final_g is not None else None, tm_tok)


def _final_norm_kernel(x_ref, g_ref, o_ref):
    x = x_ref[...]
    o_ref[...] = x * lax.rsqrt(jnp.mean(x * x, axis=-1, keepdims=True) + RMS_EPS) * g_ref[...]


def _final_norm(x, g, t0, rows, tm):
    D = x.shape[1]
    return pl.pallas_call(
        _final_norm_kernel,
        grid=(rows // tm,),
        in_specs=[pl.BlockSpec((tm, D), lambda i: (i + t0 // tm, 0)), pl.BlockSpec((1, D), lambda i: (0, 0))],
        out_specs=pl.BlockSpec((tm, D), lambda i: (i, 0)),
        out_shape=jax.ShapeDtypeStruct((rows, D), F32),
        compiler_params=_params("parallel"),
        name="final_norm",
    )(x, g.reshape(1, D))


def _grid_pos_embedding(T, D):
    rows = T // GRID_W
    row = jnp.repeat(jnp.arange(rows), GRID_W)
    col = jnp.tile(jnp.arange(GRID_W), rows)
    quarter = D // 4
    omega = 1.0 / (POS_BASE ** (jnp.arange(quarter, dtype=F32) / quarter))

    def axis_emb(p):
        ang = p.astype(F32)[:, None] * omega[None]
        return jnp.concatenate([jnp.sin(ang), jnp.cos(ang)], axis=-1)

    return jnp.concatenate([axis_emb(row), axis_emb(col)], axis=-1)


def kernel(x_prompt, x_sample, state_delta, c, c_ctx, w_mod, b_mod, norm_mix, norm_ffn, norm_final, dn_w_in, dn_conv, dn_a_log, dn_dt_bias, dn_norm, dn_w_out, hy_w_in, hy_conv, hy_w1, hy_b1, hy_w2, hy_b2, hy_w3, hy_b3, hy_w4, hy_freq, hy_bias, hy_w_out, moe_w_router, moe_b_router, moe_w_in, moe_b_in, moe_w_out, moe_b_out):
    Bp, Tp, D = x_prompt.shape
    Bs, Ts, _ = x_sample.shape
    depth = w_mod.shape[0]
    H = state_delta.shape[3]
    Mp, Ms = Bp * Tp, Bs * Ts
    M = Mp + Ms
    assert Mp % Ts == 0 and Ts % Tp == 0 and Tp % DN_CHUNK == 0, "token groups must tile each other"
    tT = Tp
    tm_big = _tile(math.gcd(Mp, Ts), 1024)
    tm_mid = _tile(math.gcd(Mp, Ts), 512)
    tm_tok = min(256, Tp)

    def rowmap_for(tm):
        tile_start = np.arange(M // tm) * tm
        return jnp.asarray(np.where(tile_start < Mp, 0, 1 + (tile_start - Mp) // Ts), I32)

    rm_big, rm_mid, rm_tok = rowmap_for(tm_big), rowmap_for(tm_mid), rowmap_for(tm_tok)

    xs = x_sample + _grid_pos_embedding(Ts, D)[None]
    x = jnp.concatenate([x_prompt.reshape(Mp, D), xs.reshape(Ms, D)], axis=0)
    conv_start = np.arange(M // tT) * tT
    seq_len = np.where(conv_start < Mp, Tp, Ts)
    seq_pos = np.where(conv_start < Mp, conv_start % Tp, (conv_start - Mp) % Ts)
    has_prev = jnp.asarray(seq_pos > 0, I32)
    has_next = jnp.asarray(seq_pos + tT < seq_len, I32)

    n_cond = 1 + Bs
    r_pad = -(-n_cond // 8) * 8
    cond = jnp.zeros((r_pad, D), F32).at[0].set(c_ctx).at[1:n_cond].set(c)
    mod = _modulation(cond, w_mod, b_mod)

    new_states = []
    i_dn = i_hy = 0
    for l in range(depth):
        mod4 = mod[l].reshape(r_pad, 6, 1, D)
        if l % 2 == 0:
            i = i_dn
            i_dn += 1
            w_in = dn_w_in[i]
            n_main = 4 * H * HEAD_DIM
            w_ba = jnp.zeros((D, LANES), F32).at[:, :4 * H].set(w_in[:, n_main:]).astype(BF16)
            pm, ba = _fused_mm(_norm_mod_prologue, [x], [norm_mix[l]], [0, 1], w_in[:, :n_main].astype(BF16),
                               rm_big, mod4=mod4, side_w=w_ba, tm=tm_big, tn=1024)
            qkv, bg = _dn_act(pm, ba, dn_conv[i], dn_a_log[i], dn_dt_bias[i], has_prev, has_next, tT)
            hb = 2 if H % 2 == 0 else 1
            cols_p, gct_p = _delta_side_inputs(bg, 0, Bp, Tp, H, hb)
            cols_s, gct_s = _delta_side_inputs(bg, Mp, Bs, Ts, H, hb)
            og_p, s_fin = _delta(qkv, pm, cols_p, gct_p, dn_norm[i], None,
                                 n_batch=Bp, batch_off=0, T=Tp, hb=hb)
            og_s, _ = _delta(qkv, pm, cols_s, gct_s, dn_norm[i], state_delta[:, i].astype(F32),
                             n_batch=Bs, batch_off=Mp // Ts, T=Ts, hb=hb)
            new_states.append(s_fin.astype(state_delta.dtype))
            x = _fused_mm(functools.partial(_two_part_prologue, Mp // tm_big), [og_p], [], [],
                          dn_w_out[i].astype(BF16), rm_big, mod4=mod4, a_rest=og_s,
                          res=x, gate_chunk=2, out_dtype=F32, tm=tm_big, tn=1024)
        else:
            i = i_hy
            i_hy += 1
            u = _fused_mm(_norm_mod_prologue, [x], [norm_mix[l]], [0, 1], hy_w_in[i].astype(BF16),
                          rm_big, mod4=mod4, tm=tm_big, tn=1024)
            x0, s = _hy_pre(u, hy_conv[i], has_prev, has_next, tT)
            hy = (hy_w1[i], hy_b1[i], hy_w2[i], hy_b2[i], hy_w3[i], hy_b3[i], hy_w4[i], hy_freq[i])
            conv_p = _hyena_conv(s, Bp, 0, Tp, _hyena_filter_taps(Tp, *hy))
            conv_s = _hyena_conv(s, Bs, Mp, Ts, _hyena_filter_taps(Ts, *hy))
            x = _fused_mm(functools.partial(_hyena_gate_prologue, Mp // tm_mid), [conv_p, x0, s], [hy_bias[i]], [],
                          hy_w_out[i].astype(BF16), rm_mid, mod4=mod4, a_rest=conv_s,
                          res=x, gate_chunk=2, out_dtype=F32, tm=tm_mid, tn=1024)
        last = l == depth - 1
        x = _moe_layer(x, norm_ffn[l], mod4, rm_tok, tm_tok, moe_w_router[l], moe_b_router[l],
                       moe_w_in[l], moe_b_in[l], moe_w_out[l], moe_b_out[l],
                       final_g=norm_final if last else None, split_rows=Mp)

    y_prompt, y_sample = x
    return y_prompt.reshape(Bp, Tp, D), y_sample.reshape(Bs, Ts, D), jnp.stack(new_states, axis=1)
```

```python
import functools
import math

import numpy as np
import jax
import jax.numpy as jnp
from jax import lax
from jax.experimental import pallas as pl
from jax.experimental.pallas import tpu as pltpu

F32, BF16, I32 = jnp.float32, jnp.bfloat16, jnp.int32

V7X_VMEM_LIMIT_BYTES = 56 * 2**20
LANES = 128
BF16_ROWS = 16

GRID_W = 64
DN_CHUNK = 128
HEAD_DIM = 128
TOP_K = 4
RMS_EPS = 1e-6
L2_EPS = 1e-6
POS_BASE = 10000.0
SWIGLU_LIMIT = 7.0
SWIGLU_ALPHA = 1.702
HY_TARGET = 1e-2
HY_FAST = 0.3
HY_SLOW = 1.5

HIGHEST = lax.Precision.HIGHEST
NT_DIMS = (((1,), (1,)), ((), ()))
TN_DIMS = (((0,), (0,)), ((), ()))


def _params(*sem):
    return pltpu.CompilerParams(dimension_semantics=sem, vmem_limit_bytes=V7X_VMEM_LIMIT_BYTES)


def _tile(n, pref):
    if n <= pref:
        return n
    t = pref - pref % LANES
    while n % t:
        t -= LANES
    return t


def _silu(x):
    return x * jax.nn.sigmoid(x)


def _softplus(x):
    return jnp.maximum(x, 0.0) + jnp.log1p(jnp.exp(-jnp.abs(x)))


def _split_bf16(x):
    hi = x.astype(BF16)
    lo = (x - hi.astype(F32)).astype(BF16)
    return hi, lo


def _dot3(a, b):
    ah, al = _split_bf16(a)
    bh, bl = _split_bf16(b)
    return (jnp.dot(ah, bh, preferred_element_type=F32)
            + (jnp.dot(ah, bl, preferred_element_type=F32) + jnp.dot(al, bh, preferred_element_type=F32)))


def _modulation_kernel(c_ref, w_ref, b_ref, o_ref):
    s = _silu(c_ref[...]).astype(BF16)
    o_ref[0] = jnp.dot(s, w_ref[0].astype(BF16), preferred_element_type=F32) + b_ref[0]


def _modulation(cond, w_mod, b_mod):
    R, D = cond.shape
    L, _, N = w_mod.shape
    tn = _tile(N, 1024)
    return pl.pallas_call(
        _modulation_kernel,
        grid=(L, N // tn),
        in_specs=[pl.BlockSpec((R, D), lambda l, j: (0, 0)),
                  pl.BlockSpec((1, D, tn), lambda l, j: (l, 0, j)),
                  pl.BlockSpec((1, 1, tn), lambda l, j: (l, 0, j))],
        out_specs=pl.BlockSpec((1, R, tn), lambda l, j: (l, 0, j)),
        out_shape=jax.ShapeDtypeStruct((L, R, N), F32),
        compiler_params=_params("parallel", "parallel"),
        name="modulation",
    )(cond, w_mod, b_mod.reshape(L, 1, N))


def _fused_mm_kernel(prologue, n_a, n_v, n_m, has_res, has_side, rm_ref, *refs):
    del rm_ref
    a_refs, refs = refs[:n_a], refs[n_a:]
    v_refs, refs = refs[:n_v], refs[n_v:]
    m_refs, refs = refs[:n_m], refs[n_m:]
    w_ref, refs = refs[0], refs[1:]
    if has_res:
        res_ref, gate_ref, refs = refs[0], refs[1], refs[2:]
    if has_side:
        sw_ref, refs = refs[0], refs[1:]
    o_ref, refs = refs[0], refs[1:]
    if has_side:
        so_ref, refs = refs[0], refs[1:]
    a_scr = refs[0]

    row_tile = pl.program_id(0)

    @pl.when(pl.program_id(1) == 0)
    def _():
        a = prologue([r[...] for r in a_refs], [r[...] for r in v_refs], [r[0, 0] for r in m_refs], row_tile)
        a_scr[...] = a.astype(BF16)
        if has_side:
            so_ref[...] = jnp.dot(a_scr[...], sw_ref[...], preferred_element_type=F32)

    acc = jnp.dot(a_scr[...], w_ref[...], preferred_element_type=F32)
    if has_res:
        acc = res_ref[...] + gate_ref[0, 0] * acc
    o_ref[...] = acc.astype(o_ref.dtype)


def _fused_mm(prologue, a_ins, vec_ins, mod_ins, w, rowmap, *, mod4=None, res=None, gate_chunk=None,
              side_w=None, a_rest=None, out_dtype=BF16, tm, tn):
    a_split = None if a_rest is None else a_ins[0].shape[0]
    K = a_ins[0].shape[1]
    M = rowmap.shape[0] * tm
    N = w.shape[1]
    tn = _tile(N, tn)
    in_specs, args = [], []
    for a in a_ins:
        in_specs.append(pl.BlockSpec((tm, K), lambda i, j, rm: (i, 0)))
        args.append(a)
    if a_split is not None:
        n_first = a_split // tm
        in_specs[0] = pl.BlockSpec((tm, K), lambda i, j, rm: (jnp.minimum(i, n_first - 1), 0))
        in_specs.insert(1, pl.BlockSpec((tm, K), lambda i, j, rm: (jnp.maximum(i - n_first, 0), 0)))
        args.insert(1, a_rest)
    for v in vec_ins:
        in_specs.append(pl.BlockSpec((1, K), lambda i, j, rm: (0, 0)))
        args.append(v.reshape(1, K))
    for c in mod_ins:
        in_specs.append(pl.BlockSpec((1, 1, 1, K), lambda i, j, rm, c=c: (rm[i], c, 0, 0)))
        args.append(mod4)
    in_specs.append(pl.BlockSpec((K, tn), lambda i, j, rm: (0, j)))
    args.append(w)
    if res is not None:
        in_specs.append(pl.BlockSpec((tm, tn), lambda i, j, rm: (i, j)))
        args.append(res)
        in_specs.append(pl.BlockSpec((1, 1, 1, tn), lambda i, j, rm, c=gate_chunk: (rm[i], c, 0, j)))
        args.append(mod4)
    out_specs = [pl.BlockSpec((tm, tn), lambda i, j, rm: (i, j))]
    out_shape = [jax.ShapeDtypeStruct((M, N), out_dtype)]
    if side_w is not None:
        ns = side_w.shape[1]
        in_specs.append(pl.BlockSpec((K, ns), lambda i, j, rm: (0, 0)))
        args.append(side_w)
        out_specs.append(pl.BlockSpec((tm, ns), lambda i, j, rm: (i, 0)))
        out_shape.append(jax.ShapeDtypeStruct((M, ns), F32))
    kern = functools.partial(_fused_mm_kernel, prologue, len(a_ins) + (a_rest is not None), len(vec_ins),
                             len(mod_ins), res is not None, side_w is not None)
    outs = pl.pallas_call(
        kern,
        grid_spec=pltpu.PrefetchScalarGridSpec(
            num_scalar_prefetch=1, grid=(M // tm, N // tn), in_specs=in_specs, out_specs=out_specs,
            scratch_shapes=[pltpu.VMEM((tm, K), BF16)]),
        out_shape=out_shape,
        compiler_params=_params("parallel", "arbitrary"),
        name="fused_mm",
    )(rowmap, *args)
    return outs if side_w is not None else outs[0]


def _norm_mod_prologue(a, v, m, row_tile=None):
    x, g, (shift, scale) = a[0], v[0], m
    y = x * lax.rsqrt(jnp.mean(x * x, axis=-1, keepdims=True) + RMS_EPS) * g
    return y * (1.0 + scale) + shift


def _two_part_prologue(n_first_tiles, a, v, m, row_tile):
    return jnp.where(row_tile < n_first_tiles, a[0], a[1])


def _hyena_gate_prologue(n_first_tiles, a, v, m, row_tile):
    conv = jnp.where(row_tile < n_first_tiles, a[0], a[1]).astype(F32)
    x0, s = a[2].astype(F32), a[3].astype(F32)
    return x0 * (conv + s * v[0])


def _conv3(x, prev_row, next_row, w, rid):
    t = x.shape[0]
    xm = jnp.where(rid == 0, prev_row, pltpu.roll(x, 1, 0))
    xp = jnp.where(rid == t - 1, next_row, pltpu.roll(x, t - 1, 0))
    return xm * w[0:1] + x * w[1:2] + xp * w[2:3]


def _halo_specs(tT, width, col_block):
    hb = tT // BF16_ROWS
    return [
        pl.BlockSpec((tT, width), lambda i, hp, hn: (i, col_block)),
        pl.BlockSpec((BF16_ROWS, width), lambda i, hp, hn: (jnp.maximum(i * hb - 1, 0), col_block)),
        pl.BlockSpec((BF16_ROWS, width), lambda i, hp, hn: ((i + 1) * hb * hn[i], col_block)),
    ]


def _dn_act_kernel(n_heads, hp_ref, hn_ref, x_ref, xp_ref, xn_ref, ba_ref, cw_ref, al_ref, dtb_ref,
                   qkv_ref, bg_ref):
    i = pl.program_id(0)
    hp = hp_ref[i].astype(F32)
    hn = hn_ref[i].astype(F32)
    tT = x_ref.shape[0]
    rid = lax.broadcasted_iota(I32, (tT, LANES), 0)
    for cb in range(3 * n_heads):
        sl = slice(cb * LANES, (cb + 1) * LANES)
        x = x_ref[:, sl].astype(F32)
        pr = xp_ref[:, sl].astype(F32)[BF16_ROWS - 1:BF16_ROWS] * hp
        nx = xn_ref[:, sl].astype(F32)[0:1] * hn
        y = _silu(_conv3(x, pr, nx, cw_ref[:, sl], rid))
        if cb < 2 * n_heads:
            y = y * lax.rsqrt(jnp.sum(y * y, axis=-1, keepdims=True) + L2_EPS)
            if cb < n_heads:
                y = y * (HEAD_DIM ** -0.5)
        qkv_ref[:, sl] = y.astype(BF16)

    ba = ba_ref[...]
    lane = lax.broadcasted_iota(I32, (tT, LANES), 1)
    beta = jax.nn.sigmoid(ba)
    g = -jnp.exp(al_ref[...]) * _softplus(ba + dtb_ref[...])
    r = lax.broadcasted_iota(I32, (tT, tT), 0)
    c = lax.broadcasted_iota(I32, (tT, tT), 1)
    shift = DN_CHUNK.bit_length() - 1
    same = jnp.right_shift(r, shift) == jnp.right_shift(c, shift)
    l_pre = jnp.where(same & (c <= r), 1.0, 0.0).astype(F32)
    l_suf = jnp.where(same & (c >= r), 1.0, 0.0).astype(F32)
    g_pre = jnp.dot(l_pre, g, precision=HIGHEST, preferred_element_type=F32)
    g_suf = jnp.dot(l_suf, g, precision=HIGHEST, preferred_element_type=F32)
    gc = jnp.where(lane < 3 * n_heads, g_pre, g_suf)
    bg_ref[...] = jnp.where(lane < 2 * n_heads, beta, gc)


def _dn_act(pm, ba, conv_w, a_log, dt_bias, has_prev, has_next, tT):
    M = pm.shape[0]
    cw = conv_w.shape[1]
    n_heads = cw // (3 * HEAD_DIM)
    pad = lambda v: jnp.zeros((1, LANES), F32).at[0, 2 * n_heads:4 * n_heads].set(v.reshape(-1).astype(F32))
    vec = pl.BlockSpec((1, LANES), lambda i, hp, hn: (0, 0))
    return pl.pallas_call(
        functools.partial(_dn_act_kernel, n_heads),
        grid_spec=pltpu.PrefetchScalarGridSpec(
            num_scalar_prefetch=2, grid=(M // tT,),
            in_specs=_halo_specs(tT, cw, 0) + [
                pl.BlockSpec((tT, LANES), lambda i, hp, hn: (i, 0)),
                pl.BlockSpec((3, cw), lambda i, hp, hn: (0, 0)), vec, vec],
            out_specs=[pl.BlockSpec((tT, cw), lambda i, hp, hn: (i, 0)),
                       pl.BlockSpec((tT, LANES), lambda i, hp, hn: (i, 0))]),
        out_shape=[jax.ShapeDtypeStruct((M, cw), BF16), jax.ShapeDtypeStruct((M, LANES), F32)],
        compiler_params=_params("parallel"),
        name="dn_act",
    )(has_prev, has_next, pm, pm, pm, ba, conv_w, pad(a_log), pad(dt_bias))


def _merge_masks(ri, ci, n, lower):
    masks = []
    s = 0
    while (1 << s) < n:
        same = jnp.right_shift(ri, s + 1) == jnp.right_shift(ci, s + 1)
        hi_r = jnp.bitwise_and(jnp.right_shift(ri, s), 1)
        hi_c = jnp.bitwise_and(jnp.right_shift(ci, s), 1)
        off = (hi_r == 1) & (hi_c == 0) if lower else (hi_r == 0) & (hi_c == 1)
        masks.append(same & off)
        s += 1
    return masks


def _dot3_many(xs, ys):
    sx = [_split_bf16(x) for x in xs]
    sy = [_split_bf16(y) for y in ys]
    hh = [jnp.dot(x[0], y[0], preferred_element_type=F32) for x, y in zip(sx, sy)]
    hl = [jnp.dot(x[0], y[1], preferred_element_type=F32) for x, y in zip(sx, sy)]
    lh = [jnp.dot(x[1], y[0], preferred_element_type=F32) for x, y in zip(sx, sy)]
    return [a + (b + c) for a, b, c in zip(hh, hl, lh)]


def _dot1_many(xs, ys):
    return [jnp.dot(x.astype(BF16), y.astype(BF16), preferred_element_type=F32) for x, y in zip(xs, ys)]


def _unit_tri_inverse_many(mats, eye, masks, stricts):
    ts = [eye - jnp.where(m[0], a, 0.0) for a, m in zip(mats, masks)]
    for lvl in range(1, len(masks[0])):
        off = [jnp.where(m[lvl], a, 0.0) for a, m in zip(mats, masks)]
        upd = _dot1_many(_dot1_many(ts, off), ts)
        ts = [t - u for t, u in zip(ts, upd)]
    full = [eye + jnp.where(st, a, 0.0) for a, st in zip(mats, stricts)]
    res = [eye - p for p in _dot3_many(full, ts)]
    return [t + c for t, c in zip(ts, _dot1_many(ts, res))]


def _delta_kernel(hb, n_chunks, group, has_s0, *refs):
    if has_s0:
        q_ref, k_ref, v_ref, z_ref, cols_ref, gct_ref, ng_ref, s0_ref, o_ref, sfin_ref = refs[:10]
        scr = refs[10:]
    else:
        q_ref, k_ref, v_ref, z_ref, cols_ref, gct_ref, ng_ref, o_ref, sfin_ref = refs[:9]
        scr = refs[9:]
    u_scr, wq_scr, qk_scr, s_scr, o_scr = scr
    C = DN_CHUNK
    n = n_chunks
    ri = lax.broadcasted_iota(I32, (C, C), 0)
    ci = lax.broadcasted_iota(I32, (C, C), 1)
    eye = (ri == ci).astype(F32)
    incl = (ri >= ci, ri <= ci)

    def gate_cols(hh, d, rows):
        bcol = cols_ref[0, 0, rows, hh * 4 + d:hh * 4 + d + 1]
        gcol = cols_ref[0, 0, rows, hh * 4 + 2 + d:hh * 4 + 3 + d]
        gl = gcol[C - 1:C, :] if d == 0 else gcol[0:1, :]
        return bcol, gcol, gl

    def prep(it, carry):
        tri_masks = (_merge_masks(ri, ci, C, True), _merge_masks(ri, ci, C, False))
        tri_strict = (ri > ci, ri < ci)
        where_, mats, masks, stricts, rhss = [], [], [], [], []
        for gi in range(group):
            c = it * group + gi
            rows = pl.ds(pl.multiple_of(c * C, C), C)
            for hh in range(hb):
                ls = slice(hh * HEAD_DIM, (hh + 1) * HEAD_DIM)
                qb, kb16, vb = q_ref[0, rows, ls], k_ref[0, rows, ls], v_ref[0, rows, ls]
                qf, kf, vf = qb.astype(F32), kb16.astype(F32), vb.astype(F32)
                qkt = lax.dot_general(qb, kb16, NT_DIMS, preferred_element_type=F32)
                for d in range(2):
                    ch = hh * 2 + d
                    bcol, gcol, _ = gate_cols(hh, d, rows)
                    grow = gct_ref[0, hh, d, pl.ds(c, 1), :]
                    dm = jnp.where(incl[d], jnp.exp(jnp.where(incl[d], gcol - grow, 0.0)), 0.0)
                    kbeta = kf * bcol
                    eg = jnp.exp(gcol)
                    mats.append(lax.dot_general(kbeta.astype(BF16), kb16, NT_DIMS, preferred_element_type=F32) * dm)
                    masks.append(tri_masks[d])
                    stricts.append(tri_strict[d])
                    rhss.append(jnp.concatenate([vf * bcol, kbeta * eg], axis=1))
                    where_.append((ch, c, rows))
                    wq_scr[ch, pl.ds(pl.multiple_of(c * 2 * C + C, C), C), :] = (qf * eg).astype(BF16)
                    qk_scr[ch, rows, :] = (qkt * dm).astype(BF16)
        sols = _dot3_many(_unit_tri_inverse_many(mats, eye, masks, stricts), rhss)
        for (ch, c, rows), sol in zip(where_, sols):
            u_scr[ch, rows, :] = sol[:, :HEAD_DIM]
            wq_scr[ch, pl.ds(pl.multiple_of(c * 2 * C, 2 * C), C), :] = sol[:, HEAD_DIM:].astype(BF16)
        return carry

    lax.fori_loop(0, n // group, prep, 0)

    for hh in range(hb):
        for d in range(2):
            s_scr[hh * 2 + d] = s0_ref[0, d, hh] if has_s0 else jnp.zeros((HEAD_DIM, HEAD_DIM), F32)

    T = n * C
    tr = min(T, 256)

    def clear(b, carry):
        o_scr[:, pl.ds(pl.multiple_of(b * tr, tr), tr), :] = jnp.zeros((hb, tr, HEAD_DIM), F32)
        return carry

    lax.fori_loop(0, T // tr, clear, 0)

    def scan(it, carry):
        chains = [(hh, d) for hh in range(hb) for d in range(2)]
        cs = [it if d == 0 else n - 1 - it for _, d in chains]
        rows = [pl.ds(pl.multiple_of(c * C, C), C) for c in cs]
        ss = [s_scr[hh * 2 + d] for hh, d in chains]
        rs = [jnp.dot(wq_scr[hh * 2 + d, pl.ds(pl.multiple_of(c * 2 * C, 2 * C), 2 * C), :], s.astype(BF16),
                      preferred_element_type=F32) for (hh, d), c, s in zip(chains, cs, ss)]
        vns = [(u_scr[hh * 2 + d, rw, :] - r[:C]).astype(BF16) for (hh, d), rw, r in zip(chains, rows, rs)]
        for i, (hh, d) in enumerate(chains):
            _, gcol, gl = gate_cols(hh, d, rows[i])
            kt = (k_ref[0, rows[i], hh * HEAD_DIM:(hh + 1) * HEAD_DIM].astype(F32) * jnp.exp(gl - gcol)).astype(BF16)
            s_scr[hh * 2 + d] = (ss[i] * jnp.exp(gl)
                                 + lax.dot_general(kt, vns[i], TN_DIMS, preferred_element_type=F32))
        for i, (hh, d) in enumerate(chains):
            o = rs[i][C:] + jnp.dot(qk_scr[hh * 2 + d, rows[i], :], vns[i], preferred_element_type=F32)
            o_scr[hh, rows[i], :] += o
        return carry

    lax.fori_loop(0, n, scan, 0)

    for hh in range(hb):
        for d in range(2):
            sfin_ref[0, d, hh] = s_scr[hh * 2 + d]

    def gate(b, carry):
        rows = pl.ds(pl.multiple_of(b * tr, tr), tr)
        for hh in range(hb):
            ls = slice(hh * HEAD_DIM, (hh + 1) * HEAD_DIM)
            o = o_scr[hh, rows, :]
            o = o * lax.rsqrt(jnp.mean(o * o, axis=-1, keepdims=True) + RMS_EPS) * ng_ref[...]
            o_ref[0, rows, ls] = (o * _silu(z_ref[0, rows, ls].astype(F32))).astype(BF16)
        return carry

    lax.fori_loop(0, T // tr, gate, 0)


def _delta(qkv, pm, cols, gct, norm_g, s0, *, n_batch, batch_off, T, hb):
    M, cw = qkv.shape
    H = cw // (3 * HEAD_DIM)
    n = T // DN_CHUNK
    W = hb * HEAD_DIM
    nb = H // hb
    qkv3 = qkv.reshape(M // T, T, cw)
    pm3 = pm.reshape(M // T, T, pm.shape[1])
    once = pl.Buffered(1)
    blk = lambda off: pl.BlockSpec((1, T, W), lambda b, j, off=off: (b + batch_off, 0, off * nb + j),
                                   pipeline_mode=once)
    in_specs = [blk(0), blk(1), blk(2), blk(3),
                pl.BlockSpec((1, 1, T, 4 * hb), lambda b, j: (b, j, 0, 0), pipeline_mode=once),
                pl.BlockSpec((1, hb, 2, n, DN_CHUNK), lambda b, j: (b, j, 0, 0, 0)),
                pl.BlockSpec((1, HEAD_DIM), lambda b, j: (0, 0))]
    args = [qkv3, qkv3, qkv3, pm3, cols, gct, norm_g.reshape(1, HEAD_DIM)]
    if s0 is not None:
        in_specs.append(pl.BlockSpec((1, 2, hb, HEAD_DIM, HEAD_DIM), lambda b, j: (b, 0, j, 0, 0)))
        args.append(s0)
    nch = 2 * hb
    group = math.gcd(n, max(1, 16 // nch))
    out, sfin = pl.pallas_call(
        functools.partial(_delta_kernel, hb, n, group, s0 is not None),
        grid=(n_batch, nb),
        in_specs=in_specs,
        out_specs=[pl.BlockSpec((1, T, W), lambda b, j: (b, 0, j)),
                   pl.BlockSpec((1, 2, hb, HEAD_DIM, HEAD_DIM), lambda b, j: (b, 0, j, 0, 0))],
        out_shape=[jax.ShapeDtypeStruct((n_batch, T, H * HEAD_DIM), BF16),
                   jax.ShapeDtypeStruct((n_batch, 2, H, HEAD_DIM, HEAD_DIM), F32)],
        scratch_shapes=[pltpu.VMEM((nch, T, HEAD_DIM), F32),
                        pltpu.VMEM((nch, 2 * T, HEAD_DIM), BF16),
                        pltpu.VMEM((nch, T, DN_CHUNK), BF16),
                        pltpu.VMEM((nch, HEAD_DIM, HEAD_DIM), F32),
                        pltpu.VMEM((hb, T, HEAD_DIM), F32)],
        compiler_params=_params("parallel", "parallel"),
        name="delta_rule",
    )(*args)
    return out.reshape(n_batch * T, H * HEAD_DIM), sfin


def _delta_side_inputs(bg, t0, n_batch, T, H, hb):
    b5 = bg[t0:t0 + n_batch * T, :4 * H].reshape(n_batch, T, 2, 2, H)
    cols = jnp.transpose(b5, (0, 4, 1, 2, 3)).reshape(n_batch, H // hb, hb, T, 4)
    cols = jnp.transpose(cols, (0, 1, 3, 2, 4)).reshape(n_batch, H // hb, T, 4 * hb)
    gct = jnp.transpose(b5[:, :, 1], (0, 3, 2, 1)).reshape(n_batch, H, 2, T // DN_CHUNK, DN_CHUNK)
    return cols, gct


def _hy_pre_kernel(D, hp_ref, hn_ref, x_ref, xp_ref, xn_ref, cw_ref, x0_ref, s_ref):
    i = pl.program_id(0)
    hp = hp_ref[i].astype(F32)
    hn = hn_ref[i].astype(F32)
    tT = x_ref.shape[0]
    W = min(D, 2 * LANES)
    rid = lax.broadcasted_iota(I32, (tT, W), 0)

    def conv(cb, part):
        sl = slice(part * D + cb * W, part * D + (cb + 1) * W)
        x = x_ref[:, sl].astype(F32)
        pr = xp_ref[:, sl].astype(F32)[BF16_ROWS - 1:BF16_ROWS] * hp
        nx = xn_ref[:, sl].astype(F32)[0:1] * hn
        return _conv3(x, pr, nx, cw_ref[:, sl], rid)

    for cb in range(D // W):
        sl = slice(cb * W, (cb + 1) * W)
        x0_ref[:, sl] = conv(cb, 0).astype(BF16)
        s_ref[:, sl] = (conv(cb, 1) * conv(cb, 2)).astype(BF16)


def _hy_pre(u, conv_w, has_prev, has_next, tT):
    M, W3 = u.shape
    D = W3 // 3
    out = pl.BlockSpec((tT, D), lambda i, hp, hn: (i, 0))
    return pl.pallas_call(
        functools.partial(_hy_pre_kernel, D),
        grid_spec=pltpu.PrefetchScalarGridSpec(
            num_scalar_prefetch=2, grid=(M // tT,),
            in_specs=_halo_specs(tT, W3, 0) + [pl.BlockSpec((3, W3), lambda i, hp, hn: (0, 0))],
            out_specs=[out, out]),
        out_shape=[jax.ShapeDtypeStruct((M, D), BF16)] * 2,
        compiler_params=_params("parallel"),
        name="hy_pre",
    )(has_prev, has_next, u, u, u, conv_w)


def _bmm_kernel(a_ref, b_ref, o_ref, acc_ref):
    k = pl.program_id(3)
    part = jnp.dot(a_ref[...], b_ref[0], preferred_element_type=F32)

    @pl.when(k == 0)
    def _():
        acc_ref[...] = part

    @pl.when(k > 0)
    def _():
        acc_ref[...] += part

    @pl.when(k == pl.num_programs(3) - 1)
    def _():
        o_ref[0] = acc_ref[...].astype(o_ref.dtype)


def _bmm(a, b, b_batch_off=0, n_batch=None, out_dtype=BF16, tm=1024, tn=1024, tk=2048):
    M, K = a.shape
    N = b.shape[2]
    nb = b.shape[0] if n_batch is None else n_batch
    tm, tn, tk = _tile(M, tm), _tile(N, tn), _tile(K, tk)
    return pl.pallas_call(
        _bmm_kernel,
        grid=(nb, M // tm, N // tn, K // tk),
        in_specs=[pl.BlockSpec((tm, tk), lambda i, m, n, k: (m, k)),
                  pl.BlockSpec((1, tk, tn), lambda i, m, n, k: (i + b_batch_off, k, n))],
        out_specs=pl.BlockSpec((1, tm, tn), lambda i, m, n, k: (i, m, n)),
        out_shape=jax.ShapeDtypeStruct((nb, M, N), out_dtype),
        scratch_shapes=[pltpu.VMEM((tm, tn), F32)],
        compiler_params=_params("parallel", "parallel", "parallel", "arbitrary"),
        name="dft_mm",
    )(a, b)


def _spec_prod_kernel(inv_n, s_ref, k_ref, y_ref):
    sc, ss = s_ref[0, 0].astype(F32), s_ref[0, 1].astype(F32)
    kc, ks = k_ref[0], k_ref[1]
    first = (lax.broadcasted_iota(I32, sc.shape, 0) == 0) & (pl.program_id(1) == 0)
    y_ref[0, 0] = (jnp.where(first, sc * kc, 2.0 * (sc * kc - ss * ks)) * inv_n).astype(y_ref.dtype)
    y_ref[0, 1] = (jnp.where(first, ss * ks, 2.0 * (sc * ks + ss * kc)) * inv_n).astype(y_ref.dtype)


def _spec_prod(sf, kf):
    B, _, T, D = sf.shape
    tr, tc = _tile(T, 256), _tile(D, 1024)
    return pl.pallas_call(
        functools.partial(_spec_prod_kernel, 1.0 / (2 * T)),
        grid=(B, T // tr, D // tc),
        in_specs=[pl.BlockSpec((1, 2, tr, tc), lambda b, i, j: (b, 0, i, j)),
                  pl.BlockSpec((2, tr, tc), lambda b, i, j: (0, i, j))],
        out_specs=pl.BlockSpec((1, 2, tr, tc), lambda b, i, j: (b, 0, i, j)),
        out_shape=jax.ShapeDtypeStruct(sf.shape, BF16),
        compiler_params=_params("parallel", "parallel", "parallel"),
        name="spec_prod",
    )(sf, kf)


def _dft_matrix(T):
    k = jnp.arange(T, dtype=I32)[:, None]
    t = jnp.arange(T, dtype=I32)[None, :]
    ang = ((k * t) % (2 * T)).astype(F32) * (math.pi / T)
    nyq = jnp.where(t % 2 == 0, 1.0, -1.0).astype(F32)
    sin = jnp.where(k == 0, nyq, jnp.sin(ang))
    return jnp.concatenate([jnp.cos(ang), sin], axis=0).astype(BF16)


def _hyena_filter_taps(L, w1, b1, w2, b2, w3, b3, w4, freq):
    D = w4.shape[1] // 2
    n_bands = (w1.shape[0] - 1) // 2
    pos = jnp.arange(L, dtype=F32)
    t = pos / max(L - 1, 1)
    bands = jnp.linspace(1e-4, n_bands - 1, n_bands, dtype=F32)
    ang = (2.0 * math.pi / L) * pos[:, None] * bands[None]
    feats = jnp.concatenate([t[:, None], jnp.cos(ang), -jnp.sin(ang)], axis=-1)
    zf = jnp.sin(freq[0] * (feats @ w1 + b1))
    zf = jnp.sin(freq[1] * (zf @ w2 + b2))
    zf = jnp.sin(freq[2] * (zf @ w3 + b3))
    filt = (zf @ w4).reshape(L, 2, D)
    deltas = jnp.abs(jnp.linspace(math.log(HY_TARGET) / HY_SLOW, math.log(HY_TARGET) / HY_FAST, D, dtype=F32))
    filt = filt * jnp.exp(-t[:, None, None] * deltas[None, None])
    f = filt[:, 0]
    b = filt[:, 1].at[0].set(0.0)
    scale = lax.rsqrt(jnp.sum(f * f, axis=0) + jnp.sum(b * b, axis=0) + 1e-6)
    return f * scale, b * scale


def _hyena_conv(s, n_batch, t0, T, taps):
    M, D = s.shape
    f, b = taps
    fwd = _dft_matrix(T)
    kf = _bmm(fwd, jnp.concatenate([f, b], axis=1).astype(BF16)[None], out_dtype=F32)[0]
    p, q = kf[:, :D].reshape(2, T, D), kf[:, D:].reshape(2, T, D)
    first = (jnp.arange(T) == 0)[:, None]
    kspec = jnp.stack([p[0] + q[0], jnp.where(first, p[1] + q[1], p[1] - q[1])])
    s3 = s.reshape(M // T, T, D)
    sf = _bmm(fwd, s3, b_batch_off=t0 // T, n_batch=n_batch)
    y = _spec_prod(sf.reshape(n_batch, 2, T, D), kspec).reshape(n_batch, 2 * T, D)
    conv = _bmm(fwd.T, y)
    return conv.reshape(n_batch * T, D)


def _store_token_major(ref, x):
    t, d = x.shape
    nb = d // LANES
    for j in range(nb):
        ref[pl.ds(j, t, stride=nb), :] = x[:, j * LANES:(j + 1) * LANES]


def _load_token_major(ref, t, j, lead=()):
    nb = ref.shape[-2] // t
    return ref[lead + (pl.ds(j, t, stride=nb), slice(None))]


def _router_kernel(n_experts, rm_ref, x_ref, g_ref, sh_ref, sc_ref, wr_ref, br_ref,
                   h_ref, ti_ref, tg_ref, rk_ref, cnt_ref, base_scr):
    del rm_ref

    @pl.when(pl.program_id(0) == 0)
    def _():
        base_scr[...] = jnp.zeros(base_scr.shape, F32)

    h = _norm_mod_prologue([x_ref[...]], [g_ref[...]], (sh_ref[0, 0], sc_ref[0, 0]))
    _store_token_major(h_ref, h)
    logits = jnp.dot(h, wr_ref[...], precision=HIGHEST, preferred_element_type=F32) + br_ref[...]
    tm = logits.shape[0]
    lane = lax.broadcasted_iota(I32, logits.shape, 1)
    lane_f = lane.astype(F32)
    neg = jnp.float32(-jnp.inf)
    l = jnp.where(lane < n_experts, logits, neg)
    ti = jnp.zeros(logits.shape, I32)
    tl = jnp.full(logits.shape, neg, F32)
    onehot = jnp.zeros(logits.shape, F32)
    picks = []
    for r in range(TOP_K):
        m = jnp.max(l, axis=-1, keepdims=True)
        idx = jnp.min(jnp.where(l == m, lane_f, float(LANES)), axis=-1, keepdims=True).astype(I32)
        ti = jnp.where(lane == r, idx, ti)
        tl = jnp.where(lane == r, m, tl)
        l = jnp.where(lane == idx, neg, l)
        onehot = jnp.where(lane == idx, 1.0, onehot)
        picks.append(idx)
    e = jnp.exp(tl - jnp.max(tl, axis=-1, keepdims=True))
    ti_ref[...] = ti
    tg_ref[...] = e / jnp.sum(e, axis=-1, keepdims=True)

    r_i = lax.broadcasted_iota(I32, (tm, tm), 0)
    c_i = lax.broadcasted_iota(I32, (tm, tm), 1)
    earlier = jnp.where(c_i < r_i, 1.0, 0.0).astype(BF16)
    before = jnp.dot(earlier, onehot.astype(BF16), preferred_element_type=F32) + base_scr[...]
    rk = jnp.zeros(logits.shape, F32)
    for r in range(TOP_K):
        mine = jnp.sum(jnp.where(lane == picks[r], before, 0.0), axis=-1, keepdims=True)
        rk = jnp.where(lane == r, mine, rk)
    rk_ref[...] = rk.astype(I32)
    base_scr[...] += jnp.sum(onehot, axis=0, keepdims=True)
    cnt_ref[...] = base_scr[...].astype(I32)


def _router(x, g, mod4, rowmap, w_router, b_router, tm):
    M, D = x.shape
    E = w_router.shape[1]
    wr = jnp.zeros((D, LANES), F32).at[:, :E].set(w_router)
    br = jnp.zeros((1, LANES), F32).at[0, :E].set(b_router)
    row = lambda c: pl.BlockSpec((1, 1, 1, D), lambda i, rm, c=c: (rm[i], c, 0, 0))
    tile = lambda w: pl.BlockSpec((tm, w), lambda i, rm: (i, 0))
    return pl.pallas_call(
        functools.partial(_router_kernel, E),
        grid_spec=pltpu.PrefetchScalarGridSpec(
            num_scalar_prefetch=1, grid=(M // tm,),
            in_specs=[tile(D), pl.BlockSpec((1, D), lambda i, rm: (0, 0)), row(3), row(4),
                      pl.BlockSpec((D, LANES), lambda i, rm: (0, 0)),
                      pl.BlockSpec((1, LANES), lambda i, rm: (0, 0))],
            out_specs=[pl.BlockSpec((tm * D // LANES, LANES), lambda i, rm: (i, 0)),
                       tile(LANES), tile(LANES), tile(LANES),
                       pl.BlockSpec((1, LANES), lambda i, rm: (0, 0))],
            scratch_shapes=[pltpu.VMEM((1, LANES), F32)]),
        out_shape=[jax.ShapeDtypeStruct((M * D // LANES, LANES), F32), jax.ShapeDtypeStruct((M, LANES), I32),
                   jax.ShapeDtypeStruct((M, LANES), F32), jax.ShapeDtypeStruct((M, LANES), I32),
                   jax.ShapeDtypeStruct((1, LANES), I32)],
        compiler_params=_params("arbitrary"),
        name="router",
    )(rowmap, x, g.reshape(1, D), mod4, mod4, wr, br)


def _dest_kernel(ps_ref, ti_ref, rk_ref, o_ref):
    ti = ti_ref[...]
    start = jnp.zeros(ti.shape, I32)
    for e in range(ps_ref.shape[0]):
        start = jnp.where(ti == e, ps_ref[e], start)
    o_ref[...] = start + rk_ref[...]


def _dest_rows(seg_start, ti, rk, tm):
    M = ti.shape[0]
    tile = pl.BlockSpec((tm, LANES), lambda i, ps: (i, 0))
    return pl.pallas_call(
        _dest_kernel,
        grid_spec=pltpu.PrefetchScalarGridSpec(num_scalar_prefetch=1, grid=(M // tm,),
                                               in_specs=[tile, tile], out_specs=tile),
        out_shape=jax.ShapeDtypeStruct((M, LANES), I32),
        compiler_params=_params("parallel"),
        name="moe_dest",
    )(seg_start, ti, rk)


def _load_tile_dest(dest_ref, dest_smem, sem):
    cp = pltpu.make_async_copy(dest_ref, dest_smem, sem)
    cp.start()
    cp.wait()


def _zero_pad_rows(nb, pad_start_ref, pad_len_ref, o_ref, zero_ref, sem, start):
    def copy(pos, p):
        cp = pltpu.make_async_copy(zero_ref.at[pl.ds(0, p * nb)],
                                   o_ref.at[pl.ds(pl.multiple_of(pos * nb, nb), p * nb)], sem)
        if start:
            cp.start()
        else:
            cp.wait()

    def per_expert(e, carry):
        pos, length = pad_start_ref[e], pad_len_ref[e]
        p = zero_ref.shape[0] // nb // 2
        while p >= 1:
            hit = jnp.bitwise_and(length, p) != 0

            @pl.when(hit)
            def _(pos=pos, p=p):
                copy(pos, p)

            pos = pos + jnp.where(hit, p, 0)
            p //= 2
        return carry

    lax.fori_loop(0, pad_start_ref.shape[0], per_expert, 0)


def _dispatch_kernel(nb, pad_start_ref, pad_len_ref, tail_ref, dest_ref, h_ref, o_ref, dest_smem, zero_ref, sem):
    tm = h_ref.shape[0] // nb
    tz = zero_ref.shape[0]

    @pl.when(pl.program_id(0) == 0)
    def _():
        zero_ref[...] = jnp.zeros(zero_ref.shape, F32)

        def tail_copy(t):
            return pltpu.make_async_copy(zero_ref, o_ref.at[pl.ds(pl.multiple_of(t * tz, tz), tz)], sem.at[2])

        def tail_start(t, carry):
            tail_copy(t).start()
            return carry

        def tail_wait(t, carry):
            tail_copy(t).wait()
            return carry

        _zero_pad_rows(nb, pad_start_ref, pad_len_ref, o_ref, zero_ref, sem.at[2], True)
        lax.fori_loop(tail_ref[0], tail_ref[1], tail_start, 0)
        _zero_pad_rows(nb, pad_start_ref, pad_len_ref, o_ref, zero_ref, sem.at[2], False)
        lax.fori_loop(tail_ref[0], tail_ref[1], tail_wait, 0)

    _load_tile_dest(dest_ref, dest_smem, sem.at[0])

    def row_copy(j):
        dst = dest_smem[jnp.right_shift(j, 7), jnp.bitwise_and(j, LANES - 1)]
        return pltpu.make_async_copy(h_ref.at[pl.ds(pl.multiple_of(jnp.right_shift(j, 2) * nb, nb), nb)],
                                     o_ref.at[pl.ds(pl.multiple_of(dst * nb, nb), nb)], sem.at[1])

    def start(j, carry):
        row_copy(j).start()
        return carry

    def wait(j, carry):
        row_copy(j).wait()
        return carry

    lax.fori_loop(0, tm * TOP_K, start, 0, unroll=8)
    lax.fori_loop(0, tm * TOP_K, wait, 0, unroll=8)


def _dispatch(h, nb, dest2d, pad_start, pad_len, tail, n_rows, tm, tm_rows):
    M = h.shape[0] // nb
    nd = tm * TOP_K // LANES
    return pl.pallas_call(
        functools.partial(_dispatch_kernel, nb),
        grid_spec=pltpu.PrefetchScalarGridSpec(
            num_scalar_prefetch=3, grid=(M // tm,),
            in_specs=[pl.BlockSpec((nd, LANES), lambda i, ps, pn, tl: (i, 0)),
                      pl.BlockSpec((tm * nb, LANES), lambda i, ps, pn, tl: (i, 0))],
            out_specs=pl.BlockSpec(memory_space=pl.ANY),
            scratch_shapes=[pltpu.SMEM((nd, LANES), I32), pltpu.VMEM((tm_rows * nb, LANES), F32),
                            pltpu.SemaphoreType.DMA((3,))]),
        out_shape=jax.ShapeDtypeStruct((n_rows * nb, LANES), F32),
        compiler_params=_params("arbitrary"),
        name="moe_dispatch",
    )(pad_start, pad_len, tail, dest2d, h)


def _combine_kernel(final, rm_ref, dest_ref, y_ref, tg_ref, x_ref, gate_ref, ng_ref, *refs):
    del rm_ref
    n_out = 1 if final is None else 2
    o_ref = refs[0] if final is None else refs[:2]
    buf, dest_smem, sem = refs[n_out:]
    tm = x_ref.shape[0]
    _load_tile_dest(dest_ref, dest_smem, sem.at[0])

    nb = x_ref.shape[1] // LANES

    def row_copy(j):
        src = dest_smem[jnp.right_shift(j, 7), jnp.bitwise_and(j, LANES - 1)]
        return pltpu.make_async_copy(
            y_ref.at[pl.ds(pl.multiple_of(src * nb, nb), nb)],
            buf.at[jnp.bitwise_and(j, TOP_K - 1), pl.ds(pl.multiple_of(jnp.right_shift(j, 2) * nb, nb), nb)],
            sem.at[1])

    def start(j, carry):
        row_copy(j).start()
        return carry

    def wait(j, carry):
        row_copy(j).wait()
        return carry

    lax.fori_loop(0, tm * TOP_K, start, 0, unroll=8)
    lax.fori_loop(0, tm * TOP_K, wait, 0, unroll=8)
    tg = tg_ref[...]
    blocks = []
    for j in range(nb):
        acc = tg[:, 0:1] * _load_token_major(buf, tm, j, (0,))
        for k in range(1, TOP_K):
            acc = acc + tg[:, k:k + 1] * _load_token_major(buf, tm, j, (k,))
        blocks.append(acc)
    x = x_ref[...] + gate_ref[0, 0] * jnp.concatenate(blocks, axis=1)
    if final is None:
        o_ref[...] = x
    else:
        x = x * lax.rsqrt(jnp.mean(x * x, axis=-1, keepdims=True) + RMS_EPS) * ng_ref[...]
        first_ref, second_ref = o_ref

        @pl.when(pl.program_id(0) < final)
        def _():
            first_ref[...] = x

        @pl.when(pl.program_id(0) >= final)
        def _():
            second_ref[...] = x


def _combine(y_rows, dest2d, tg, x, mod4, rowmap, norm_g, split_rows, tm):
    M, D = x.shape
    nd = tm * TOP_K // LANES
    if split_rows is None:
        final = None
        out_specs = pl.BlockSpec((tm, D), lambda i, rm: (i, 0))
        out_shape = jax.ShapeDtypeStruct((M, D), F32)
    else:
        final = split_rows // tm
        out_specs = [pl.BlockSpec((tm, D), lambda i, rm: (jnp.minimum(i, final - 1), 0)),
                     pl.BlockSpec((tm, D), lambda i, rm: (jnp.maximum(i - final, 0), 0))]
        out_shape = [jax.ShapeDtypeStruct((split_rows, D), F32), jax.ShapeDtypeStruct((M - split_rows, D), F32)]
    return _combine_call(final, out_specs, out_shape, y_rows, dest2d, tg, x, mod4, rowmap, norm_g, tm, nd)


def _combine_call(final, out_specs, out_shape, y_rows, dest2d, tg, x, mod4, rowmap, norm_g, tm, nd):
    M, D = x.shape
    return pl.pallas_call(
        functools.partial(_combine_kernel, final),
        grid_spec=pltpu.PrefetchScalarGridSpec(
            num_scalar_prefetch=1, grid=(M // tm,),
            in_specs=[pl.BlockSpec((nd, LANES), lambda i, rm: (i, 0)),
                      pl.BlockSpec(memory_space=pl.ANY),
                      pl.BlockSpec((tm, LANES), lambda i, rm: (i, 0)),
                      pl.BlockSpec((tm, D), lambda i, rm: (i, 0)),
                      pl.BlockSpec((1, 1, 1, D), lambda i, rm: (rm[i], 5, 0, 0)),
                      pl.BlockSpec((1, D), lambda i, rm: (0, 0))],
            out_specs=out_specs,
            scratch_shapes=[pltpu.VMEM((TOP_K, tm * D // LANES, LANES), F32), pltpu.SMEM((nd, LANES), I32),
                            pltpu.SemaphoreType.DMA((2,))]),
        out_shape=out_shape,
        compiler_params=_params("arbitrary"),
        name="moe_combine",
    )(rowmap, dest2d, y_rows, tg, x, mod4, norm_g.reshape(1, D))


def _moe_kernel(te_ref, nv_ref, x_ref, wg_ref, wu_ref, bg_ref, bu_ref, wo_ref, bo_ref, o_ref,
                acc_ref, xb_ref):
    del te_ref
    t, f = pl.program_id(0), pl.program_id(1)
    last = pl.num_programs(1) - 1
    valid = t < nv_ref[0]

    tm, d = xb_ref.shape
    nb = d // LANES

    @pl.when(valid & (f == 0))
    def _():
        for j in range(nb):
            xb_ref[:, j * LANES:(j + 1) * LANES] = _load_token_major(x_ref, tm, j).astype(BF16)

    @pl.when(valid)
    def _():
        x = xb_ref[...]
        g = jnp.dot(x, wg_ref[0], preferred_element_type=F32) + bg_ref[0]
        u = jnp.dot(x, wu_ref[0], preferred_element_type=F32) + bu_ref[0]
        g = jnp.minimum(g, SWIGLU_LIMIT)
        u = jnp.clip(u, -SWIGLU_LIMIT, SWIGLU_LIMIT)
        h = ((u + 1.0) * g * jax.nn.sigmoid(SWIGLU_ALPHA * g)).astype(BF16)
        part = jnp.dot(h, wo_ref[0], preferred_element_type=F32)

        @pl.when(f == 0)
        def _():
            acc_ref[...] = part + bo_ref[0]

        @pl.when(f > 0)
        def _():
            acc_ref[...] += part

        @pl.when(f == last)
        def _():
            _store_token_major(o_ref, acc_ref[...])

    @pl.when(jnp.logical_not(valid) & (f == last))
    def _():
        o_ref[...] = jnp.zeros(o_ref.shape, o_ref.dtype)


def _moe_experts(x_rows, tile_expert, n_valid, w_in, b_in, w_out, b_out, tm, tf):
    E, D, F2 = w_in.shape
    nb = D // LANES
    R = x_rows.shape[0] // nb
    F = F2 // 2
    tf = _tile(F, tf)
    nf = F // tf

    def fi(t, f, nv):
        return jnp.where(t < nv[0], f, nf - 1)

    def ti(t, nv):
        return jnp.minimum(t, jnp.maximum(nv[0] - 1, 0))

    return pl.pallas_call(
        _moe_kernel,
        grid_spec=pltpu.PrefetchScalarGridSpec(
            num_scalar_prefetch=2, grid=(R // tm, nf),
            in_specs=[pl.BlockSpec((tm * nb, LANES), lambda t, f, te, nv: (ti(t, nv), 0)),
                      pl.BlockSpec((1, D, tf), lambda t, f, te, nv: (te[t], 0, fi(t, f, nv))),
                      pl.BlockSpec((1, D, tf), lambda t, f, te, nv: (te[t], 0, nf + fi(t, f, nv))),
                      pl.BlockSpec((1, 1, tf), lambda t, f, te, nv: (te[t], 0, fi(t, f, nv))),
                      pl.BlockSpec((1, 1, tf), lambda t, f, te, nv: (te[t], 0, nf + fi(t, f, nv))),
                      pl.BlockSpec((1, tf, D), lambda t, f, te, nv: (te[t], fi(t, f, nv), 0)),
                      pl.BlockSpec((1, 1, D), lambda t, f, te, nv: (te[t], 0, 0))],
            out_specs=pl.BlockSpec((tm * nb, LANES), lambda t, f, te, nv: (t, 0)),
            scratch_shapes=[pltpu.VMEM((tm, D), F32), pltpu.VMEM((tm, D), BF16)]),
        out_shape=jax.ShapeDtypeStruct((R * nb, LANES), F32),
        compiler_params=_params("parallel", "arbitrary"),
        name="moe_experts",
    )(tile_expert, n_valid, x_rows, w_in, w_in, b_in.reshape(E, 1, F2), b_in.reshape(E, 1, F2),
      w_out, b_out.reshape(E, 1, D))


def _moe_layer(x, norm_g, mod4, rowmap, tm_tok, w_router, b_router, w_in, b_in, w_out, b_out, first_expert,
               final_g=None, split_rows=None, tm=512, tf=1024):
    M, D = x.shape
    E = w_router.shape[1]
    h, ti, tg, rk, cnt = _router(x, norm_g, mod4, rowmap, w_router, b_router, tm_tok)
    A = M * TOP_K
    n_tiles = -(-A // tm) + E
    counts = cnt[0, :E]
    ptiles = (counts + tm - 1) // tm
    pend = jnp.cumsum(ptiles)
    pstart = pend - ptiles
    n_valid = pend[-1]
    tix = jnp.minimum(jnp.arange(n_tiles, dtype=I32), jnp.maximum(n_valid - 1, 0))
    tile_expert = jnp.minimum(jnp.searchsorted(pend, tix, side='right'), E - 1).astype(I32)
    dest = _dest_rows((pstart * tm).astype(I32), ti, rk, tm_tok)
    dest2d = dest[:, :TOP_K].reshape(A // LANES, LANES)
    tail = jnp.stack([n_valid, jnp.asarray(n_tiles, n_valid.dtype)]).astype(I32)
    x_rows = _dispatch(h, D // LANES, dest2d, (pstart * tm + counts).astype(I32), (ptiles * tm - counts).astype(I32),
                       tail, n_tiles * tm, tm_tok, tm)
    y_rows = _moe_experts(x_rows, tile_expert + first_expert, n_valid.reshape(1).astype(I32),
                          w_in, b_in, w_out, b_out, tm, tf)
    return _combine(y_rows, dest2d, tg, x, mod4, rowmap, norm_g if final_g is None else final_g,
                    split_rows if final_g is not None else None, tm_tok)


def _grid_pos_embedding(T, D):
    rows = T // GRID_W
    row = jnp.repeat(jnp.arange(rows), GRID_W)
    col = jnp.tile(jnp.arange(GRID_W), rows)
    quarter = D // 4
    omega = 1.0 / (POS_BASE ** (jnp.arange(quarter, dtype=F32) / quarter))

    def axis_emb(p):
        ang = p.astype(F32)[:, None] * omega[None]
        return jnp.concatenate([jnp.sin(ang), jnp.cos(ang)], axis=-1)

    return jnp.concatenate([axis_emb(row), axis_emb(col)], axis=-1)


def kernel(x_prompt, x_sample, state_delta, c, c_ctx, w_mod, b_mod, norm_mix, norm_ffn, norm_final, dn_w_in, dn_conv, dn_a_log, dn_dt_bias, dn_norm, dn_w_out, hy_w_in, hy_conv, hy_w1, hy_b1, hy_w2, hy_b2, hy_w3, hy_b3, hy_w4, hy_freq, hy_bias, hy_w_out, moe_w_router, moe_b_router, moe_w_in, moe_b_in, moe_w_out, moe_b_out):
    Bp, Tp, D = x_prompt.shape
    Bs, Ts, _ = x_sample.shape
    depth = w_mod.shape[0]
    H = state_delta.shape[3]
    Mp, Ms = Bp * Tp, Bs * Ts
    M = Mp + Ms
    assert Mp % Ts == 0 and Ts % Tp == 0 and Tp % DN_CHUNK == 0, "token groups must tile each other"
    tT = Tp
    tm_big = _tile(math.gcd(Mp, Ts), 1024)
    tm_mid = _tile(math.gcd(Mp, Ts), 512)
    tm_tok = min(256, Tp)

    def rowmap_for(tm):
        tile_start = np.arange(M // tm) * tm
        return jnp.asarray(np.where(tile_start < Mp, 0, 1 + (tile_start - Mp) // Ts), I32)

    rm_big, rm_mid, rm_tok = rowmap_for(tm_big), rowmap_for(tm_mid), rowmap_for(tm_tok)

    xs = x_sample + _grid_pos_embedding(Ts, D)[None]
    x = jnp.concatenate([x_prompt.reshape(Mp, D), xs.reshape(Ms, D)], axis=0)
    conv_start = np.arange(M // tT) * tT
    seq_len = np.where(conv_start < Mp, Tp, Ts)
    seq_pos = np.where(conv_start < Mp, conv_start % Tp, (conv_start - Mp) % Ts)
    has_prev = jnp.asarray(seq_pos > 0, I32)
    has_next = jnp.asarray(seq_pos + tT < seq_len, I32)

    n_cond = 1 + Bs
    r_pad = -(-n_cond // 8) * 8
    cond = jnp.zeros((r_pad, D), F32).at[0].set(c_ctx).at[1:n_cond].set(c)
    mod = _modulation(cond, w_mod, b_mod)

    n_experts = moe_w_in.shape[1]
    w_in_all = moe_w_in.astype(BF16).reshape((depth * n_experts,) + moe_w_in.shape[2:])
    w_out_all = moe_w_out.astype(BF16).reshape((depth * n_experts,) + moe_w_out.shape[2:])
    b_in_all = moe_b_in.reshape(depth * n_experts, -1)
    b_out_all = moe_b_out.reshape(depth * n_experts, -1)

    new_states = []
    i_dn = i_hy = 0
    for l in range(depth):
        mod4 = mod[l].reshape(r_pad, 6, 1, D)
        if l % 2 == 0:
            i = i_dn
            i_dn += 1
            w_in = dn_w_in[i]
            n_main = 4 * H * HEAD_DIM
            w_ba = jnp.zeros((D, LANES), F32).at[:, :4 * H].set(w_in[:, n_main:]).astype(BF16)
            pm, ba = _fused_mm(_norm_mod_prologue, [x], [norm_mix[l]], [0, 1], w_in[:, :n_main].astype(BF16),
                               rm_big, mod4=mod4, side_w=w_ba, tm=tm_big, tn=1024)
            qkv, bg = _dn_act(pm, ba, dn_conv[i], dn_a_log[i], dn_dt_bias[i], has_prev, has_next, tT)
            hb = 2 if H % 2 == 0 else 1
            cols_p, gct_p = _delta_side_inputs(bg, 0, Bp, Tp, H, hb)
            cols_s, gct_s = _delta_side_inputs(bg, Mp, Bs, Ts, H, hb)
            og_p, s_fin = _delta(qkv, pm, cols_p, gct_p, dn_norm[i], None,
                                 n_batch=Bp, batch_off=0, T=Tp, hb=hb)
            og_s, _ = _delta(qkv, pm, cols_s, gct_s, dn_norm[i], state_delta[:, i].astype(F32),
                             n_batch=Bs, batch_off=Mp // Ts, T=Ts, hb=hb)
            new_states.append(s_fin.astype(state_delta.dtype))
            x = _fused_mm(functools.partial(_two_part_prologue, Mp // tm_big), [og_p], [], [],
                          dn_w_out[i].astype(BF16), rm_big, mod4=mod4, a_rest=og_s,
                          res=x, gate_chunk=2, out_dtype=F32, tm=tm_big, tn=1024)
        else:
            i = i_hy
            i_hy += 1
            u = _fused_mm(_norm_mod_prologue, [x], [norm_mix[l]], [0, 1], hy_w_in[i].astype(BF16),
                          rm_big, mod4=mod4, tm=tm_big, tn=1024)
            x0, s = _hy_pre(u, hy_conv[i], has_prev, has_next, tT)
            hy = (hy_w1[i], hy_b1[i], hy_w2[i], hy_b2[i], hy_w3[i], hy_b3[i], hy_w4[i], hy_freq[i])
            conv_p = _hyena_conv(s, Bp, 0, Tp, _hyena_filter_taps(Tp, *hy))
            conv_s = _hyena_conv(s, Bs, Mp, Ts, _hyena_filter_taps(Ts, *hy))
            x = _fused_mm(functools.partial(_hyena_gate_prologue, Mp // tm_mid), [conv_p, x0, s], [hy_bias[i]], [],
                          hy_w_out[i].astype(BF16), rm_mid, mod4=mod4, a_rest=conv_s,
                          res=x, gate_chunk=2, out_dtype=F32, tm=tm_mid, tn=1024)
        last = l == depth - 1
        x = _moe_layer(x, norm_ffn[l], mod4, rm_tok, tm_tok, moe_w_router[l], moe_b_router[l],
                       w_in_all, b_in_all, w_out_all, b_out_all, l * n_experts,
                       final_g=norm_final if last else None, split_rows=Mp)

    y_prompt, y_sample = x
    return y_prompt.reshape(Bp, Tp, D), y_sample.reshape(Bs, Ts, D), jnp.stack(new_states, axis=1)
```

```python
import functools
import math

import numpy as np
import jax
import jax.numpy as jnp
from jax import lax
from jax.experimental import pallas as pl
from jax.experimental.pallas import tpu as pltpu

F32, BF16, I32 = jnp.float32, jnp.bfloat16, jnp.int32

V7X_VMEM_LIMIT_BYTES = 56 * 2**20
LANES = 128
BF16_ROWS = 16

GRID_W = 64
DN_CHUNK = 128
HEAD_DIM = 128
TOP_K = 4
RMS_EPS = 1e-6
L2_EPS = 1e-6
POS_BASE = 10000.0
SWIGLU_LIMIT = 7.0
SWIGLU_ALPHA = 1.702
HY_TARGET = 1e-2
HY_FAST = 0.3
HY_SLOW = 1.5

LOG2_LANES = LANES.bit_length() - 1
LOG2_TOP_K = TOP_K.bit_length() - 1
assert 1 << LOG2_LANES == LANES and 1 << LOG2_TOP_K == TOP_K

HIGHEST = lax.Precision.HIGHEST
NT_DIMS = (((1,), (1,)), ((), ()))
TN_DIMS = (((0,), (0,)), ((), ()))


def _params(*sem):
    return pltpu.CompilerParams(dimension_semantics=sem, vmem_limit_bytes=V7X_VMEM_LIMIT_BYTES)


def _tile(n, pref):
    if n <= pref:
        return n
    t = pref - pref % LANES
    while n % t:
        t -= LANES
    return t


def _silu(x):
    return x * jax.nn.sigmoid(x)


def _softplus(x):
    return jnp.maximum(x, 0.0) + jnp.log1p(jnp.exp(-jnp.abs(x)))


def _split_bf16(x):
    hi = x.astype(BF16)
    lo = (x - hi.astype(F32)).astype(BF16)
    return hi, lo


def _dot3(a, b):
    ah, al = _split_bf16(a)
    bh, bl = _split_bf16(b)
    return (jnp.dot(ah, bh, preferred_element_type=F32)
            + (jnp.dot(ah, bl, preferred_element_type=F32) + jnp.dot(al, bh, preferred_element_type=F32)))


def _modulation_kernel(c_ref, w_ref, b_ref, o_ref):
    s = _silu(c_ref[...]).astype(BF16)
    o_ref[0] = jnp.dot(s, w_ref[0].astype(BF16), preferred_element_type=F32) + b_ref[0]


def _modulation(cond, w_mod, b_mod):
    R, D = cond.shape
    L, _, N = w_mod.shape
    tn = _tile(N, 1024)
    return pl.pallas_call(
        _modulation_kernel,
        grid=(L, N // tn),
        in_specs=[pl.BlockSpec((R, D), lambda l, j: (0, 0)),
                  pl.BlockSpec((1, D, tn), lambda l, j: (l, 0, j)),
                  pl.BlockSpec((1, 1, tn), lambda l, j: (l, 0, j))],
        out_specs=pl.BlockSpec((1, R, tn), lambda l, j: (l, 0, j)),
        out_shape=jax.ShapeDtypeStruct((L, R, N), F32),
        compiler_params=_params("parallel", "parallel"),
        name="modulation",
    )(cond, w_mod, b_mod.reshape(L, 1, N))


def _fused_mm_kernel(prologue, n_a, n_v, n_m, has_res, has_side, rm_ref, *refs):
    del rm_ref
    a_refs, refs = refs[:n_a], refs[n_a:]
    v_refs, refs = refs[:n_v], refs[n_v:]
    m_refs, refs = refs[:n_m], refs[n_m:]
    w_ref, refs = refs[0], refs[1:]
    if has_res:
        res_ref, gate_ref, refs = refs[0], refs[1], refs[2:]
    if has_side:
        sw_ref, refs = refs[0], refs[1:]
    o_ref, refs = refs[0], refs[1:]
    if has_side:
        so_ref, refs = refs[0], refs[1:]
    a_scr = refs[0]

    row_tile = pl.program_id(0)

    @pl.when(pl.program_id(1) == 0)
    def _():
        a = prologue([r[...] for r in a_refs], [r[...] for r in v_refs], [r[0, 0] for r in m_refs], row_tile)
        a_scr[...] = a.astype(BF16)
        if has_side:
            so_ref[...] = jnp.dot(a_scr[...], sw_ref[...], preferred_element_type=F32)

    acc = jnp.dot(a_scr[...], w_ref[...], preferred_element_type=F32)
    if has_res:
        acc = res_ref[...] + gate_ref[0, 0] * acc
    o_ref[...] = acc.astype(o_ref.dtype)


def _fused_mm(prologue, a_ins, vec_ins, mod_ins, w, rowmap, *, mod4=None, res=None, gate_chunk=None,
              side_w=None, a_rest=None, out_dtype=BF16, tm, tn):
    a_split = None if a_rest is None else a_ins[0].shape[0]
    K = a_ins[0].shape[1]
    M = rowmap.shape[0] * tm
    N = w.shape[1]
    tn = _tile(N, tn)
    in_specs, args = [], []
    for a in a_ins:
        in_specs.append(pl.BlockSpec((tm, K), lambda i, j, rm: (i, 0)))
        args.append(a)
    if a_split is not None:
        n_first = a_split // tm
        in_specs[0] = pl.BlockSpec((tm, K), lambda i, j, rm: (jnp.minimum(i, n_first - 1), 0))
        in_specs.insert(1, pl.BlockSpec((tm, K), lambda i, j, rm: (jnp.maximum(i - n_first, 0), 0)))
        args.insert(1, a_rest)
    for v in vec_ins:
        in_specs.append(pl.BlockSpec((1, K), lambda i, j, rm: (0, 0)))
        args.append(v.reshape(1, K))
    for c in mod_ins:
        in_specs.append(pl.BlockSpec((1, 1, 1, K), lambda i, j, rm, c=c: (rm[i], c, 0, 0)))
        args.append(mod4)
    in_specs.append(pl.BlockSpec((K, tn), lambda i, j, rm: (0, j)))
    args.append(w)
    if res is not None:
        in_specs.append(pl.BlockSpec((tm, tn), lambda i, j, rm: (i, j)))
        args.append(res)
        in_specs.append(pl.BlockSpec((1, 1, 1, tn), lambda i, j, rm, c=gate_chunk: (rm[i], c, 0, j)))
        args.append(mod4)
    out_specs = [pl.BlockSpec((tm, tn), lambda i, j, rm: (i, j))]
    out_shape = [jax.ShapeDtypeStruct((M, N), out_dtype)]
    if side_w is not None:
        ns = side_w.shape[1]
        in_specs.append(pl.BlockSpec((K, ns), lambda i, j, rm: (0, 0)))
        args.append(side_w)
        out_specs.append(pl.BlockSpec((tm, ns), lambda i, j, rm: (i, 0)))
        out_shape.append(jax.ShapeDtypeStruct((M, ns), F32))
    kern = functools.partial(_fused_mm_kernel, prologue, len(a_ins) + (a_rest is not None), len(vec_ins),
                             len(mod_ins), res is not None, side_w is not None)
    outs = pl.pallas_call(
        kern,
        grid_spec=pltpu.PrefetchScalarGridSpec(
            num_scalar_prefetch=1, grid=(M // tm, N // tn), in_specs=in_specs, out_specs=out_specs,
            scratch_shapes=[pltpu.VMEM((tm, K), BF16)]),
        out_shape=out_shape,
        compiler_params=_params("parallel", "arbitrary"),
        name="fused_mm",
    )(rowmap, *args)
    return outs if side_w is not None else outs[0]


def _norm_mod_prologue(a, v, m, row_tile=None):
    x, g, (shift, scale) = a[0], v[0], m
    y = x * lax.rsqrt(jnp.mean(x * x, axis=-1, keepdims=True) + RMS_EPS) * g
    return y * (1.0 + scale) + shift


def _two_part_prologue(n_first_tiles, a, v, m, row_tile):
    return jnp.where(row_tile < n_first_tiles, a[0], a[1])


def _hyena_gate_prologue(n_first_tiles, a, v, m, row_tile):
    conv = jnp.where(row_tile < n_first_tiles, a[0], a[1]).astype(F32)
    x0, s = a[2].astype(F32), a[3].astype(F32)
    return x0 * (conv + s * v[0])


def _conv3(x, prev_row, next_row, w, rid):
    t = x.shape[0]
    xm = jnp.where(rid == 0, prev_row, pltpu.roll(x, 1, 0))
    xp = jnp.where(rid == t - 1, next_row, pltpu.roll(x, t - 1, 0))
    return xm * w[0:1] + x * w[1:2] + xp * w[2:3]


def _halo_specs(tT, width, col_block):
    hb = tT // BF16_ROWS
    return [
        pl.BlockSpec((tT, width), lambda i, hp, hn: (i, col_block)),
        pl.BlockSpec((BF16_ROWS, width), lambda i, hp, hn: (jnp.maximum(i * hb - 1, 0), col_block)),
        pl.BlockSpec((BF16_ROWS, width), lambda i, hp, hn: ((i + 1) * hb * hn[i], col_block)),
    ]


def _dn_act_kernel(n_heads, hp_ref, hn_ref, x_ref, xp_ref, xn_ref, ba_ref, cw_ref, al_ref, dtb_ref,
                   qkv_ref, bg_ref):
    i = pl.program_id(0)
    hp = hp_ref[i].astype(F32)
    hn = hn_ref[i].astype(F32)
    tT = x_ref.shape[0]
    rid = lax.broadcasted_iota(I32, (tT, LANES), 0)
    for cb in range(3 * n_heads):
        sl = slice(cb * LANES, (cb + 1) * LANES)
        x = x_ref[:, sl].astype(F32)
        pr = xp_ref[:, sl].astype(F32)[BF16_ROWS - 1:BF16_ROWS] * hp
        nx = xn_ref[:, sl].astype(F32)[0:1] * hn
        y = _silu(_conv3(x, pr, nx, cw_ref[:, sl], rid))
        if cb < 2 * n_heads:
            y = y * lax.rsqrt(jnp.sum(y * y, axis=-1, keepdims=True) + L2_EPS)
            if cb < n_heads:
                y = y * (HEAD_DIM ** -0.5)
        qkv_ref[:, sl] = y.astype(BF16)

    ba = ba_ref[...]
    lane = lax.broadcasted_iota(I32, (tT, LANES), 1)
    beta = jax.nn.sigmoid(ba)
    g = -jnp.exp(al_ref[...]) * _softplus(ba + dtb_ref[...])
    r = lax.broadcasted_iota(I32, (tT, tT), 0)
    c = lax.broadcasted_iota(I32, (tT, tT), 1)
    shift = DN_CHUNK.bit_length() - 1
    same = jnp.right_shift(r, shift) == jnp.right_shift(c, shift)
    l_pre = jnp.where(same & (c <= r), 1.0, 0.0).astype(F32)
    l_suf = jnp.where(same & (c >= r), 1.0, 0.0).astype(F32)
    g_pre = jnp.dot(l_pre, g, precision=HIGHEST, preferred_element_type=F32)
    g_suf = jnp.dot(l_suf, g, precision=HIGHEST, preferred_element_type=F32)
    gc = jnp.where(lane < 3 * n_heads, g_pre, g_suf)
    bg_ref[...] = jnp.where(lane < 2 * n_heads, beta, gc)


def _dn_act(pm, ba, conv_w, a_log, dt_bias, has_prev, has_next, tT):
    M = pm.shape[0]
    cw = conv_w.shape[1]
    n_heads = cw // (3 * HEAD_DIM)
    pad = lambda v: jnp.zeros((1, LANES), F32).at[0, 2 * n_heads:4 * n_heads].set(v.reshape(-1).astype(F32))
    vec = pl.BlockSpec((1, LANES), lambda i, hp, hn: (0, 0))
    return pl.pallas_call(
        functools.partial(_dn_act_kernel, n_heads),
        grid_spec=pltpu.PrefetchScalarGridSpec(
            num_scalar_prefetch=2, grid=(M // tT,),
            in_specs=_halo_specs(tT, cw, 0) + [
                pl.BlockSpec((tT, LANES), lambda i, hp, hn: (i, 0)),
                pl.BlockSpec((3, cw), lambda i, hp, hn: (0, 0)), vec, vec],
            out_specs=[pl.BlockSpec((tT, cw), lambda i, hp, hn: (i, 0)),
                       pl.BlockSpec((tT, LANES), lambda i, hp, hn: (i, 0))]),
        out_shape=[jax.ShapeDtypeStruct((M, cw), BF16), jax.ShapeDtypeStruct((M, LANES), F32)],
        compiler_params=_params("parallel"),
        name="dn_act",
    )(has_prev, has_next, pm, pm, pm, ba, conv_w, pad(a_log), pad(dt_bias))


def _merge_masks(ri, ci, n, lower):
    masks = []
    s = 0
    while (1 << s) < n:
        same = jnp.right_shift(ri, s + 1) == jnp.right_shift(ci, s + 1)
        hi_r = jnp.bitwise_and(jnp.right_shift(ri, s), 1)
        hi_c = jnp.bitwise_and(jnp.right_shift(ci, s), 1)
        off = (hi_r == 1) & (hi_c == 0) if lower else (hi_r == 0) & (hi_c == 1)
        masks.append(same & off)
        s += 1
    return masks


def _dot3_many(xs, ys):
    sx = [_split_bf16(x) for x in xs]
    sy = [_split_bf16(y) for y in ys]
    hh = [jnp.dot(x[0], y[0], preferred_element_type=F32) for x, y in zip(sx, sy)]
    hl = [jnp.dot(x[0], y[1], preferred_element_type=F32) for x, y in zip(sx, sy)]
    lh = [jnp.dot(x[1], y[0], preferred_element_type=F32) for x, y in zip(sx, sy)]
    return [a + (b + c) for a, b, c in zip(hh, hl, lh)]


def _dot1_many(xs, ys):
    return [jnp.dot(x.astype(BF16), y.astype(BF16), preferred_element_type=F32) for x, y in zip(xs, ys)]


def _unit_tri_inverse_many(mats, eye, masks, stricts):
    ts = [eye - jnp.where(m[0], a, 0.0) for a, m in zip(mats, masks)]
    for lvl in range(1, len(masks[0])):
        off = [jnp.where(m[lvl], a, 0.0) for a, m in zip(mats, masks)]
        upd = _dot1_many(_dot1_many(ts, off), ts)
        ts = [t - u for t, u in zip(ts, upd)]
    full = [eye + jnp.where(st, a, 0.0) for a, st in zip(mats, stricts)]
    res = [eye - p for p in _dot3_many(full, ts)]
    return [t + c for t, c in zip(ts, _dot1_many(ts, res))]


def _delta_kernel(hb, n_chunks, group, has_s0, *refs):
    if has_s0:
        q_ref, k_ref, v_ref, z_ref, cols_ref, gct_ref, ng_ref, s0_ref, o_ref, sfin_ref = refs[:10]
        scr = refs[10:]
    else:
        q_ref, k_ref, v_ref, z_ref, cols_ref, gct_ref, ng_ref, o_ref, sfin_ref = refs[:9]
        scr = refs[9:]
    u_scr, wq_scr, qk_scr, s_scr, o_scr = scr
    C = DN_CHUNK
    n = n_chunks
    ri = lax.broadcasted_iota(I32, (C, C), 0)
    ci = lax.broadcasted_iota(I32, (C, C), 1)
    eye = (ri == ci).astype(F32)
    incl = (ri >= ci, ri <= ci)

    def gate_cols(hh, d, rows):
        bcol = cols_ref[0, 0, rows, hh * 4 + d:hh * 4 + d + 1]
        gcol = cols_ref[0, 0, rows, hh * 4 + 2 + d:hh * 4 + 3 + d]
        gl = gcol[C - 1:C, :] if d == 0 else gcol[0:1, :]
        return bcol, gcol, gl

    def prep(it, carry):
        tri_masks = (_merge_masks(ri, ci, C, True), _merge_masks(ri, ci, C, False))
        tri_strict = (ri > ci, ri < ci)
        where_, mats, masks, stricts, rhss = [], [], [], [], []
        for gi in range(group):
            c = it * group + gi
            rows = pl.ds(pl.multiple_of(c * C, C), C)
            for hh in range(hb):
                ls = slice(hh * HEAD_DIM, (hh + 1) * HEAD_DIM)
                qb, kb16, vb = q_ref[0, rows, ls], k_ref[0, rows, ls], v_ref[0, rows, ls]
                qf, kf, vf = qb.astype(F32), kb16.astype(F32), vb.astype(F32)
                qkt = lax.dot_general(qb, kb16, NT_DIMS, preferred_element_type=F32)
                for d in range(2):
                    ch = hh * 2 + d
                    bcol, gcol, _ = gate_cols(hh, d, rows)
                    grow = gct_ref[0, hh, d, pl.ds(c, 1), :]
                    dm = jnp.where(incl[d], jnp.exp(jnp.where(incl[d], gcol - grow, 0.0)), 0.0)
                    kbeta = kf * bcol
                    eg = jnp.exp(gcol)
                    mats.append(lax.dot_general(kbeta.astype(BF16), kb16, NT_DIMS, preferred_element_type=F32) * dm)
                    masks.append(tri_masks[d])
                    stricts.append(tri_strict[d])
                    rhss.append(jnp.concatenate([vf * bcol, kbeta * eg], axis=1))
                    where_.append((ch, c, rows))
                    wq_scr[ch, pl.ds(pl.multiple_of(c * 2 * C + C, C), C), :] = (qf * eg).astype(BF16)
                    qk_scr[ch, rows, :] = (qkt * dm).astype(BF16)
        sols = _dot3_many(_unit_tri_inverse_many(mats, eye, masks, stricts), rhss)
        for (ch, c, rows), sol in zip(where_, sols):
            u_scr[ch, rows, :] = sol[:, :HEAD_DIM]
            wq_scr[ch, pl.ds(pl.multiple_of(c * 2 * C, 2 * C), C), :] = sol[:, HEAD_DIM:].astype(BF16)
        return carry

    lax.fori_loop(0, n // group, prep, 0)

    for hh in range(hb):
        for d in range(2):
            s_scr[hh * 2 + d] = s0_ref[0, d, hh] if has_s0 else jnp.zeros((HEAD_DIM, HEAD_DIM), F32)

    T = n * C
    tr = min(T, 256)

    def clear(b, carry):
        o_scr[:, pl.ds(pl.multiple_of(b * tr, tr), tr), :] = jnp.zeros((hb, tr, HEAD_DIM), F32)
        return carry

    lax.fori_loop(0, T // tr, clear, 0)

    def scan(it, carry):
        chains = [(hh, d) for hh in range(hb) for d in range(2)]
        cs = [it if d == 0 else n - 1 - it for _, d in chains]
        rows = [pl.ds(pl.multiple_of(c * C, C), C) for c in cs]
        ss = [s_scr[hh * 2 + d] for hh, d in chains]
        rs = [jnp.dot(wq_scr[hh * 2 + d, pl.ds(pl.multiple_of(c * 2 * C, 2 * C), 2 * C), :], s.astype(BF16),
                      preferred_element_type=F32) for (hh, d), c, s in zip(chains, cs, ss)]
        vns = [(u_scr[hh * 2 + d, rw, :] - r[:C]).astype(BF16) for (hh, d), rw, r in zip(chains, rows, rs)]
        for i, (hh, d) in enumerate(chains):
            _, gcol, gl = gate_cols(hh, d, rows[i])
            kt = (k_ref[0, rows[i], hh * HEAD_DIM:(hh + 1) * HEAD_DIM].astype(F32) * jnp.exp(gl - gcol)).astype(BF16)
            s_scr[hh * 2 + d] = (ss[i] * jnp.exp(gl)
                                 + lax.dot_general(kt, vns[i], TN_DIMS, preferred_element_type=F32))
        for i, (hh, d) in enumerate(chains):
            o = rs[i][C:] + jnp.dot(qk_scr[hh * 2 + d, rows[i], :], vns[i], preferred_element_type=F32)
            o_scr[hh, rows[i], :] += o
        return carry

    lax.fori_loop(0, n, scan, 0)

    for hh in range(hb):
        for d in range(2):
            sfin_ref[0, d, hh] = s_scr[hh * 2 + d]

    def gate(b, carry):
        rows = pl.ds(pl.multiple_of(b * tr, tr), tr)
        for hh in range(hb):
            ls = slice(hh * HEAD_DIM, (hh + 1) * HEAD_DIM)
            o = o_scr[hh, rows, :]
            o = o * lax.rsqrt(jnp.mean(o * o, axis=-1, keepdims=True) + RMS_EPS) * ng_ref[...]
            o_ref[0, rows, ls] = (o * _silu(z_ref[0, rows, ls].astype(F32))).astype(BF16)
        return carry

    lax.fori_loop(0, T // tr, gate, 0)


def _delta(qkv, pm, cols, gct, norm_g, s0, *, n_batch, batch_off, T, hb):
    M, cw = qkv.shape
    H = cw // (3 * HEAD_DIM)
    n = T // DN_CHUNK
    W = hb * HEAD_DIM
    nb = H // hb
    qkv3 = qkv.reshape(M // T, T, cw)
    pm3 = pm.reshape(M // T, T, pm.shape[1])
    once = pl.Buffered(1)
    blk = lambda off: pl.BlockSpec((1, T, W), lambda b, j, off=off: (b + batch_off, 0, off * nb + j),
                                   pipeline_mode=once)
    in_specs = [blk(0), blk(1), blk(2), blk(3),
                pl.BlockSpec((1, 1, T, 4 * hb), lambda b, j: (b, j, 0, 0), pipeline_mode=once),
                pl.BlockSpec((1, hb, 2, n, DN_CHUNK), lambda b, j: (b, j, 0, 0, 0)),
                pl.BlockSpec((1, HEAD_DIM), lambda b, j: (0, 0))]
    args = [qkv3, qkv3, qkv3, pm3, cols, gct, norm_g.reshape(1, HEAD_DIM)]
    if s0 is not None:
        in_specs.append(pl.BlockSpec((1, 2, hb, HEAD_DIM, HEAD_DIM), lambda b, j: (b, 0, j, 0, 0)))
        args.append(s0)
    nch = 2 * hb
    group = math.gcd(n, max(1, 16 // nch))
    out, sfin = pl.pallas_call(
        functools.partial(_delta_kernel, hb, n, group, s0 is not None),
        grid=(n_batch, nb),
        in_specs=in_specs,
        out_specs=[pl.BlockSpec((1, T, W), lambda b, j: (b, 0, j)),
                   pl.BlockSpec((1, 2, hb, HEAD_DIM, HEAD_DIM), lambda b, j: (b, 0, j, 0, 0))],
        out_shape=[jax.ShapeDtypeStruct((n_batch, T, H * HEAD_DIM), BF16),
                   jax.ShapeDtypeStruct((n_batch, 2, H, HEAD_DIM, HEAD_DIM), F32)],
        scratch_shapes=[pltpu.VMEM((nch, T, HEAD_DIM), F32),
                        pltpu.VMEM((nch, 2 * T, HEAD_DIM), BF16),
                        pltpu.VMEM((nch, T, DN_CHUNK), BF16),
                        pltpu.VMEM((nch, HEAD_DIM, HEAD_DIM), F32),
                        pltpu.VMEM((hb, T, HEAD_DIM), F32)],
        compiler_params=_params("parallel", "parallel"),
        name="delta_rule",
    )(*args)
    return out.reshape(n_batch * T, H * HEAD_DIM), sfin


def _delta_side_inputs(bg, t0, n_batch, T, H, hb):
    b5 = bg[t0:t0 + n_batch * T, :4 * H].reshape(n_batch, T, 2, 2, H)
    cols = jnp.transpose(b5, (0, 4, 1, 2, 3)).reshape(n_batch, H // hb, hb, T, 4)
    cols = jnp.transpose(cols, (0, 1, 3, 2, 4)).reshape(n_batch, H // hb, T, 4 * hb)
    gct = jnp.transpose(b5[:, :, 1], (0, 3, 2, 1)).reshape(n_batch, H, 2, T // DN_CHUNK, DN_CHUNK)
    return cols, gct


def _hy_pre_kernel(D, hp_ref, hn_ref, x_ref, xp_ref, xn_ref, cw_ref, x0_ref, s_ref):
    i = pl.program_id(0)
    hp = hp_ref[i].astype(F32)
    hn = hn_ref[i].astype(F32)
    tT = x_ref.shape[0]
    W = min(D, 2 * LANES)
    rid = lax.broadcasted_iota(I32, (tT, W), 0)

    def conv(cb, part):
        sl = slice(part * D + cb * W, part * D + (cb + 1) * W)
        x = x_ref[:, sl].astype(F32)
        pr = xp_ref[:, sl].astype(F32)[BF16_ROWS - 1:BF16_ROWS] * hp
        nx = xn_ref[:, sl].astype(F32)[0:1] * hn
        return _conv3(x, pr, nx, cw_ref[:, sl], rid)

    for cb in range(D // W):
        sl = slice(cb * W, (cb + 1) * W)
        x0_ref[:, sl] = conv(cb, 0).astype(BF16)
        s_ref[:, sl] = (conv(cb, 1) * conv(cb, 2)).astype(BF16)


def _hy_pre(u, conv_w, has_prev, has_next, tT):
    M, W3 = u.shape
    D = W3 // 3
    out = pl.BlockSpec((tT, D), lambda i, hp, hn: (i, 0))
    return pl.pallas_call(
        functools.partial(_hy_pre_kernel, D),
        grid_spec=pltpu.PrefetchScalarGridSpec(
            num_scalar_prefetch=2, grid=(M // tT,),
            in_specs=_halo_specs(tT, W3, 0) + [pl.BlockSpec((3, W3), lambda i, hp, hn: (0, 0))],
            out_specs=[out, out]),
        out_shape=[jax.ShapeDtypeStruct((M, D), BF16)] * 2,
        compiler_params=_params("parallel"),
        name="hy_pre",
    )(has_prev, has_next, u, u, u, conv_w)


def _bmm_kernel(a_ref, b_ref, o_ref, acc_ref):
    k = pl.program_id(3)
    part = jnp.dot(a_ref[...], b_ref[0], preferred_element_type=F32)

    @pl.when(k == 0)
    def _():
        acc_ref[...] = part

    @pl.when(k > 0)
    def _():
        acc_ref[...] += part

    @pl.when(k == pl.num_programs(3) - 1)
    def _():
        o_ref[0] = acc_ref[...].astype(o_ref.dtype)


def _bmm(a, b, b_batch_off=0, n_batch=None, out_dtype=BF16, tm=1024, tn=1024, tk=2048):
    M, K = a.shape
    N = b.shape[2]
    nb = b.shape[0] if n_batch is None else n_batch
    tm, tn, tk = _tile(M, tm), _tile(N, tn), _tile(K, tk)
    return pl.pallas_call(
        _bmm_kernel,
        grid=(nb, M // tm, N // tn, K // tk),
        in_specs=[pl.BlockSpec((tm, tk), lambda i, m, n, k: (m, k)),
                  pl.BlockSpec((1, tk, tn), lambda i, m, n, k: (i + b_batch_off, k, n))],
        out_specs=pl.BlockSpec((1, tm, tn), lambda i, m, n, k: (i, m, n)),
        out_shape=jax.ShapeDtypeStruct((nb, M, N), out_dtype),
        scratch_shapes=[pltpu.VMEM((tm, tn), F32)],
        compiler_params=_params("parallel", "parallel", "parallel", "arbitrary"),
        name="dft_mm",
    )(a, b)


def _spec_prod_kernel(inv_n, s_ref, k_ref, y_ref):
    sc, ss = s_ref[0, 0].astype(F32), s_ref[0, 1].astype(F32)
    kc, ks = k_ref[0], k_ref[1]
    first = (lax.broadcasted_iota(I32, sc.shape, 0) == 0) & (pl.program_id(1) == 0)
    y_ref[0, 0] = (jnp.where(first, sc * kc, 2.0 * (sc * kc - ss * ks)) * inv_n).astype(y_ref.dtype)
    y_ref[0, 1] = (jnp.where(first, ss * ks, 2.0 * (sc * ks + ss * kc)) * inv_n).astype(y_ref.dtype)


def _spec_prod(sf, kf):
    B, _, T, D = sf.shape
    tr, tc = _tile(T, 256), _tile(D, 1024)
    return pl.pallas_call(
        functools.partial(_spec_prod_kernel, 1.0 / (2 * T)),
        grid=(B, T // tr, D // tc),
        in_specs=[pl.BlockSpec((1, 2, tr, tc), lambda b, i, j: (b, 0, i, j)),
                  pl.BlockSpec((2, tr, tc), lambda b, i, j: (0, i, j))],
        out_specs=pl.BlockSpec((1, 2, tr, tc), lambda b, i, j: (b, 0, i, j)),
        out_shape=jax.ShapeDtypeStruct(sf.shape, BF16),
        compiler_params=_params("parallel", "parallel", "parallel"),
        name="spec_prod",
    )(sf, kf)


def _dft_matrix(T):
    k = jnp.arange(T, dtype=I32)[:, None]
    t = jnp.arange(T, dtype=I32)[None, :]
    step = math.gcd(T, 64)
    th = jnp.arange(T // step, dtype=I32)[None, :] * step
    tl = jnp.arange(step, dtype=I32)[None, :]
    ang_h = ((k * th) % (2 * T)).astype(F32) * (math.pi / T)
    ang_l = ((k * tl) % (2 * T)).astype(F32) * (math.pi / T)
    ch, sh = jnp.cos(ang_h)[:, :, None], jnp.sin(ang_h)[:, :, None]
    cl, sl = jnp.cos(ang_l)[:, None, :], jnp.sin(ang_l)[:, None, :]
    cos = (ch * cl - sh * sl).reshape(T, T)
    sin = (sh * cl + ch * sl).reshape(T, T)
    nyq = jnp.where(t % 2 == 0, 1.0, -1.0).astype(F32)
    sin = jnp.where(k == 0, nyq, sin)
    return jnp.concatenate([cos, sin], axis=0).astype(BF16)


def _hyena_filter_taps(L, w1, b1, w2, b2, w3, b3, w4, freq):
    D = w4.shape[1] // 2
    n_bands = (w1.shape[0] - 1) // 2
    pos = jnp.arange(L, dtype=F32)
    t = pos / max(L - 1, 1)
    bands = jnp.linspace(1e-4, n_bands - 1, n_bands, dtype=F32)
    ang = (2.0 * math.pi / L) * pos[:, None] * bands[None]
    feats = jnp.concatenate([t[:, None], jnp.cos(ang), -jnp.sin(ang)], axis=-1)
    zf = jnp.sin(freq[0] * (feats @ w1 + b1))
    zf = jnp.sin(freq[1] * (zf @ w2 + b2))
    zf = jnp.sin(freq[2] * (zf @ w3 + b3))
    filt = (zf @ w4).reshape(L, 2, D)
    deltas = jnp.abs(jnp.linspace(math.log(HY_TARGET) / HY_SLOW, math.log(HY_TARGET) / HY_FAST, D, dtype=F32))
    filt = filt * jnp.exp(-t[:, None, None] * deltas[None, None])
    f = filt[:, 0]
    b = filt[:, 1].at[0].set(0.0)
    scale = lax.rsqrt(jnp.sum(f * f, axis=0) + jnp.sum(b * b, axis=0) + 1e-6)
    return f * scale, b * scale


def _hyena_conv(s, n_batch, t0, T, taps):
    M, D = s.shape
    f, b = taps
    fwd = _dft_matrix(T)
    kf = _bmm(fwd, jnp.concatenate([f, b], axis=1).astype(BF16)[None], out_dtype=F32)[0]
    p, q = kf[:, :D].reshape(2, T, D), kf[:, D:].reshape(2, T, D)
    first = (jnp.arange(T) == 0)[:, None]
    kspec = jnp.stack([p[0] + q[0], jnp.where(first, p[1] + q[1], p[1] - q[1])])
    s3 = s.reshape(M // T, T, D)
    sf = _bmm(fwd, s3, b_batch_off=t0 // T, n_batch=n_batch)
    y = _spec_prod(sf.reshape(n_batch, 2, T, D), kspec).reshape(n_batch, 2 * T, D)
    conv = _bmm(fwd.T, y)
    return conv.reshape(n_batch * T, D)


def _store_token_major(ref, x):
    t, d = x.shape
    nb = d // LANES
    for j in range(nb):
        ref[pl.ds(j, t, stride=nb), :] = x[:, j * LANES:(j + 1) * LANES]


def _load_token_major(ref, t, j, lead=()):
    nb = ref.shape[-2] // t
    return ref[lead + (pl.ds(j, t, stride=nb), slice(None))]


def _router_kernel(n_experts, rm_ref, x_ref, g_ref, sh_ref, sc_ref, wr_ref, br_ref,
                   h_ref, ti_ref, tg_ref, rk_ref, cnt_ref, base_scr):
    del rm_ref

    @pl.when(pl.program_id(0) == 0)
    def _():
        base_scr[...] = jnp.zeros(base_scr.shape, F32)

    h = _norm_mod_prologue([x_ref[...]], [g_ref[...]], (sh_ref[0, 0], sc_ref[0, 0]))
    _store_token_major(h_ref, h)
    logits = jnp.dot(h, wr_ref[...], precision=HIGHEST, preferred_element_type=F32) + br_ref[...]
    tm = logits.shape[0]
    lane = lax.broadcasted_iota(I32, logits.shape, 1)
    lane_f = lane.astype(F32)
    neg = jnp.float32(-jnp.inf)
    l = jnp.where(lane < n_experts, logits, neg)
    ti = jnp.zeros(logits.shape, I32)
    tl = jnp.full(logits.shape, neg, F32)
    onehot = jnp.zeros(logits.shape, F32)
    picks = []
    for r in range(TOP_K):
        m = jnp.max(l, axis=-1, keepdims=True)
        idx = jnp.min(jnp.where(l == m, lane_f, float(LANES)), axis=-1, keepdims=True).astype(I32)
        ti = jnp.where(lane == r, idx, ti)
        tl = jnp.where(lane == r, m, tl)
        l = jnp.where(lane == idx, neg, l)
        onehot = jnp.where(lane == idx, 1.0, onehot)
        picks.append(idx)
    e = jnp.exp(tl - jnp.max(tl, axis=-1, keepdims=True))
    ti_ref[...] = ti
    tg_ref[...] = e / jnp.sum(e, axis=-1, keepdims=True)

    r_i = lax.broadcasted_iota(I32, (tm, tm), 0)
    c_i = lax.broadcasted_iota(I32, (tm, tm), 1)
    earlier = jnp.where(c_i < r_i, 1.0, 0.0).astype(BF16)
    before = jnp.dot(earlier, onehot.astype(BF16), preferred_element_type=F32) + base_scr[...]
    rk = jnp.zeros(logits.shape, F32)
    for r in range(TOP_K):
        mine = jnp.sum(jnp.where(lane == picks[r], before, 0.0), axis=-1, keepdims=True)
        rk = jnp.where(lane == r, mine, rk)
    rk_ref[...] = rk.astype(I32)
    base_scr[...] += jnp.sum(onehot, axis=0, keepdims=True)
    cnt_ref[...] = base_scr[...].astype(I32)


def _router(x, g, mod4, rowmap, w_router, b_router, tm):
    M, D = x.shape
    E = w_router.shape[1]
    wr = jnp.zeros((D, LANES), F32).at[:, :E].set(w_router)
    br = jnp.zeros((1, LANES), F32).at[0, :E].set(b_router)
    row = lambda c: pl.BlockSpec((1, 1, 1, D), lambda i, rm, c=c: (rm[i], c, 0, 0))
    tile = lambda w: pl.BlockSpec((tm, w), lambda i, rm: (i, 0))
    return pl.pallas_call(
        functools.partial(_router_kernel, E),
        grid_spec=pltpu.PrefetchScalarGridSpec(
            num_scalar_prefetch=1, grid=(M // tm,),
            in_specs=[tile(D), pl.BlockSpec((1, D), lambda i, rm: (0, 0)), row(3), row(4),
                      pl.BlockSpec((D, LANES), lambda i, rm: (0, 0)),
                      pl.BlockSpec((1, LANES), lambda i, rm: (0, 0))],
            out_specs=[pl.BlockSpec((tm * D // LANES, LANES), lambda i, rm: (i, 0)),
                       tile(LANES), tile(LANES), tile(LANES),
                       pl.BlockSpec((1, LANES), lambda i, rm: (0, 0))],
            scratch_shapes=[pltpu.VMEM((1, LANES), F32)]),
        out_shape=[jax.ShapeDtypeStruct((M * D // LANES, LANES), F32), jax.ShapeDtypeStruct((M, LANES), I32),
                   jax.ShapeDtypeStruct((M, LANES), F32), jax.ShapeDtypeStruct((M, LANES), I32),
                   jax.ShapeDtypeStruct((1, LANES), I32)],
        compiler_params=_params("arbitrary"),
        name="router",
    )(rowmap, x, g.reshape(1, D), mod4, mod4, wr, br)


def _dest_kernel(ps_ref, ti_ref, rk_ref, o_ref):
    ti = ti_ref[...]
    start = jnp.zeros(ti.shape, I32)
    for e in range(ps_ref.shape[0]):
        start = jnp.where(ti == e, ps_ref[e], start)
    o_ref[...] = start + rk_ref[...]


def _dest_rows(seg_start, ti, rk, tm):
    M = ti.shape[0]
    tile = pl.BlockSpec((tm, LANES), lambda i, ps: (i, 0))
    return pl.pallas_call(
        _dest_kernel,
        grid_spec=pltpu.PrefetchScalarGridSpec(num_scalar_prefetch=1, grid=(M // tm,),
                                               in_specs=[tile, tile], out_specs=tile),
        out_shape=jax.ShapeDtypeStruct((M, LANES), I32),
        compiler_params=_params("parallel"),
        name="moe_dest",
    )(seg_start, ti, rk)


def _load_tile_dest(dest_ref, dest_smem, sem):
    cp = pltpu.make_async_copy(dest_ref, dest_smem, sem)
    cp.start()
    cp.wait()


def _zero_pad_rows(nb, pad_start_ref, pad_len_ref, o_ref, zero_ref, sem, start):
    def copy(pos, p):
        cp = pltpu.make_async_copy(zero_ref.at[pl.ds(0, p * nb)],
                                   o_ref.at[pl.ds(pl.multiple_of(pos * nb, nb), p * nb)], sem)
        if start:
            cp.start()
        else:
            cp.wait()

    def per_expert(e, carry):
        pos, length = pad_start_ref[e], pad_len_ref[e]
        p = zero_ref.shape[0] // nb // 2
        while p >= 1:
            hit = jnp.bitwise_and(length, p) != 0

            @pl.when(hit)
            def _(pos=pos, p=p):
                copy(pos, p)

            pos = pos + jnp.where(hit, p, 0)
            p //= 2
        return carry

    lax.fori_loop(0, pad_start_ref.shape[0], per_expert, 0)


def _dispatch_kernel(nb, pad_start_ref, pad_len_ref, tail_ref, dest_ref, h_ref, o_ref, dest_smem, zero_ref, sem):
    tm = h_ref.shape[0] // nb
    tz = zero_ref.shape[0]

    @pl.when(pl.program_id(0) == 0)
    def _():
        zero_ref[...] = jnp.zeros(zero_ref.shape, F32)

        def tail_copy(t):
            return pltpu.make_async_copy(zero_ref, o_ref.at[pl.ds(pl.multiple_of(t * tz, tz), tz)], sem.at[2])

        def tail_start(t, carry):
            tail_copy(t).start()
            return carry

        def tail_wait(t, carry):
            tail_copy(t).wait()
            return carry

        _zero_pad_rows(nb, pad_start_ref, pad_len_ref, o_ref, zero_ref, sem.at[2], True)
        lax.fori_loop(tail_ref[0], tail_ref[1], tail_start, 0)
        _zero_pad_rows(nb, pad_start_ref, pad_len_ref, o_ref, zero_ref, sem.at[2], False)
        lax.fori_loop(tail_ref[0], tail_ref[1], tail_wait, 0)

    _load_tile_dest(dest_ref, dest_smem, sem.at[0])

    def row_copy(j):
        dst = dest_smem[jnp.right_shift(j, LOG2_LANES), jnp.bitwise_and(j, LANES - 1)]
        return pltpu.make_async_copy(h_ref.at[pl.ds(pl.multiple_of(jnp.right_shift(j, LOG2_TOP_K) * nb, nb), nb)],
                                     o_ref.at[pl.ds(pl.multiple_of(dst * nb, nb), nb)], sem.at[1])

    def start(j, carry):
        row_copy(j).start()
        return carry

    def wait(j, carry):
        row_copy(j).wait()
        return carry

    lax.fori_loop(0, tm * TOP_K, start, 0, unroll=8)
    lax.fori_loop(0, tm * TOP_K, wait, 0, unroll=8)


def _dispatch(h, nb, dest2d, pad_start, pad_len, tail, n_rows, tm, tm_rows):
    M = h.shape[0] // nb
    nd = tm * TOP_K // LANES
    return pl.pallas_call(
        functools.partial(_dispatch_kernel, nb),
        grid_spec=pltpu.PrefetchScalarGridSpec(
            num_scalar_prefetch=3, grid=(M // tm,),
            in_specs=[pl.BlockSpec((nd, LANES), lambda i, ps, pn, tl: (i, 0)),
                      pl.BlockSpec((tm * nb, LANES), lambda i, ps, pn, tl: (i, 0))],
            out_specs=pl.BlockSpec(memory_space=pl.ANY),
            scratch_shapes=[pltpu.SMEM((nd, LANES), I32), pltpu.VMEM((tm_rows * nb, LANES), F32),
                            pltpu.SemaphoreType.DMA((3,))]),
        out_shape=jax.ShapeDtypeStruct((n_rows * nb, LANES), F32),
        compiler_params=_params("arbitrary"),
        name="moe_dispatch",
    )(pad_start, pad_len, tail, dest2d, h)


def _combine_kernel(final, rm_ref, dest_ref, y_ref, tg_ref, x_ref, gate_ref, ng_ref, *refs):
    del rm_ref
    n_out = 1 if final is None else 2
    o_ref = refs[0] if final is None else refs[:2]
    buf, dest_smem, sem = refs[n_out:]
    tm = x_ref.shape[0]
    _load_tile_dest(dest_ref, dest_smem, sem.at[0])

    nb = x_ref.shape[1] // LANES

    def row_copy(j):
        src = dest_smem[jnp.right_shift(j, LOG2_LANES), jnp.bitwise_and(j, LANES - 1)]
        return pltpu.make_async_copy(
            y_ref.at[pl.ds(pl.multiple_of(src * nb, nb), nb)],
            buf.at[jnp.bitwise_and(j, TOP_K - 1), pl.ds(pl.multiple_of(jnp.right_shift(j, LOG2_TOP_K) * nb, nb), nb)],
            sem.at[1])

    def start(j, carry):
        row_copy(j).start()
        return carry

    def wait(j, carry):
        row_copy(j).wait()
        return carry

    lax.fori_loop(0, tm * TOP_K, start, 0, unroll=8)
    lax.fori_loop(0, tm * TOP_K, wait, 0, unroll=8)
    tg = tg_ref[...]
    blocks = []
    for j in range(nb):
        acc = tg[:, 0:1] * _load_token_major(buf, tm, j, (0,))
        for k in range(1, TOP_K):
            acc = acc + tg[:, k:k + 1] * _load_token_major(buf, tm, j, (k,))
        blocks.append(acc)
    x = x_ref[...] + gate_ref[0, 0] * jnp.concatenate(blocks, axis=1)
    if final is None:
        o_ref[...] = x
    else:
        x = x * lax.rsqrt(jnp.mean(x * x, axis=-1, keepdims=True) + RMS_EPS) * ng_ref[...]
        first_ref, second_ref = o_ref

        @pl.when(pl.program_id(0) < final)
        def _():
            first_ref[...] = x

        @pl.when(pl.program_id(0) >= final)
        def _():
            second_ref[...] = x


def _combine(y_rows, dest2d, tg, x, mod4, rowmap, norm_g, split_rows, tm):
    M, D = x.shape
    nd = tm * TOP_K // LANES
    if split_rows is None:
        final = None
        out_specs = pl.BlockSpec((tm, D), lambda i, rm: (i, 0))
        out_shape = jax.ShapeDtypeStruct((M, D), F32)
    else:
        final = split_rows // tm
        out_specs = [pl.BlockSpec((tm, D), lambda i, rm: (jnp.minimum(i, final - 1), 0)),
                     pl.BlockSpec((tm, D), lambda i, rm: (jnp.maximum(i - final, 0), 0))]
        out_shape = [jax.ShapeDtypeStruct((split_rows, D), F32), jax.ShapeDtypeStruct((M - split_rows, D), F32)]
    return _combine_call(final, out_specs, out_shape, y_rows, dest2d, tg, x, mod4, rowmap, norm_g, tm, nd)


def _combine_call(final, out_specs, out_shape, y_rows, dest2d, tg, x, mod4, rowmap, norm_g, tm, nd):
    M, D = x.shape
    return pl.pallas_call(
        functools.partial(_combine_kernel, final),
        grid_spec=pltpu.PrefetchScalarGridSpec(
            num_scalar_prefetch=1, grid=(M // tm,),
            in_specs=[pl.BlockSpec((nd, LANES), lambda i, rm: (i, 0)),
                      pl.BlockSpec(memory_space=pl.ANY),
                      pl.BlockSpec((tm, LANES), lambda i, rm: (i, 0)),
                      pl.BlockSpec((tm, D), lambda i, rm: (i, 0)),
                      pl.BlockSpec((1, 1, 1, D), lambda i, rm: (rm[i], 5, 0, 0)),
                      pl.BlockSpec((1, D), lambda i, rm: (0, 0))],
            out_specs=out_specs,
            scratch_shapes=[pltpu.VMEM((TOP_K, tm * D // LANES, LANES), F32), pltpu.SMEM((nd, LANES), I32),
                            pltpu.SemaphoreType.DMA((2,))]),
        out_shape=out_shape,
        compiler_params=_params("arbitrary"),
        name="moe_combine",
    )(rowmap, dest2d, y_rows, tg, x, mod4, norm_g.reshape(1, D))


def _moe_kernel(nf, te_ref, nv_ref, x_ref, wg_ref, wu_ref, bg_ref, bu_ref, wo_ref, bo_ref, o_ref,
                acc_ref, xb_ref):
    del te_ref
    t, f = pl.program_id(0), pl.program_id(1)
    last = nf - 1
    valid = t < nv_ref[0]

    tm, d = xb_ref.shape
    nb = d // LANES

    @pl.when(valid & (f == 0))
    def _():
        for j in range(nb):
            xb_ref[:, j * LANES:(j + 1) * LANES] = _load_token_major(x_ref, tm, j).astype(BF16)

    @pl.when(valid)
    def _():
        x = xb_ref[...]
        g = jnp.dot(x, wg_ref[0], preferred_element_type=F32) + bg_ref[0]
        u = jnp.dot(x, wu_ref[0], preferred_element_type=F32) + bu_ref[0]
        g = jnp.minimum(g, SWIGLU_LIMIT)
        u = jnp.clip(u, -SWIGLU_LIMIT, SWIGLU_LIMIT)
        h = ((u + 1.0) * g * jax.nn.sigmoid(SWIGLU_ALPHA * g)).astype(BF16)
        part = jnp.dot(h, wo_ref[0], preferred_element_type=F32)

        if nf == 1:
            _store_token_major(o_ref, part + bo_ref[0])
        else:
            @pl.when(f == 0)
            def _():
                acc_ref[...] = part + bo_ref[0]

            @pl.when((f > 0) & (f < last))
            def _():
                acc_ref[...] += part

            @pl.when(f == last)
            def _():
                _store_token_major(o_ref, acc_ref[...] + part)

    @pl.when(jnp.logical_not(valid) & (f == last))
    def _():
        o_ref[...] = jnp.zeros(o_ref.shape, o_ref.dtype)


def _moe_experts(x_rows, tile_expert, n_valid, w_in, b_in, w_out, b_out, tm, tf):
    E, D, F2 = w_in.shape
    nb = D // LANES
    R = x_rows.shape[0] // nb
    F = F2 // 2
    tf = _tile(F, tf)
    nf = F // tf

    def fi(t, f, nv):
        return jnp.where(t < nv[0], f, nf - 1)

    def ti(t, nv):
        return jnp.minimum(t, jnp.maximum(nv[0] - 1, 0))

    return pl.pallas_call(
        functools.partial(_moe_kernel, nf),
        grid_spec=pltpu.PrefetchScalarGridSpec(
            num_scalar_prefetch=2, grid=(R // tm, nf),
            in_specs=[pl.BlockSpec((tm * nb, LANES), lambda t, f, te, nv: (ti(t, nv), 0)),
                      pl.BlockSpec((1, D, tf), lambda t, f, te, nv: (te[t], 0, fi(t, f, nv))),
                      pl.BlockSpec((1, D, tf), lambda t, f, te, nv: (te[t], 0, nf + fi(t, f, nv))),
                      pl.BlockSpec((1, 1, tf), lambda t, f, te, nv: (te[t], 0, fi(t, f, nv))),
                      pl.BlockSpec((1, 1, tf), lambda t, f, te, nv: (te[t], 0, nf + fi(t, f, nv))),
                      pl.BlockSpec((1, tf, D), lambda t, f, te, nv: (te[t], fi(t, f, nv), 0)),
                      pl.BlockSpec((1, 1, D), lambda t, f, te, nv: (te[t], 0, 0))],
            out_specs=pl.BlockSpec((tm * nb, LANES), lambda t, f, te, nv: (t, 0)),
            scratch_shapes=[pltpu.VMEM((tm, D), F32), pltpu.VMEM((tm, D), BF16)]),
        out_shape=jax.ShapeDtypeStruct((R * nb, LANES), F32),
        compiler_params=_params("parallel", "arbitrary"),
        name="moe_experts",
    )(tile_expert, n_valid, x_rows, w_in, w_in, b_in.reshape(E, 1, F2), b_in.reshape(E, 1, F2),
      w_out, b_out.reshape(E, 1, D))


def _moe_layer(x, norm_g, mod4, rowmap, tm_tok, w_router, b_router, w_in, b_in, w_out, b_out, first_expert,
               final_g=None, split_rows=None, tm=512, tf=1024):
    M, D = x.shape
    E = w_router.shape[1]
    h, ti, tg, rk, cnt = _router(x, norm_g, mod4, rowmap, w_router, b_router, tm_tok)
    A = M * TOP_K
    n_tiles = -(-A // tm) + E
    counts = cnt[0, :E]
    ptiles = (counts + tm - 1) // tm
    pend = jnp.cumsum(ptiles)
    pstart = pend - ptiles
    n_valid = pend[-1]
    tix = jnp.minimum(jnp.arange(n_tiles, dtype=I32), jnp.maximum(n_valid - 1, 0))
    tile_expert = jnp.minimum(jnp.searchsorted(pend, tix, side='right'), E - 1).astype(I32)
    dest = _dest_rows((pstart * tm).astype(I32), ti, rk, tm_tok)
    dest2d = dest[:, :TOP_K].reshape(A // LANES, LANES)
    tail = jnp.stack([n_valid, jnp.asarray(n_tiles, n_valid.dtype)]).astype(I32)
    x_rows = _dispatch(h, D // LANES, dest2d, (pstart * tm + counts).astype(I32), (ptiles * tm - counts).astype(I32),
                       tail, n_tiles * tm, tm_tok, tm)
    y_rows = _moe_experts(x_rows, tile_expert + first_expert, n_valid.reshape(1).astype(I32),
                          w_in, b_in, w_out, b_out, tm, tf)
    return _combine(y_rows, dest2d, tg, x, mod4, rowmap, norm_g if final_g is None else final_g,
                    split_rows if final_g is not None else None, tm_tok)


def _grid_pos_embedding(T, D):
    rows = T // GRID_W
    row = jnp.repeat(jnp.arange(rows), GRID_W)
    col = jnp.tile(jnp.arange(GRID_W), rows)
    quarter = D // 4
    omega = 1.0 / (POS_BASE ** (jnp.arange(quarter, dtype=F32) / quarter))

    def axis_emb(p):
        ang = p.astype(F32)[:, None] * omega[None]
        return jnp.concatenate([jnp.sin(ang), jnp.cos(ang)], axis=-1)

    return jnp.concatenate([axis_emb(row), axis_emb(col)], axis=-1)


def kernel(x_prompt, x_sample, state_delta, c, c_ctx, w_mod, b_mod, norm_mix, norm_ffn, norm_final, dn_w_in, dn_conv, dn_a_log, dn_dt_bias, dn_norm, dn_w_out, hy_w_in, hy_conv, hy_w1, hy_b1, hy_w2, hy_b2, hy_w3, hy_b3, hy_w4, hy_freq, hy_bias, hy_w_out, moe_w_router, moe_b_router, moe_w_in, moe_b_in, moe_w_out, moe_b_out):
    Bp, Tp, D = x_prompt.shape
    Bs, Ts, _ = x_sample.shape
    depth = w_mod.shape[0]
    H = state_delta.shape[3]
    Mp, Ms = Bp * Tp, Bs * Ts
    M = Mp + Ms
    assert Mp % Ts == 0 and Ts % Tp == 0 and Tp % DN_CHUNK == 0, "token groups must tile each other"
    tT = Tp
    tm_big = _tile(math.gcd(Mp, Ts), 1024)
    tm_mid = _tile(math.gcd(Mp, Ts), 512)
    tm_tok = min(256, Tp)

    def rowmap_for(tm):
        tile_start = np.arange(M // tm) * tm
        return jnp.asarray(np.where(tile_start < Mp, 0, 1 + (tile_start - Mp) // Ts), I32)

    rm_big, rm_mid, rm_tok = rowmap_for(tm_big), rowmap_for(tm_mid), rowmap_for(tm_tok)

    xs = x_sample + _grid_pos_embedding(Ts, D)[None]
    x = jnp.concatenate([x_prompt.reshape(Mp, D), xs.reshape(Ms, D)], axis=0)
    conv_start = np.arange(M // tT) * tT
    seq_len = np.where(conv_start < Mp, Tp, Ts)
    seq_pos = np.where(conv_start < Mp, conv_start % Tp, (conv_start - Mp) % Ts)
    has_prev = jnp.asarray(seq_pos > 0, I32)
    has_next = jnp.asarray(seq_pos + tT < seq_len, I32)

    n_cond = 1 + Bs
    r_pad = -(-n_cond // 8) * 8
    cond = jnp.zeros((r_pad, D), F32).at[0].set(c_ctx).at[1:n_cond].set(c)
    mod = _modulation(cond, w_mod, b_mod)

    n_experts = moe_w_in.shape[1]
    w_in_all = moe_w_in.astype(BF16).reshape((depth * n_experts,) + moe_w_in.shape[2:])
    w_out_all = moe_w_out.astype(BF16).reshape((depth * n_experts,) + moe_w_out.shape[2:])
    b_in_all = moe_b_in.reshape(depth * n_experts, -1)
    b_out_all = moe_b_out.reshape(depth * n_experts, -1)

    new_states = []
    i_dn = i_hy = 0
    for l in range(depth):
        mod4 = mod[l].reshape(r_pad, 6, 1, D)
        if l % 2 == 0:
            i = i_dn
            i_dn += 1
            w_in = dn_w_in[i]
            n_main = 4 * H * HEAD_DIM
            w_ba = jnp.zeros((D, LANES), F32).at[:, :4 * H].set(w_in[:, n_main:]).astype(BF16)
            pm, ba = _fused_mm(_norm_mod_prologue, [x], [norm_mix[l]], [0, 1], w_in[:, :n_main].astype(BF16),
                               rm_big, mod4=mod4, side_w=w_ba, tm=tm_big, tn=1024)
            qkv, bg = _dn_act(pm, ba, dn_conv[i], dn_a_log[i], dn_dt_bias[i], has_prev, has_next, tT)
            hb = 2 if H % 2 == 0 else 1
            cols_p, gct_p = _delta_side_inputs(bg, 0, Bp, Tp, H, hb)
            cols_s, gct_s = _delta_side_inputs(bg, Mp, Bs, Ts, H, hb)
            og_p, s_fin = _delta(qkv, pm, cols_p, gct_p, dn_norm[i], None,
                                 n_batch=Bp, batch_off=0, T=Tp, hb=hb)
            og_s, _ = _delta(qkv, pm, cols_s, gct_s, dn_norm[i], state_delta[:, i].astype(F32),
                             n_batch=Bs, batch_off=Mp // Ts, T=Ts, hb=hb)
            new_states.append(s_fin.astype(state_delta.dtype))
            x = _fused_mm(functools.partial(_two_part_prologue, Mp // tm_big), [og_p], [], [],
                          dn_w_out[i].astype(BF16), rm_big, mod4=mod4, a_rest=og_s,
                          res=x, gate_chunk=2, out_dtype=F32, tm=tm_big, tn=1024)
        else:
            i = i_hy
            i_hy += 1
            u = _fused_mm(_norm_mod_prologue, [x], [norm_mix[l]], [0, 1], hy_w_in[i].astype(BF16),
                          rm_big, mod4=mod4, tm=tm_big, tn=1024)
            x0, s = _hy_pre(u, hy_conv[i], has_prev, has_next, tT)
            hy = (hy_w1[i], hy_b1[i], hy_w2[i], hy_b2[i], hy_w3[i], hy_b3[i], hy_w4[i], hy_freq[i])
            conv_p = _hyena_conv(s, Bp, 0, Tp, _hyena_filter_taps(Tp, *hy))
            conv_s = _hyena_conv(s, Bs, Mp, Ts, _hyena_filter_taps(Ts, *hy))
            x = _fused_mm(functools.partial(_hyena_gate_prologue, Mp // tm_mid), [conv_p, x0, s], [hy_bias[i]], [],
                          hy_w_out[i].astype(BF16), rm_mid, mod4=mod4, a_rest=conv_s,
                          res=x, gate_chunk=2, out_dtype=F32, tm=tm_mid, tn=1024)
        last = l == depth - 1
        x = _moe_layer(x, norm_ffn[l], mod4, rm_tok, tm_tok, moe_w_router[l], moe_b_router[l],
                       w_in_all, b_in_all, w_out_all, b_out_all, l * n_experts,
                       final_g=norm_final if last else None, split_rows=Mp)

    y_prompt, y_sample = x
    return y_prompt.reshape(Bp, Tp, D), y_sample.reshape(Bs, Ts, D), jnp.stack(new_states, axis=1)
```

```python
import functools
import math

import numpy as np
import jax
import jax.numpy as jnp
from jax import lax
from jax.experimental import pallas as pl
from jax.experimental.pallas import tpu as pltpu

F32, BF16, I32 = jnp.float32, jnp.bfloat16, jnp.int32

V7X_VMEM_LIMIT_BYTES = 56 * 2**20
LANES = 128
BF16_ROWS = 16

GRID_W = 64
DN_CHUNK = 128
HEAD_DIM = 128
TOP_K = 4
RMS_EPS = 1e-6
L2_EPS = 1e-6
POS_BASE = 10000.0
SWIGLU_LIMIT = 7.0
SWIGLU_ALPHA = 1.702
HY_TARGET = 1e-2
HY_FAST = 0.3
HY_SLOW = 1.5

LOG2_LANES = LANES.bit_length() - 1
LOG2_TOP_K = TOP_K.bit_length() - 1
assert 1 << LOG2_LANES == LANES and 1 << LOG2_TOP_K == TOP_K

HIGHEST = lax.Precision.HIGHEST
NT_DIMS = (((1,), (1,)), ((), ()))
TN_DIMS = (((0,), (0,)), ((), ()))


def _params(*sem):
    return pltpu.CompilerParams(dimension_semantics=sem, vmem_limit_bytes=V7X_VMEM_LIMIT_BYTES)


def _tile(n, pref):
    if n <= pref:
        return n
    t = pref - pref % LANES
    while n % t:
        t -= LANES
    return t


def _silu(x):
    return x * jax.nn.sigmoid(x)


def _softplus(x):
    return jnp.maximum(x, 0.0) + jnp.log1p(jnp.exp(-jnp.abs(x)))


def _split_bf16(x):
    hi = x.astype(BF16)
    lo = (x - hi.astype(F32)).astype(BF16)
    return hi, lo


def _dot3(a, b):
    ah, al = _split_bf16(a)
    bh, bl = _split_bf16(b)
    return (jnp.dot(ah, bh, preferred_element_type=F32)
            + (jnp.dot(ah, bl, preferred_element_type=F32) + jnp.dot(al, bh, preferred_element_type=F32)))


def _modulation_kernel(c_ref, w_ref, b_ref, o_ref):
    s = _silu(c_ref[...]).astype(BF16)
    o_ref[0] = jnp.dot(s, w_ref[0].astype(BF16), preferred_element_type=F32) + b_ref[0]


def _modulation(cond, w_mod, b_mod):
    R, D = cond.shape
    L, _, N = w_mod.shape
    tn = _tile(N, 1024)
    return pl.pallas_call(
        _modulation_kernel,
        grid=(L, N // tn),
        in_specs=[pl.BlockSpec((R, D), lambda l, j: (0, 0)),
                  pl.BlockSpec((1, D, tn), lambda l, j: (l, 0, j)),
                  pl.BlockSpec((1, 1, tn), lambda l, j: (l, 0, j))],
        out_specs=pl.BlockSpec((1, R, tn), lambda l, j: (l, 0, j)),
        out_shape=jax.ShapeDtypeStruct((L, R, N), F32),
        compiler_params=_params("parallel", "parallel"),
        name="modulation",
    )(cond, w_mod, b_mod.reshape(L, 1, N))


def _fused_mm_kernel(prologue, n_a, n_v, n_m, has_res, has_side, rm_ref, *refs):
    del rm_ref
    a_refs, refs = refs[:n_a], refs[n_a:]
    v_refs, refs = refs[:n_v], refs[n_v:]
    m_refs, refs = refs[:n_m], refs[n_m:]
    w_ref, refs = refs[0], refs[1:]
    if has_res:
        res_ref, gate_ref, refs = refs[0], refs[1], refs[2:]
    if has_side:
        sw_ref, refs = refs[0], refs[1:]
    o_ref, refs = refs[0], refs[1:]
    if has_side:
        so_ref, refs = refs[0], refs[1:]
    a_scr = refs[0]

    row_tile = pl.program_id(0)

    @pl.when(pl.program_id(1) == 0)
    def _():
        a = prologue([r[...] for r in a_refs], [r[...] for r in v_refs], [r[0, 0] for r in m_refs], row_tile)
        a_scr[...] = a.astype(BF16)
        if has_side:
            so_ref[...] = jnp.dot(a_scr[...], sw_ref[...], preferred_element_type=F32)

    acc = jnp.dot(a_scr[...], w_ref[...], preferred_element_type=F32)
    if has_res:
        acc = res_ref[...] + gate_ref[0, 0] * acc
    o_ref[...] = acc.astype(o_ref.dtype)


def _fused_mm(prologue, a_ins, vec_ins, mod_ins, w, rowmap, *, mod4=None, res=None, gate_chunk=None,
              side_w=None, a_rest=None, out_dtype=BF16, tm, tn):
    a_split = None if a_rest is None else a_ins[0].shape[0]
    K = a_ins[0].shape[1]
    M = rowmap.shape[0] * tm
    N = w.shape[1]
    tn = _tile(N, tn)
    in_specs, args = [], []
    for a in a_ins:
        in_specs.append(pl.BlockSpec((tm, K), lambda i, j, rm: (i, 0)))
        args.append(a)
    if a_split is not None:
        n_first = a_split // tm
        in_specs[0] = pl.BlockSpec((tm, K), lambda i, j, rm: (jnp.minimum(i, n_first - 1), 0))
        in_specs.insert(1, pl.BlockSpec((tm, K), lambda i, j, rm: (jnp.maximum(i - n_first, 0), 0)))
        args.insert(1, a_rest)
    for v in vec_ins:
        in_specs.append(pl.BlockSpec((1, K), lambda i, j, rm: (0, 0)))
        args.append(v.reshape(1, K))
    for c in mod_ins:
        in_specs.append(pl.BlockSpec((1, 1, 1, K), lambda i, j, rm, c=c: (rm[i], c, 0, 0)))
        args.append(mod4)
    in_specs.append(pl.BlockSpec((K, tn), lambda i, j, rm: (0, j)))
    args.append(w)
    if res is not None:
        in_specs.append(pl.BlockSpec((tm, tn), lambda i, j, rm: (i, j)))
        args.append(res)
        in_specs.append(pl.BlockSpec((1, 1, 1, tn), lambda i, j, rm, c=gate_chunk: (rm[i], c, 0, j)))
        args.append(mod4)
    out_specs = [pl.BlockSpec((tm, tn), lambda i, j, rm: (i, j))]
    out_shape = [jax.ShapeDtypeStruct((M, N), out_dtype)]
    if side_w is not None:
        ns = side_w.shape[1]
        in_specs.append(pl.BlockSpec((K, ns), lambda i, j, rm: (0, 0)))
        args.append(side_w)
        out_specs.append(pl.BlockSpec((tm, ns), lambda i, j, rm: (i, 0)))
        out_shape.append(jax.ShapeDtypeStruct((M, ns), F32))
    kern = functools.partial(_fused_mm_kernel, prologue, len(a_ins) + (a_rest is not None), len(vec_ins),
                             len(mod_ins), res is not None, side_w is not None)
    outs = pl.pallas_call(
        kern,
        grid_spec=pltpu.PrefetchScalarGridSpec(
            num_scalar_prefetch=1, grid=(M // tm, N // tn), in_specs=in_specs, out_specs=out_specs,
            scratch_shapes=[pltpu.VMEM((tm, K), BF16)]),
        out_shape=out_shape,
        compiler_params=_params("parallel", "arbitrary"),
        name="fused_mm",
    )(rowmap, *args)
    return outs if side_w is not None else outs[0]


def _norm_mod_prologue(a, v, m, row_tile=None):
    x, g, (shift, scale) = a[0], v[0], m
    y = x * lax.rsqrt(jnp.mean(x * x, axis=-1, keepdims=True) + RMS_EPS) * g
    return y * (1.0 + scale) + shift


def _two_part_prologue(n_first_tiles, a, v, m, row_tile):
    return jnp.where(row_tile < n_first_tiles, a[0], a[1])


def _hyena_gate_prologue(n_first_tiles, a, v, m, row_tile):
    conv = jnp.where(row_tile < n_first_tiles, a[0], a[1]).astype(F32)
    x0, s = a[2].astype(F32), a[3].astype(F32)
    return x0 * (conv + s * v[0])


def _conv3(x, prev_row, next_row, w, rid):
    t = x.shape[0]
    xm = jnp.where(rid == 0, prev_row, pltpu.roll(x, 1, 0))
    xp = jnp.where(rid == t - 1, next_row, pltpu.roll(x, t - 1, 0))
    return xm * w[0:1] + x * w[1:2] + xp * w[2:3]


def _halo_specs(tT, width, col_block):
    hb = tT // BF16_ROWS
    return [
        pl.BlockSpec((tT, width), lambda i, hp, hn: (i, col_block)),
        pl.BlockSpec((BF16_ROWS, width), lambda i, hp, hn: (jnp.maximum(i * hb - 1, 0), col_block)),
        pl.BlockSpec((BF16_ROWS, width), lambda i, hp, hn: ((i + 1) * hb * hn[i], col_block)),
    ]


def _dn_act_kernel(n_heads, hp_ref, hn_ref, x_ref, xp_ref, xn_ref, ba_ref, cw_ref, al_ref, dtb_ref,
                   qkv_ref, bg_ref):
    i = pl.program_id(0)
    hp = hp_ref[i].astype(F32)
    hn = hn_ref[i].astype(F32)
    tT = x_ref.shape[0]
    rid = lax.broadcasted_iota(I32, (tT, LANES), 0)
    for cb in range(3 * n_heads):
        sl = slice(cb * LANES, (cb + 1) * LANES)
        x = x_ref[:, sl].astype(F32)
        pr = xp_ref[:, sl].astype(F32)[BF16_ROWS - 1:BF16_ROWS] * hp
        nx = xn_ref[:, sl].astype(F32)[0:1] * hn
        y = _silu(_conv3(x, pr, nx, cw_ref[:, sl], rid))
        if cb < 2 * n_heads:
            y = y * lax.rsqrt(jnp.sum(y * y, axis=-1, keepdims=True) + L2_EPS)
            if cb < n_heads:
                y = y * (HEAD_DIM ** -0.5)
        qkv_ref[:, sl] = y.astype(BF16)

    ba = ba_ref[...]
    lane = lax.broadcasted_iota(I32, (tT, LANES), 1)
    beta = jax.nn.sigmoid(ba)
    g = -jnp.exp(al_ref[...]) * _softplus(ba + dtb_ref[...])
    r = lax.broadcasted_iota(I32, (tT, tT), 0)
    c = lax.broadcasted_iota(I32, (tT, tT), 1)
    shift = DN_CHUNK.bit_length() - 1
    same = jnp.right_shift(r, shift) == jnp.right_shift(c, shift)
    l_pre = jnp.where(same & (c <= r), 1.0, 0.0).astype(F32)
    l_suf = jnp.where(same & (c >= r), 1.0, 0.0).astype(F32)
    g_pre = jnp.dot(l_pre, g, precision=HIGHEST, preferred_element_type=F32)
    g_suf = jnp.dot(l_suf, g, precision=HIGHEST, preferred_element_type=F32)
    gc = jnp.where(lane < 3 * n_heads, g_pre, g_suf)
    bg_ref[...] = jnp.where(lane < 2 * n_heads, beta, gc)


def _dn_act(pm, ba, conv_w, a_log, dt_bias, has_prev, has_next, tT):
    M = pm.shape[0]
    cw = conv_w.shape[1]
    n_heads = cw // (3 * HEAD_DIM)
    pad = lambda v: jnp.zeros((1, LANES), F32).at[0, 2 * n_heads:4 * n_heads].set(v.reshape(-1).astype(F32))
    vec = pl.BlockSpec((1, LANES), lambda i, hp, hn: (0, 0))
    return pl.pallas_call(
        functools.partial(_dn_act_kernel, n_heads),
        grid_spec=pltpu.PrefetchScalarGridSpec(
            num_scalar_prefetch=2, grid=(M // tT,),
            in_specs=_halo_specs(tT, cw, 0) + [
                pl.BlockSpec((tT, LANES), lambda i, hp, hn: (i, 0)),
                pl.BlockSpec((3, cw), lambda i, hp, hn: (0, 0)), vec, vec],
            out_specs=[pl.BlockSpec((tT, cw), lambda i, hp, hn: (i, 0)),
                       pl.BlockSpec((tT, LANES), lambda i, hp, hn: (i, 0))]),
        out_shape=[jax.ShapeDtypeStruct((M, cw), BF16), jax.ShapeDtypeStruct((M, LANES), F32)],
        compiler_params=_params("parallel"),
        name="dn_act",
    )(has_prev, has_next, pm, pm, pm, ba, conv_w, pad(a_log), pad(dt_bias))


def _merge_masks(ri, ci, n, lower):
    masks = []
    s = 0
    while (1 << s) < n:
        same = jnp.right_shift(ri, s + 1) == jnp.right_shift(ci, s + 1)
        hi_r = jnp.bitwise_and(jnp.right_shift(ri, s), 1)
        hi_c = jnp.bitwise_and(jnp.right_shift(ci, s), 1)
        off = (hi_r == 1) & (hi_c == 0) if lower else (hi_r == 0) & (hi_c == 1)
        masks.append(same & off)
        s += 1
    return masks


def _dot3_many(xs, ys):
    sx = [_split_bf16(x) for x in xs]
    sy = [_split_bf16(y) for y in ys]
    hh = [jnp.dot(x[0], y[0], preferred_element_type=F32) for x, y in zip(sx, sy)]
    hl = [jnp.dot(x[0], y[1], preferred_element_type=F32) for x, y in zip(sx, sy)]
    lh = [jnp.dot(x[1], y[0], preferred_element_type=F32) for x, y in zip(sx, sy)]
    return [a + (b + c) for a, b, c in zip(hh, hl, lh)]


def _dot1_many(xs, ys):
    return [jnp.dot(x.astype(BF16), y.astype(BF16), preferred_element_type=F32) for x, y in zip(xs, ys)]


def _unit_tri_inverse_many(mats, eye, masks, stricts):
    ts = [eye - jnp.where(m[0], a, 0.0) for a, m in zip(mats, masks)]
    for lvl in range(1, len(masks[0])):
        off = [jnp.where(m[lvl], a, 0.0) for a, m in zip(mats, masks)]
        upd = _dot1_many(_dot1_many(ts, off), ts)
        ts = [t - u for t, u in zip(ts, upd)]
    full = [eye + jnp.where(st, a, 0.0) for a, st in zip(mats, stricts)]
    res = [eye - p for p in _dot3_many(full, ts)]
    return [t + c for t, c in zip(ts, _dot1_many(ts, res))]


def _delta_kernel(hb, n_chunks, group, has_s0, *refs):
    if has_s0:
        q_ref, k_ref, v_ref, z_ref, cols_ref, gct_ref, ng_ref, s0_ref, o_ref, sfin_ref = refs[:10]
        scr = refs[10:]
    else:
        q_ref, k_ref, v_ref, z_ref, cols_ref, gct_ref, ng_ref, o_ref, sfin_ref = refs[:9]
        scr = refs[9:]
    u_scr, wq_scr, qk_scr, s_scr, o_scr = scr
    C = DN_CHUNK
    n = n_chunks
    ri = lax.broadcasted_iota(I32, (C, C), 0)
    ci = lax.broadcasted_iota(I32, (C, C), 1)
    eye = (ri == ci).astype(F32)
    incl = (ri >= ci, ri <= ci)

    def gate_cols(hh, d, rows):
        bcol = cols_ref[0, 0, rows, hh * 4 + d:hh * 4 + d + 1]
        gcol = cols_ref[0, 0, rows, hh * 4 + 2 + d:hh * 4 + 3 + d]
        gl = gcol[C - 1:C, :] if d == 0 else gcol[0:1, :]
        return bcol, gcol, gl

    def prep(it, carry):
        tri_masks = (_merge_masks(ri, ci, C, True), _merge_masks(ri, ci, C, False))
        tri_strict = (ri > ci, ri < ci)
        where_, mats, masks, stricts, rhss = [], [], [], [], []
        for gi in range(group):
            c = it * group + gi
            rows = pl.ds(pl.multiple_of(c * C, C), C)
            for hh in range(hb):
                ls = slice(hh * HEAD_DIM, (hh + 1) * HEAD_DIM)
                qb, kb16, vb = q_ref[0, rows, ls], k_ref[0, rows, ls], v_ref[0, rows, ls]
                qf, kf, vf = qb.astype(F32), kb16.astype(F32), vb.astype(F32)
                qkt = lax.dot_general(qb, kb16, NT_DIMS, preferred_element_type=F32)
                for d in range(2):
                    ch = hh * 2 + d
                    bcol, gcol, _ = gate_cols(hh, d, rows)
                    grow = gct_ref[0, hh, d, pl.ds(c, 1), :]
                    dm = jnp.where(incl[d], jnp.exp(jnp.where(incl[d], gcol - grow, 0.0)), 0.0)
                    kbeta = kf * bcol
                    eg = jnp.exp(gcol)
                    mats.append(lax.dot_general(kbeta.astype(BF16), kb16, NT_DIMS, preferred_element_type=F32) * dm)
                    masks.append(tri_masks[d])
                    stricts.append(tri_strict[d])
                    rhss.append(jnp.concatenate([vf * bcol, kbeta * eg], axis=1))
                    where_.append((ch, c, rows))
                    wq_scr[ch, pl.ds(pl.multiple_of(c * 2 * C + C, C), C), :] = (qf * eg).astype(BF16)
                    qk_scr[ch, rows, :] = (qkt * dm).astype(BF16)
        sols = _dot3_many(_unit_tri_inverse_many(mats, eye, masks, stricts), rhss)
        for (ch, c, rows), sol in zip(where_, sols):
            u_scr[ch, rows, :] = sol[:, :HEAD_DIM]
            wq_scr[ch, pl.ds(pl.multiple_of(c * 2 * C, 2 * C), C), :] = sol[:, HEAD_DIM:].astype(BF16)
        return carry

    lax.fori_loop(0, n // group, prep, 0)

    for hh in range(hb):
        for d in range(2):
            s_scr[hh * 2 + d] = s0_ref[0, d, hh] if has_s0 else jnp.zeros((HEAD_DIM, HEAD_DIM), F32)

    T = n * C
    tr = min(T, 256)

    def clear(b, carry):
        o_scr[:, pl.ds(pl.multiple_of(b * tr, tr), tr), :] = jnp.zeros((hb, tr, HEAD_DIM), F32)
        return carry

    lax.fori_loop(0, T // tr, clear, 0)

    def scan(it, carry):
        chains = [(hh, d) for hh in range(hb) for d in range(2)]
        cs = [it if d == 0 else n - 1 - it for _, d in chains]
        rows = [pl.ds(pl.multiple_of(c * C, C), C) for c in cs]
        ss = [s_scr[hh * 2 + d] for hh, d in chains]
        rs = [jnp.dot(wq_scr[hh * 2 + d, pl.ds(pl.multiple_of(c * 2 * C, 2 * C), 2 * C), :], s.astype(BF16),
                      preferred_element_type=F32) for (hh, d), c, s in zip(chains, cs, ss)]
        vns = [(u_scr[hh * 2 + d, rw, :] - r[:C]).astype(BF16) for (hh, d), rw, r in zip(chains, rows, rs)]
        for i, (hh, d) in enumerate(chains):
            _, gcol, gl = gate_cols(hh, d, rows[i])
            kt = (k_ref[0, rows[i], hh * HEAD_DIM:(hh + 1) * HEAD_DIM].astype(F32) * jnp.exp(gl - gcol)).astype(BF16)
            s_scr[hh * 2 + d] = (ss[i] * jnp.exp(gl)
                                 + lax.dot_general(kt, vns[i], TN_DIMS, preferred_element_type=F32))
        for i, (hh, d) in enumerate(chains):
            o = rs[i][C:] + jnp.dot(qk_scr[hh * 2 + d, rows[i], :], vns[i], preferred_element_type=F32)
            o_scr[hh, rows[i], :] += o
        return carry

    lax.fori_loop(0, n, scan, 0)

    for hh in range(hb):
        for d in range(2):
            sfin_ref[0, d, hh] = s_scr[hh * 2 + d]

    def gate(b, carry):
        rows = pl.ds(pl.multiple_of(b * tr, tr), tr)
        for hh in range(hb):
            ls = slice(hh * HEAD_DIM, (hh + 1) * HEAD_DIM)
            o = o_scr[hh, rows, :]
            o = o * lax.rsqrt(jnp.mean(o * o, axis=-1, keepdims=True) + RMS_EPS) * ng_ref[...]
            o_ref[0, rows, ls] = (o * _silu(z_ref[0, rows, ls].astype(F32))).astype(BF16)
        return carry

    lax.fori_loop(0, T // tr, gate, 0)


def _delta(qkv, pm, cols, gct, norm_g, s0, *, n_batch, batch_off, T, hb):
    M, cw = qkv.shape
    H = cw // (3 * HEAD_DIM)
    n = T // DN_CHUNK
    W = hb * HEAD_DIM
    nb = H // hb
    qkv3 = qkv.reshape(M // T, T, cw)
    pm3 = pm.reshape(M // T, T, pm.shape[1])
    once = pl.Buffered(1)
    blk = lambda off: pl.BlockSpec((1, T, W), lambda b, j, off=off: (b + batch_off, 0, off * nb + j),
                                   pipeline_mode=once)
    in_specs = [blk(0), blk(1), blk(2), blk(3),
                pl.BlockSpec((1, 1, T, 4 * hb), lambda b, j: (b, j, 0, 0), pipeline_mode=once),
                pl.BlockSpec((1, hb, 2, n, DN_CHUNK), lambda b, j: (b, j, 0, 0, 0)),
                pl.BlockSpec((1, HEAD_DIM), lambda b, j: (0, 0))]
    args = [qkv3, qkv3, qkv3, pm3, cols, gct, norm_g.reshape(1, HEAD_DIM)]
    if s0 is not None:
        in_specs.append(pl.BlockSpec((1, 2, hb, HEAD_DIM, HEAD_DIM), lambda b, j: (b, 0, j, 0, 0)))
        args.append(s0)
    nch = 2 * hb
    group = math.gcd(n, max(1, 16 // nch))
    out, sfin = pl.pallas_call(
        functools.partial(_delta_kernel, hb, n, group, s0 is not None),
        grid=(n_batch, nb),
        in_specs=in_specs,
        out_specs=[pl.BlockSpec((1, T, W), lambda b, j: (b, 0, j)),
                   pl.BlockSpec((1, 2, hb, HEAD_DIM, HEAD_DIM), lambda b, j: (b, 0, j, 0, 0))],
        out_shape=[jax.ShapeDtypeStruct((n_batch, T, H * HEAD_DIM), BF16),
                   jax.ShapeDtypeStruct((n_batch, 2, H, HEAD_DIM, HEAD_DIM), F32)],
        scratch_shapes=[pltpu.VMEM((nch, T, HEAD_DIM), F32),
                        pltpu.VMEM((nch, 2 * T, HEAD_DIM), BF16),
                        pltpu.VMEM((nch, T, DN_CHUNK), BF16),
                        pltpu.VMEM((nch, HEAD_DIM, HEAD_DIM), F32),
                        pltpu.VMEM((hb, T, HEAD_DIM), F32)],
        compiler_params=_params("parallel", "parallel"),
        name="delta_rule",
    )(*args)
    return out.reshape(n_batch * T, H * HEAD_DIM), sfin


def _delta_side_inputs(bg, t0, n_batch, T, H, hb):
    b5 = bg[t0:t0 + n_batch * T, :4 * H].reshape(n_batch, T, 2, 2, H)
    cols = jnp.transpose(b5, (0, 4, 1, 2, 3)).reshape(n_batch, H // hb, hb, T, 4)
    cols = jnp.transpose(cols, (0, 1, 3, 2, 4)).reshape(n_batch, H // hb, T, 4 * hb)
    gct = jnp.transpose(b5[:, :, 1], (0, 3, 2, 1)).reshape(n_batch, H, 2, T // DN_CHUNK, DN_CHUNK)
    return cols, gct


def _hy_pre_kernel(D, hp_ref, hn_ref, x_ref, xp_ref, xn_ref, cw_ref, x0_ref, s_ref):
    i = pl.program_id(0)
    hp = hp_ref[i].astype(F32)
    hn = hn_ref[i].astype(F32)
    tT = x_ref.shape[0]
    W = min(D, 2 * LANES)
    rid = lax.broadcasted_iota(I32, (tT, W), 0)

    def conv(cb, part):
        sl = slice(part * D + cb * W, part * D + (cb + 1) * W)
        x = x_ref[:, sl].astype(F32)
        pr = xp_ref[:, sl].astype(F32)[BF16_ROWS - 1:BF16_ROWS] * hp
        nx = xn_ref[:, sl].astype(F32)[0:1] * hn
        return _conv3(x, pr, nx, cw_ref[:, sl], rid)

    for cb in range(D // W):
        sl = slice(cb * W, (cb + 1) * W)
        x0_ref[:, sl] = conv(cb, 0).astype(BF16)
        s_ref[:, sl] = (conv(cb, 1) * conv(cb, 2)).astype(BF16)


def _hy_pre(u, conv_w, has_prev, has_next, tT):
    M, W3 = u.shape
    D = W3 // 3
    out = pl.BlockSpec((tT, D), lambda i, hp, hn: (i, 0))
    return pl.pallas_call(
        functools.partial(_hy_pre_kernel, D),
        grid_spec=pltpu.PrefetchScalarGridSpec(
            num_scalar_prefetch=2, grid=(M // tT,),
            in_specs=_halo_specs(tT, W3, 0) + [pl.BlockSpec((3, W3), lambda i, hp, hn: (0, 0))],
            out_specs=[out, out]),
        out_shape=[jax.ShapeDtypeStruct((M, D), BF16)] * 2,
        compiler_params=_params("parallel"),
        name="hy_pre",
    )(has_prev, has_next, u, u, u, conv_w)


def _bmm_kernel(a_ref, b_ref, o_ref, acc_ref):
    k = pl.program_id(3)
    part = jnp.dot(a_ref[...], b_ref[0], preferred_element_type=F32)

    @pl.when(k == 0)
    def _():
        acc_ref[...] = part

    @pl.when(k > 0)
    def _():
        acc_ref[...] += part

    @pl.when(k == pl.num_programs(3) - 1)
    def _():
        o_ref[0] = acc_ref[...].astype(o_ref.dtype)


def _bmm(a, b, b_batch_off=0, n_batch=None, out_dtype=BF16, tm=1024, tn=1024, tk=4096):
    M, K = a.shape
    N = b.shape[2]
    nb = b.shape[0] if n_batch is None else n_batch
    tm, tn, tk = _tile(M, tm), _tile(N, tn), _tile(K, tk)
    return pl.pallas_call(
        _bmm_kernel,
        grid=(nb, M // tm, N // tn, K // tk),
        in_specs=[pl.BlockSpec((tm, tk), lambda i, m, n, k: (m, k)),
                  pl.BlockSpec((1, tk, tn), lambda i, m, n, k: (i + b_batch_off, k, n))],
        out_specs=pl.BlockSpec((1, tm, tn), lambda i, m, n, k: (i, m, n)),
        out_shape=jax.ShapeDtypeStruct((nb, M, N), out_dtype),
        scratch_shapes=[pltpu.VMEM((tm, tn), F32)],
        compiler_params=_params("parallel", "parallel", "parallel", "arbitrary"),
        name="dft_mm",
    )(a, b)


def _spec_prod_kernel(inv_n, s_ref, k_ref, y_ref):
    sc, ss = s_ref[0, 0].astype(F32), s_ref[0, 1].astype(F32)
    kc, ks = k_ref[0], k_ref[1]
    first = (lax.broadcasted_iota(I32, sc.shape, 0) == 0) & (pl.program_id(1) == 0)
    y_ref[0, 0] = (jnp.where(first, sc * kc, 2.0 * (sc * kc - ss * ks)) * inv_n).astype(y_ref.dtype)
    y_ref[0, 1] = (jnp.where(first, ss * ks, 2.0 * (sc * ks + ss * kc)) * inv_n).astype(y_ref.dtype)


def _spec_prod(sf, kf):
    B, _, T, D = sf.shape
    tr, tc = _tile(T, 256), _tile(D, 1024)
    return pl.pallas_call(
        functools.partial(_spec_prod_kernel, 1.0 / (2 * T)),
        grid=(B, T // tr, D // tc),
        in_specs=[pl.BlockSpec((1, 2, tr, tc), lambda b, i, j: (b, 0, i, j)),
                  pl.BlockSpec((2, tr, tc), lambda b, i, j: (0, i, j))],
        out_specs=pl.BlockSpec((1, 2, tr, tc), lambda b, i, j: (b, 0, i, j)),
        out_shape=jax.ShapeDtypeStruct(sf.shape, BF16),
        compiler_params=_params("parallel", "parallel", "parallel"),
        name="spec_prod",
    )(sf, kf)


def _dft_matrix(T):
    k = jnp.arange(T, dtype=I32)[:, None]
    t = jnp.arange(T, dtype=I32)[None, :]
    step = math.gcd(T, 64)
    th = jnp.arange(T // step, dtype=I32)[None, :] * step
    tl = jnp.arange(step, dtype=I32)[None, :]
    ang_h = ((k * th) % (2 * T)).astype(F32) * (math.pi / T)
    ang_l = ((k * tl) % (2 * T)).astype(F32) * (math.pi / T)
    ch, sh = jnp.cos(ang_h)[:, :, None], jnp.sin(ang_h)[:, :, None]
    cl, sl = jnp.cos(ang_l)[:, None, :], jnp.sin(ang_l)[:, None, :]
    cos = (ch * cl - sh * sl).reshape(T, T)
    sin = (sh * cl + ch * sl).reshape(T, T)
    nyq = jnp.where(t % 2 == 0, 1.0, -1.0).astype(F32)
    sin = jnp.where(k == 0, nyq, sin)
    return jnp.concatenate([cos, sin], axis=0).astype(BF16)


def _hyena_filter_taps(L, w1, b1, w2, b2, w3, b3, w4, freq):
    D = w4.shape[1] // 2
    n_bands = (w1.shape[0] - 1) // 2
    pos = jnp.arange(L, dtype=F32)
    t = pos / max(L - 1, 1)
    bands = jnp.linspace(1e-4, n_bands - 1, n_bands, dtype=F32)
    ang = (2.0 * math.pi / L) * pos[:, None] * bands[None]
    feats = jnp.concatenate([t[:, None], jnp.cos(ang), -jnp.sin(ang)], axis=-1)
    zf = jnp.sin(freq[0] * (feats @ w1 + b1))
    zf = jnp.sin(freq[1] * (zf @ w2 + b2))
    zf = jnp.sin(freq[2] * (zf @ w3 + b3))
    filt = (zf @ w4).reshape(L, 2, D)
    deltas = jnp.abs(jnp.linspace(math.log(HY_TARGET) / HY_SLOW, math.log(HY_TARGET) / HY_FAST, D, dtype=F32))
    filt = filt * jnp.exp(-t[:, None, None] * deltas[None, None])
    f = filt[:, 0]
    b = filt[:, 1].at[0].set(0.0)
    scale = lax.rsqrt(jnp.sum(f * f, axis=0) + jnp.sum(b * b, axis=0) + 1e-6)
    return f * scale, b * scale


def _hyena_conv(s, n_batch, t0, T, taps):
    M, D = s.shape
    f, b = taps
    fwd = _dft_matrix(T)
    kf = _bmm(fwd, jnp.concatenate([f, b], axis=1).astype(BF16)[None], out_dtype=F32)[0]
    p, q = kf[:, :D].reshape(2, T, D), kf[:, D:].reshape(2, T, D)
    first = (jnp.arange(T) == 0)[:, None]
    kspec = jnp.stack([p[0] + q[0], jnp.where(first, p[1] + q[1], p[1] - q[1])])
    s3 = s.reshape(M // T, T, D)
    sf = _bmm(fwd, s3, b_batch_off=t0 // T, n_batch=n_batch)
    y = _spec_prod(sf.reshape(n_batch, 2, T, D), kspec).reshape(n_batch, 2 * T, D)
    conv = _bmm(fwd.T, y)
    return conv.reshape(n_batch * T, D)


def _store_token_major(ref, x):
    t, d = x.shape
    nb = d // LANES
    for j in range(nb):
        ref[pl.ds(j, t, stride=nb), :] = x[:, j * LANES:(j + 1) * LANES]


def _load_token_major(ref, t, j, lead=()):
    nb = ref.shape[-2] // t
    return ref[lead + (pl.ds(j, t, stride=nb), slice(None))]


def _router_kernel(n_experts, rm_ref, x_ref, g_ref, sh_ref, sc_ref, wr_ref, br_ref,
                   h_ref, ti_ref, tg_ref, rk_ref, cnt_ref, base_scr):
    del rm_ref

    @pl.when(pl.program_id(0) == 0)
    def _():
        base_scr[...] = jnp.zeros(base_scr.shape, F32)

    h = _norm_mod_prologue([x_ref[...]], [g_ref[...]], (sh_ref[0, 0], sc_ref[0, 0]))
    _store_token_major(h_ref, h)
    logits = jnp.dot(h, wr_ref[...], precision=HIGHEST, preferred_element_type=F32) + br_ref[...]
    tm = logits.shape[0]
    lane = lax.broadcasted_iota(I32, logits.shape, 1)
    lane_f = lane.astype(F32)
    neg = jnp.float32(-jnp.inf)
    l = jnp.where(lane < n_experts, logits, neg)
    ti = jnp.zeros(logits.shape, I32)
    tl = jnp.full(logits.shape, neg, F32)
    onehot = jnp.zeros(logits.shape, F32)
    picks = []
    for r in range(TOP_K):
        m = jnp.max(l, axis=-1, keepdims=True)
        idx = jnp.min(jnp.where(l == m, lane_f, float(LANES)), axis=-1, keepdims=True).astype(I32)
        ti = jnp.where(lane == r, idx, ti)
        tl = jnp.where(lane == r, m, tl)
        l = jnp.where(lane == idx, neg, l)
        onehot = jnp.where(lane == idx, 1.0, onehot)
        picks.append(idx)
    e = jnp.exp(tl - jnp.max(tl, axis=-1, keepdims=True))
    ti_ref[...] = ti
    tg_ref[...] = e / jnp.sum(e, axis=-1, keepdims=True)

    r_i = lax.broadcasted_iota(I32, (tm, tm), 0)
    c_i = lax.broadcasted_iota(I32, (tm, tm), 1)
    earlier = jnp.where(c_i < r_i, 1.0, 0.0).astype(BF16)
    before = jnp.dot(earlier, onehot.astype(BF16), preferred_element_type=F32) + base_scr[...]
    rk = jnp.zeros(logits.shape, F32)
    for r in range(TOP_K):
        mine = jnp.sum(jnp.where(lane == picks[r], before, 0.0), axis=-1, keepdims=True)
        rk = jnp.where(lane == r, mine, rk)
    rk_ref[...] = rk.astype(I32)
    base_scr[...] += jnp.sum(onehot, axis=0, keepdims=True)
    cnt_ref[...] = base_scr[...].astype(I32)


def _router(x, g, mod4, rowmap, w_router, b_router, tm):
    M, D = x.shape
    E = w_router.shape[1]
    wr = jnp.zeros((D, LANES), F32).at[:, :E].set(w_router)
    br = jnp.zeros((1, LANES), F32).at[0, :E].set(b_router)
    row = lambda c: pl.BlockSpec((1, 1, 1, D), lambda i, rm, c=c: (rm[i], c, 0, 0))
    tile = lambda w: pl.BlockSpec((tm, w), lambda i, rm: (i, 0))
    return pl.pallas_call(
        functools.partial(_router_kernel, E),
        grid_spec=pltpu.PrefetchScalarGridSpec(
            num_scalar_prefetch=1, grid=(M // tm,),
            in_specs=[tile(D), pl.BlockSpec((1, D), lambda i, rm: (0, 0)), row(3), row(4),
                      pl.BlockSpec((D, LANES), lambda i, rm: (0, 0)),
                      pl.BlockSpec((1, LANES), lambda i, rm: (0, 0))],
            out_specs=[pl.BlockSpec((tm * D // LANES, LANES), lambda i, rm: (i, 0)),
                       tile(LANES), tile(LANES), tile(LANES),
                       pl.BlockSpec((1, LANES), lambda i, rm: (0, 0))],
            scratch_shapes=[pltpu.VMEM((1, LANES), F32)]),
        out_shape=[jax.ShapeDtypeStruct((M * D // LANES, LANES), F32), jax.ShapeDtypeStruct((M, LANES), I32),
                   jax.ShapeDtypeStruct((M, LANES), F32), jax.ShapeDtypeStruct((M, LANES), I32),
                   jax.ShapeDtypeStruct((1, LANES), I32)],
        compiler_params=_params("arbitrary"),
        name="router",
    )(rowmap, x, g.reshape(1, D), mod4, mod4, wr, br)


def _dest_kernel(ps_ref, ti_ref, rk_ref, o_ref):
    ti = ti_ref[...]
    start = jnp.zeros(ti.shape, I32)
    for e in range(ps_ref.shape[0]):
        start = jnp.where(ti == e, ps_ref[e], start)
    o_ref[...] = start + rk_ref[...]


def _dest_rows(seg_start, ti, rk, tm):
    M = ti.shape[0]
    tile = pl.BlockSpec((tm, LANES), lambda i, ps: (i, 0))
    return pl.pallas_call(
        _dest_kernel,
        grid_spec=pltpu.PrefetchScalarGridSpec(num_scalar_prefetch=1, grid=(M // tm,),
                                               in_specs=[tile, tile], out_specs=tile),
        out_shape=jax.ShapeDtypeStruct((M, LANES), I32),
        compiler_params=_params("parallel"),
        name="moe_dest",
    )(seg_start, ti, rk)


def _load_tile_dest(dest_ref, dest_smem, sem):
    cp = pltpu.make_async_copy(dest_ref, dest_smem, sem)
    cp.start()
    cp.wait()


def _zero_pad_rows(nb, pad_start_ref, pad_len_ref, o_ref, zero_ref, sem, start):
    def copy(pos, p):
        cp = pltpu.make_async_copy(zero_ref.at[pl.ds(0, p * nb)],
                                   o_ref.at[pl.ds(pl.multiple_of(pos * nb, nb), p * nb)], sem)
        if start:
            cp.start()
        else:
            cp.wait()

    def per_expert(e, carry):
        pos, length = pad_start_ref[e], pad_len_ref[e]
        p = zero_ref.shape[0] // nb // 2
        while p >= 1:
            hit = jnp.bitwise_and(length, p) != 0

            @pl.when(hit)
            def _(pos=pos, p=p):
                copy(pos, p)

            pos = pos + jnp.where(hit, p, 0)
            p //= 2
        return carry

    lax.fori_loop(0, pad_start_ref.shape[0], per_expert, 0)


def _dispatch_kernel(nb, pad_start_ref, pad_len_ref, tail_ref, dest_ref, h_ref, o_ref, dest_smem, zero_ref, sem):
    tm = h_ref.shape[0] // nb
    tz = zero_ref.shape[0]

    @pl.when(pl.program_id(0) == 0)
    def _():
        zero_ref[...] = jnp.zeros(zero_ref.shape, F32)

        def tail_copy(t):
            return pltpu.make_async_copy(zero_ref, o_ref.at[pl.ds(pl.multiple_of(t * tz, tz), tz)], sem.at[2])

        def tail_start(t, carry):
            tail_copy(t).start()
            return carry

        def tail_wait(t, carry):
            tail_copy(t).wait()
            return carry

        _zero_pad_rows(nb, pad_start_ref, pad_len_ref, o_ref, zero_ref, sem.at[2], True)
        lax.fori_loop(tail_ref[0], tail_ref[1], tail_start, 0)
        _zero_pad_rows(nb, pad_start_ref, pad_len_ref, o_ref, zero_ref, sem.at[2], False)
        lax.fori_loop(tail_ref[0], tail_ref[1], tail_wait, 0)

    _load_tile_dest(dest_ref, dest_smem, sem.at[0])

    def row_copy(j):
        dst = dest_smem[jnp.right_shift(j, LOG2_LANES), jnp.bitwise_and(j, LANES - 1)]
        return pltpu.make_async_copy(h_ref.at[pl.ds(pl.multiple_of(jnp.right_shift(j, LOG2_TOP_K) * nb, nb), nb)],
                                     o_ref.at[pl.ds(pl.multiple_of(dst * nb, nb), nb)], sem.at[1])

    def start(j, carry):
        row_copy(j).start()
        return carry

    def wait(j, carry):
        row_copy(j).wait()
        return carry

    lax.fori_loop(0, tm * TOP_K, start, 0, unroll=8)
    lax.fori_loop(0, tm * TOP_K, wait, 0, unroll=8)


def _dispatch(h, nb, dest2d, pad_start, pad_len, tail, n_rows, tm, tm_rows):
    M = h.shape[0] // nb
    nd = tm * TOP_K // LANES
    return pl.pallas_call(
        functools.partial(_dispatch_kernel, nb),
        grid_spec=pltpu.PrefetchScalarGridSpec(
            num_scalar_prefetch=3, grid=(M // tm,),
            in_specs=[pl.BlockSpec((nd, LANES), lambda i, ps, pn, tl: (i, 0)),
                      pl.BlockSpec((tm * nb, LANES), lambda i, ps, pn, tl: (i, 0))],
            out_specs=pl.BlockSpec(memory_space=pl.ANY),
            scratch_shapes=[pltpu.SMEM((nd, LANES), I32), pltpu.VMEM((tm_rows * nb, LANES), F32),
                            pltpu.SemaphoreType.DMA((3,))]),
        out_shape=jax.ShapeDtypeStruct((n_rows * nb, LANES), F32),
        compiler_params=_params("arbitrary"),
        name="moe_dispatch",
    )(pad_start, pad_len, tail, dest2d, h)


def _combine_kernel(final, rm_ref, dest_ref, y_ref, tg_ref, x_ref, gate_ref, ng_ref, *refs):
    del rm_ref
    n_out = 1 if final is None else 2
    o_ref = refs[0] if final is None else refs[:2]
    buf, dest_smem, sem = refs[n_out:]
    tm = x_ref.shape[0]
    _load_tile_dest(dest_ref, dest_smem, sem.at[0])

    nb = x_ref.shape[1] // LANES

    def row_copy(j):
        src = dest_smem[jnp.right_shift(j, LOG2_LANES), jnp.bitwise_and(j, LANES - 1)]
        return pltpu.make_async_copy(
            y_ref.at[pl.ds(pl.multiple_of(src * nb, nb), nb)],
            buf.at[jnp.bitwise_and(j, TOP_K - 1), pl.ds(pl.multiple_of(jnp.right_shift(j, LOG2_TOP_K) * nb, nb), nb)],
            sem.at[1])

    def start(j, carry):
        row_copy(j).start()
        return carry

    def wait(j, carry):
        row_copy(j).wait()
        return carry

    lax.fori_loop(0, tm * TOP_K, start, 0, unroll=8)
    lax.fori_loop(0, tm * TOP_K, wait, 0, unroll=8)
    tg = tg_ref[...]
    blocks = []
    for j in range(nb):
        acc = tg[:, 0:1] * _load_token_major(buf, tm, j, (0,))
        for k in range(1, TOP_K):
            acc = acc + tg[:, k:k + 1] * _load_token_major(buf, tm, j, (k,))
        blocks.append(acc)
    x = x_ref[...] + gate_ref[0, 0] * jnp.concatenate(blocks, axis=1)
    if final is None:
        o_ref[...] = x
    else:
        x = x * lax.rsqrt(jnp.mean(x * x, axis=-1, keepdims=True) + RMS_EPS) * ng_ref[...]
        first_ref, second_ref = o_ref

        @pl.when(pl.program_id(0) < final)
        def _():
            first_ref[...] = x

        @pl.when(pl.program_id(0) >= final)
        def _():
            second_ref[...] = x


def _combine(y_rows, dest2d, tg, x, mod4, rowmap, norm_g, split_rows, tm):
    M, D = x.shape
    nd = tm * TOP_K // LANES
    if split_rows is None:
        final = None
        out_specs = pl.BlockSpec((tm, D), lambda i, rm: (i, 0))
        out_shape = jax.ShapeDtypeStruct((M, D), F32)
    else:
        final = split_rows // tm
        out_specs = [pl.BlockSpec((tm, D), lambda i, rm: (jnp.minimum(i, final - 1), 0)),
                     pl.BlockSpec((tm, D), lambda i, rm: (jnp.maximum(i - final, 0), 0))]
        out_shape = [jax.ShapeDtypeStruct((split_rows, D), F32), jax.ShapeDtypeStruct((M - split_rows, D), F32)]
    return _combine_call(final, out_specs, out_shape, y_rows, dest2d, tg, x, mod4, rowmap, norm_g, tm, nd)


def _combine_call(final, out_specs, out_shape, y_rows, dest2d, tg, x, mod4, rowmap, norm_g, tm, nd):
    M, D = x.shape
    return pl.pallas_call(
        functools.partial(_combine_kernel, final),
        grid_spec=pltpu.PrefetchScalarGridSpec(
            num_scalar_prefetch=1, grid=(M // tm,),
            in_specs=[pl.BlockSpec((nd, LANES), lambda i, rm: (i, 0)),
                      pl.BlockSpec(memory_space=pl.ANY),
                      pl.BlockSpec((tm, LANES), lambda i, rm: (i, 0)),
                      pl.BlockSpec((tm, D), lambda i, rm: (i, 0)),
                      pl.BlockSpec((1, 1, 1, D), lambda i, rm: (rm[i], 5, 0, 0)),
                      pl.BlockSpec((1, D), lambda i, rm: (0, 0))],
            out_specs=out_specs,
            scratch_shapes=[pltpu.VMEM((TOP_K, tm * D // LANES, LANES), F32), pltpu.SMEM((nd, LANES), I32),
                            pltpu.SemaphoreType.DMA((2,))]),
        out_shape=out_shape,
        compiler_params=_params("arbitrary"),
        name="moe_combine",
    )(rowmap, dest2d, y_rows, tg, x, mod4, norm_g.reshape(1, D))


def _moe_kernel(nf, te_ref, nv_ref, x_ref, wg_ref, wu_ref, bg_ref, bu_ref, wo_ref, bo_ref, o_ref,
                acc_ref, xb_ref):
    del te_ref
    t, f = pl.program_id(0), pl.program_id(1)
    last = nf - 1
    valid = t < nv_ref[0]

    tm, d = xb_ref.shape
    nb = d // LANES

    @pl.when(valid & (f == 0))
    def _():
        for j in range(nb):
            xb_ref[:, j * LANES:(j + 1) * LANES] = _load_token_major(x_ref, tm, j).astype(BF16)

    @pl.when(valid)
    def _():
        x = xb_ref[...]
        g = jnp.dot(x, wg_ref[0], preferred_element_type=F32) + bg_ref[0]
        u = jnp.dot(x, wu_ref[0], preferred_element_type=F32) + bu_ref[0]
        g = jnp.minimum(g, SWIGLU_LIMIT)
        u = jnp.clip(u, -SWIGLU_LIMIT, SWIGLU_LIMIT)
        h = ((u + 1.0) * g * jax.nn.sigmoid(SWIGLU_ALPHA * g)).astype(BF16)
        part = jnp.dot(h, wo_ref[0], preferred_element_type=F32)

        if nf == 1:
            _store_token_major(o_ref, part + bo_ref[0])
        else:
            @pl.when(f == 0)
            def _():
                acc_ref[...] = part + bo_ref[0]

            @pl.when((f > 0) & (f < last))
            def _():
                acc_ref[...] += part

            @pl.when(f == last)
            def _():
                _store_token_major(o_ref, acc_ref[...] + part)

    @pl.when(jnp.logical_not(valid) & (f == last))
    def _():
        o_ref[...] = jnp.zeros(o_ref.shape, o_ref.dtype)


def _moe_experts(x_rows, tile_expert, n_valid, w_in, b_in, w_out, b_out, tm, tf):
    E, D, F2 = w_in.shape
    nb = D // LANES
    R = x_rows.shape[0] // nb
    F = F2 // 2
    tf = _tile(F, tf)
    nf = F // tf

    def fi(t, f, nv):
        return jnp.where(t < nv[0], f, nf - 1)

    def ti(t, nv):
        return jnp.minimum(t, jnp.maximum(nv[0] - 1, 0))

    return pl.pallas_call(
        functools.partial(_moe_kernel, nf),
        grid_spec=pltpu.PrefetchScalarGridSpec(
            num_scalar_prefetch=2, grid=(R // tm, nf),
            in_specs=[pl.BlockSpec((tm * nb, LANES), lambda t, f, te, nv: (ti(t, nv), 0)),
                      pl.BlockSpec((1, D, tf), lambda t, f, te, nv: (te[t], 0, fi(t, f, nv))),
                      pl.BlockSpec((1, D, tf), lambda t, f, te, nv: (te[t], 0, nf + fi(t, f, nv))),
                      pl.BlockSpec((1, 1, tf), lambda t, f, te, nv: (te[t], 0, fi(t, f, nv))),
                      pl.BlockSpec((1, 1, tf), lambda t, f, te, nv: (te[t], 0, nf + fi(t, f, nv))),
                      pl.BlockSpec((1, tf, D), lambda t, f, te, nv: (te[t], fi(t, f, nv), 0)),
                      pl.BlockSpec((1, 1, D), lambda t, f, te, nv: (te[t], 0, 0))],
            out_specs=pl.BlockSpec((tm * nb, LANES), lambda t, f, te, nv: (t, 0)),
            scratch_shapes=[pltpu.VMEM((tm, D), F32), pltpu.VMEM((tm, D), BF16)]),
        out_shape=jax.ShapeDtypeStruct((R * nb, LANES), F32),
        compiler_params=_params("parallel", "arbitrary"),
        name="moe_experts",
    )(tile_expert, n_valid, x_rows, w_in, w_in, b_in.reshape(E, 1, F2), b_in.reshape(E, 1, F2),
      w_out, b_out.reshape(E, 1, D))


def _moe_layer(x, norm_g, mod4, rowmap, tm_tok, w_router, b_router, w_in, b_in, w_out, b_out, first_expert,
               final_g=None, split_rows=None, tm=512, tf=1024):
    M, D = x.shape
    E = w_router.shape[1]
    h, ti, tg, rk, cnt = _router(x, norm_g, mod4, rowmap, w_router, b_router, tm_tok)
    A = M * TOP_K
    n_tiles = -(-A // tm) + E
    counts = cnt[0, :E]
    ptiles = (counts + tm - 1) // tm
    pend = jnp.cumsum(ptiles)
    pstart = pend - ptiles
    n_valid = pend[-1]
    tix = jnp.minimum(jnp.arange(n_tiles, dtype=I32), jnp.maximum(n_valid - 1, 0))
    tile_expert = jnp.minimum(jnp.searchsorted(pend, tix, side='right'), E - 1).astype(I32)
    dest = _dest_rows((pstart * tm).astype(I32), ti, rk, tm_tok)
    dest2d = dest[:, :TOP_K].reshape(A // LANES, LANES)
    tail = jnp.stack([n_valid, jnp.asarray(n_tiles, n_valid.dtype)]).astype(I32)
    x_rows = _dispatch(h, D // LANES, dest2d, (pstart * tm + counts).astype(I32), (ptiles * tm - counts).astype(I32),
                       tail, n_tiles * tm, tm_tok, tm)
    y_rows = _moe_experts(x_rows, tile_expert + first_expert, n_valid.reshape(1).astype(I32),
                          w_in, b_in, w_out, b_out, tm, tf)
    return _combine(y_rows, dest2d, tg, x, mod4, rowmap, norm_g if final_g is None else final_g,
                    split_rows if final_g is not None else None, tm_tok)


def _grid_pos_embedding(T, D):
    rows = T // GRID_W
    row = jnp.repeat(jnp.arange(rows), GRID_W)
    col = jnp.tile(jnp.arange(GRID_W), rows)
    quarter = D // 4
    omega = 1.0 / (POS_BASE ** (jnp.arange(quarter, dtype=F32) / quarter))

    def axis_emb(p):
        ang = p.astype(F32)[:, None] * omega[None]
        return jnp.concatenate([jnp.sin(ang), jnp.cos(ang)], axis=-1)

    return jnp.concatenate([axis_emb(row), axis_emb(col)], axis=-1)


def kernel(x_prompt, x_sample, state_delta, c, c_ctx, w_mod, b_mod, norm_mix, norm_ffn, norm_final, dn_w_in, dn_conv, dn_a_log, dn_dt_bias, dn_norm, dn_w_out, hy_w_in, hy_conv, hy_w1, hy_b1, hy_w2, hy_b2, hy_w3, hy_b3, hy_w4, hy_freq, hy_bias, hy_w_out, moe_w_router, moe_b_router, moe_w_in, moe_b_in, moe_w_out, moe_b_out):
    Bp, Tp, D = x_prompt.shape
    Bs, Ts, _ = x_sample.shape
    depth = w_mod.shape[0]
    H = state_delta.shape[3]
    Mp, Ms = Bp * Tp, Bs * Ts
    M = Mp + Ms
    assert Mp % Ts == 0 and Ts % Tp == 0 and Tp % DN_CHUNK == 0, "token groups must tile each other"
    tT = Tp
    tm_big = _tile(math.gcd(Mp, Ts), 1024)
    tm_mid = _tile(math.gcd(Mp, Ts), 512)
    tm_tok = min(256, Tp)

    def rowmap_for(tm):
        tile_start = np.arange(M // tm) * tm
        return jnp.asarray(np.where(tile_start < Mp, 0, 1 + (tile_start - Mp) // Ts), I32)

    rm_big, rm_mid, rm_tok = rowmap_for(tm_big), rowmap_for(tm_mid), rowmap_for(tm_tok)

    xs = x_sample + _grid_pos_embedding(Ts, D)[None]
    x = jnp.concatenate([x_prompt.reshape(Mp, D), xs.reshape(Ms, D)], axis=0)
    conv_start = np.arange(M // tT) * tT
    seq_len = np.where(conv_start < Mp, Tp, Ts)
    seq_pos = np.where(conv_start < Mp, conv_start % Tp, (conv_start - Mp) % Ts)
    has_prev = jnp.asarray(seq_pos > 0, I32)
    has_next = jnp.asarray(seq_pos + tT < seq_len, I32)

    n_cond = 1 + Bs
    r_pad = -(-n_cond // 8) * 8
    cond = jnp.zeros((r_pad, D), F32).at[0].set(c_ctx).at[1:n_cond].set(c)
    mod = _modulation(cond, w_mod, b_mod)

    n_experts = moe_w_in.shape[1]
    w_in_all = moe_w_in.astype(BF16).reshape((depth * n_experts,) + moe_w_in.shape[2:])
    w_out_all = moe_w_out.astype(BF16).reshape((depth * n_experts,) + moe_w_out.shape[2:])
    b_in_all = moe_b_in.reshape(depth * n_experts, -1)
    b_out_all = moe_b_out.reshape(depth * n_experts, -1)

    new_states = []
    i_dn = i_hy = 0
    for l in range(depth):
        mod4 = mod[l].reshape(r_pad, 6, 1, D)
        if l % 2 == 0:
            i = i_dn
            i_dn += 1
            w_in = dn_w_in[i]
            n_main = 4 * H * HEAD_DIM
            w_ba = jnp.zeros((D, LANES), F32).at[:, :4 * H].set(w_in[:, n_main:]).astype(BF16)
            pm, ba = _fused_mm(_norm_mod_prologue, [x], [norm_mix[l]], [0, 1], w_in[:, :n_main].astype(BF16),
                               rm_big, mod4=mod4, side_w=w_ba, tm=tm_big, tn=2048)
            qkv, bg = _dn_act(pm, ba, dn_conv[i], dn_a_log[i], dn_dt_bias[i], has_prev, has_next, tT)
            hb = 2 if H % 2 == 0 else 1
            cols_p, gct_p = _delta_side_inputs(bg, 0, Bp, Tp, H, hb)
            cols_s, gct_s = _delta_side_inputs(bg, Mp, Bs, Ts, H, hb)
            og_p, s_fin = _delta(qkv, pm, cols_p, gct_p, dn_norm[i], None,
                                 n_batch=Bp, batch_off=0, T=Tp, hb=hb)
            og_s, _ = _delta(qkv, pm, cols_s, gct_s, dn_norm[i], state_delta[:, i].astype(F32),
                             n_batch=Bs, batch_off=Mp // Ts, T=Ts, hb=hb)
            new_states.append(s_fin.astype(state_delta.dtype))
            x = _fused_mm(functools.partial(_two_part_prologue, Mp // tm_big), [og_p], [], [],
                          dn_w_out[i].astype(BF16), rm_big, mod4=mod4, a_rest=og_s,
                          res=x, gate_chunk=2, out_dtype=F32, tm=tm_big, tn=1024)
        else:
            i = i_hy
            i_hy += 1
            u = _fused_mm(_norm_mod_prologue, [x], [norm_mix[l]], [0, 1], hy_w_in[i].astype(BF16),
                          rm_big, mod4=mod4, tm=tm_big, tn=2048)
            x0, s = _hy_pre(u, hy_conv[i], has_prev, has_next, tT)
            hy = (hy_w1[i], hy_b1[i], hy_w2[i], hy_b2[i], hy_w3[i], hy_b3[i], hy_w4[i], hy_freq[i])
            conv_p = _hyena_conv(s, Bp, 0, Tp, _hyena_filter_taps(Tp, *hy))
            conv_s = _hyena_conv(s, Bs, Mp, Ts, _hyena_filter_taps(Ts, *hy))
            x = _fused_mm(functools.partial(_hyena_gate_prologue, Mp // tm_mid), [conv_p, x0, s], [hy_bias[i]], [],
                          hy_w_out[i].astype(BF16), rm_mid, mod4=mod4, a_rest=conv_s,
                          res=x, gate_chunk=2, out_dtype=F32, tm=tm_mid, tn=1024)
        last = l == depth - 1
        x = _moe_layer(x, norm_ffn[l], mod4, rm_tok, tm_tok, moe_w_router[l], moe_b_router[l],
                       w_in_all, b_in_all, w_out_all, b_out_all, l * n_experts,
                       final_g=norm_final if last else None, split_rows=Mp)

    y_prompt, y_sample = x
    return y_prompt.reshape(Bp, Tp, D), y_sample.reshape(Bs, Ts, D), jnp.stack(new_states, axis=1)
```

```python
import functools
import math

import numpy as np
import jax
import jax.numpy as jnp
from jax import lax
from jax.experimental import pallas as pl
from jax.experimental.pallas import tpu as pltpu

F32, BF16, I32 = jnp.float32, jnp.bfloat16, jnp.int32

V7X_VMEM_LIMIT_BYTES = 56 * 2**20
LANES = 128
BF16_ROWS = 16

GRID_W = 64
DN_CHUNK = 128
HEAD_DIM = 128
TOP_K = 4
RMS_EPS = 1e-6
L2_EPS = 1e-6
POS_BASE = 10000.0
SWIGLU_LIMIT = 7.0
SWIGLU_ALPHA = 1.702
HY_TARGET = 1e-2
HY_FAST = 0.3
HY_SLOW = 1.5

DMA_BURST = 8
LOG2_LANES = LANES.bit_length() - 1
LOG2_TOP_K = TOP_K.bit_length() - 1
assert 1 << LOG2_LANES == LANES and 1 << LOG2_TOP_K == TOP_K

HIGHEST = lax.Precision.HIGHEST
NT_DIMS = (((1,), (1,)), ((), ()))
TN_DIMS = (((0,), (0,)), ((), ()))


def _params(*sem):
    return pltpu.CompilerParams(dimension_semantics=sem, vmem_limit_bytes=V7X_VMEM_LIMIT_BYTES)


def _tile(n, pref):
    if n <= pref:
        return n
    t = pref - pref % LANES
    while n % t:
        t -= LANES
    return t


def _silu(x):
    return x * jax.nn.sigmoid(x)


def _softplus(x):
    return jnp.maximum(x, 0.0) + jnp.log1p(jnp.exp(-jnp.abs(x)))


def _split_bf16(x):
    hi = x.astype(BF16)
    lo = (x - hi.astype(F32)).astype(BF16)
    return hi, lo


def _dot3(a, b):
    ah, al = _split_bf16(a)
    bh, bl = _split_bf16(b)
    return (jnp.dot(ah, bh, preferred_element_type=F32)
            + (jnp.dot(ah, bl, preferred_element_type=F32) + jnp.dot(al, bh, preferred_element_type=F32)))


def _modulation_kernel(c_ref, w_ref, b_ref, o_ref):
    s = _silu(c_ref[...]).astype(BF16)
    o_ref[0] = jnp.dot(s, w_ref[0].astype(BF16), preferred_element_type=F32) + b_ref[0]


def _modulation(cond, w_mod, b_mod):
    R, D = cond.shape
    L, _, N = w_mod.shape
    tn = _tile(N, 1024)
    return pl.pallas_call(
        _modulation_kernel,
        grid=(L, N // tn),
        in_specs=[pl.BlockSpec((R, D), lambda l, j: (0, 0)),
                  pl.BlockSpec((1, D, tn), lambda l, j: (l, 0, j)),
                  pl.BlockSpec((1, 1, tn), lambda l, j: (l, 0, j))],
        out_specs=pl.BlockSpec((1, R, tn), lambda l, j: (l, 0, j)),
        out_shape=jax.ShapeDtypeStruct((L, R, N), F32),
        compiler_params=_params("parallel", "parallel"),
        name="modulation",
    )(cond, w_mod, b_mod.reshape(L, 1, N))


def _fused_mm_kernel(prologue, n_a, n_v, n_m, has_res, has_side, rm_ref, *refs):
    del rm_ref
    a_refs, refs = refs[:n_a], refs[n_a:]
    v_refs, refs = refs[:n_v], refs[n_v:]
    m_refs, refs = refs[:n_m], refs[n_m:]
    w_ref, refs = refs[0], refs[1:]
    if has_res:
        res_ref, gate_ref, refs = refs[0], refs[1], refs[2:]
    if has_side:
        sw_ref, refs = refs[0], refs[1:]
    o_ref, refs = refs[0], refs[1:]
    if has_side:
        so_ref, refs = refs[0], refs[1:]
    a_scr = refs[0]

    row_tile = pl.program_id(0)

    @pl.when(pl.program_id(1) == 0)
    def _():
        a = prologue([r[...] for r in a_refs], [r[...] for r in v_refs], [r[0, 0] for r in m_refs], row_tile)
        a_scr[...] = a.astype(BF16)
        if has_side:
            so_ref[...] = jnp.dot(a_scr[...], sw_ref[...], preferred_element_type=F32)

    acc = jnp.dot(a_scr[...], w_ref[...], preferred_element_type=F32)
    if has_res:
        acc = res_ref[...] + gate_ref[0, 0] * acc
    o_ref[...] = acc.astype(o_ref.dtype)


def _fused_mm(prologue, a_ins, vec_ins, mod_ins, w, rowmap, *, mod4=None, res=None, gate_chunk=None,
              side_w=None, a_rest=None, out_dtype=BF16, tm, tn):
    a_split = None if a_rest is None else a_ins[0].shape[0]
    K = a_ins[0].shape[1]
    M = rowmap.shape[0] * tm
    N = w.shape[1]
    tn = _tile(N, tn)
    in_specs, args = [], []
    for a in a_ins:
        in_specs.append(pl.BlockSpec((tm, K), lambda i, j, rm: (i, 0)))
        args.append(a)
    if a_split is not None:
        n_first = a_split // tm
        in_specs[0] = pl.BlockSpec((tm, K), lambda i, j, rm: (jnp.minimum(i, n_first - 1), 0))
        in_specs.insert(1, pl.BlockSpec((tm, K), lambda i, j, rm: (jnp.maximum(i - n_first, 0), 0)))
        args.insert(1, a_rest)
    for v in vec_ins:
        in_specs.append(pl.BlockSpec((1, K), lambda i, j, rm: (0, 0)))
        args.append(v.reshape(1, K))
    for c in mod_ins:
        in_specs.append(pl.BlockSpec((1, 1, 1, K), lambda i, j, rm, c=c: (rm[i], c, 0, 0)))
        args.append(mod4)
    in_specs.append(pl.BlockSpec((K, tn), lambda i, j, rm: (0, j)))
    args.append(w)
    if res is not None:
        in_specs.append(pl.BlockSpec((tm, tn), lambda i, j, rm: (i, j)))
        args.append(res)
        in_specs.append(pl.BlockSpec((1, 1, 1, tn), lambda i, j, rm, c=gate_chunk: (rm[i], c, 0, j)))
        args.append(mod4)
    out_specs = [pl.BlockSpec((tm, tn), lambda i, j, rm: (i, j))]
    out_shape = [jax.ShapeDtypeStruct((M, N), out_dtype)]
    if side_w is not None:
        ns = side_w.shape[1]
        in_specs.append(pl.BlockSpec((K, ns), lambda i, j, rm: (0, 0)))
        args.append(side_w)
        out_specs.append(pl.BlockSpec((tm, ns), lambda i, j, rm: (i, 0)))
        out_shape.append(jax.ShapeDtypeStruct((M, ns), F32))
    kern = functools.partial(_fused_mm_kernel, prologue, len(a_ins) + (a_rest is not None), len(vec_ins),
                             len(mod_ins), res is not None, side_w is not None)
    outs = pl.pallas_call(
        kern,
        grid_spec=pltpu.PrefetchScalarGridSpec(
            num_scalar_prefetch=1, grid=(M // tm, N // tn), in_specs=in_specs, out_specs=out_specs,
            scratch_shapes=[pltpu.VMEM((tm, K), BF16)]),
        out_shape=out_shape,
        compiler_params=_params("parallel", "arbitrary"),
        name="fused_mm",
    )(rowmap, *args)
    return outs if side_w is not None else outs[0]


def _norm_mod_prologue(a, v, m, row_tile=None):
    x, g, (shift, scale) = a[0], v[0], m
    y = x * lax.rsqrt(jnp.mean(x * x, axis=-1, keepdims=True) + RMS_EPS) * g
    return y * (1.0 + scale) + shift


def _two_part_prologue(n_first_tiles, a, v, m, row_tile):
    return jnp.where(row_tile < n_first_tiles, a[0], a[1])


def _hyena_gate_prologue(n_first_tiles, a, v, m, row_tile):
    conv = jnp.where(row_tile < n_first_tiles, a[0], a[1]).astype(F32)
    x0, s = a[2].astype(F32), a[3].astype(F32)
    return x0 * (conv + s * v[0])


def _conv3(x, prev_row, next_row, w, rid):
    t = x.shape[0]
    xm = jnp.where(rid == 0, prev_row, pltpu.roll(x, 1, 0))
    xp = jnp.where(rid == t - 1, next_row, pltpu.roll(x, t - 1, 0))
    return xm * w[0:1] + x * w[1:2] + xp * w[2:3]


def _halo_specs(tT, width, col_block):
    hb = tT // BF16_ROWS
    return [
        pl.BlockSpec((tT, width), lambda i, hp, hn: (i, col_block)),
        pl.BlockSpec((BF16_ROWS, width), lambda i, hp, hn: (jnp.maximum(i * hb - 1, 0), col_block)),
        pl.BlockSpec((BF16_ROWS, width), lambda i, hp, hn: ((i + 1) * hb * hn[i], col_block)),
    ]


def _dn_act_kernel(n_heads, hp_ref, hn_ref, x_ref, xp_ref, xn_ref, ba_ref, cw_ref, al_ref, dtb_ref,
                   qkv_ref, bg_ref):
    i = pl.program_id(0)
    hp = hp_ref[i].astype(F32)
    hn = hn_ref[i].astype(F32)
    tT = x_ref.shape[0]
    rid = lax.broadcasted_iota(I32, (tT, LANES), 0)
    for cb in range(3 * n_heads):
        sl = slice(cb * LANES, (cb + 1) * LANES)
        x = x_ref[:, sl].astype(F32)
        pr = xp_ref[:, sl].astype(F32)[BF16_ROWS - 1:BF16_ROWS] * hp
        nx = xn_ref[:, sl].astype(F32)[0:1] * hn
        y = _silu(_conv3(x, pr, nx, cw_ref[:, sl], rid))
        if cb < 2 * n_heads:
            y = y * lax.rsqrt(jnp.sum(y * y, axis=-1, keepdims=True) + L2_EPS)
            if cb < n_heads:
                y = y * (HEAD_DIM ** -0.5)
        qkv_ref[:, sl] = y.astype(BF16)

    ba = ba_ref[...]
    lane = lax.broadcasted_iota(I32, (tT, LANES), 1)
    beta = jax.nn.sigmoid(ba)
    g = -jnp.exp(al_ref[...]) * _softplus(ba + dtb_ref[...])
    r = lax.broadcasted_iota(I32, (tT, tT), 0)
    c = lax.broadcasted_iota(I32, (tT, tT), 1)
    shift = DN_CHUNK.bit_length() - 1
    same = jnp.right_shift(r, shift) == jnp.right_shift(c, shift)
    l_pre = jnp.where(same & (c <= r), 1.0, 0.0).astype(F32)
    l_suf = jnp.where(same & (c >= r), 1.0, 0.0).astype(F32)
    g_pre = jnp.dot(l_pre, g, precision=HIGHEST, preferred_element_type=F32)
    g_suf = jnp.dot(l_suf, g, precision=HIGHEST, preferred_element_type=F32)
    gc = jnp.where(lane < 3 * n_heads, g_pre, g_suf)
    bg_ref[...] = jnp.where(lane < 2 * n_heads, beta, gc)


def _dn_act(pm, ba, conv_w, a_log, dt_bias, has_prev, has_next, tT):
    M = pm.shape[0]
    cw = conv_w.shape[1]
    n_heads = cw // (3 * HEAD_DIM)
    pad = lambda v: jnp.zeros((1, LANES), F32).at[0, 2 * n_heads:4 * n_heads].set(v.reshape(-1).astype(F32))
    vec = pl.BlockSpec((1, LANES), lambda i, hp, hn: (0, 0))
    return pl.pallas_call(
        functools.partial(_dn_act_kernel, n_heads),
        grid_spec=pltpu.PrefetchScalarGridSpec(
            num_scalar_prefetch=2, grid=(M // tT,),
            in_specs=_halo_specs(tT, cw, 0) + [
                pl.BlockSpec((tT, LANES), lambda i, hp, hn: (i, 0)),
                pl.BlockSpec((3, cw), lambda i, hp, hn: (0, 0)), vec, vec],
            out_specs=[pl.BlockSpec((tT, cw), lambda i, hp, hn: (i, 0)),
                       pl.BlockSpec((tT, LANES), lambda i, hp, hn: (i, 0))]),
        out_shape=[jax.ShapeDtypeStruct((M, cw), BF16), jax.ShapeDtypeStruct((M, LANES), F32)],
        compiler_params=_params("parallel"),
        name="dn_act",
    )(has_prev, has_next, pm, pm, pm, ba, conv_w, pad(a_log), pad(dt_bias))


def _merge_masks(ri, ci, n, lower):
    masks = []
    s = 0
    while (1 << s) < n:
        same = jnp.right_shift(ri, s + 1) == jnp.right_shift(ci, s + 1)
        hi_r = jnp.bitwise_and(jnp.right_shift(ri, s), 1)
        hi_c = jnp.bitwise_and(jnp.right_shift(ci, s), 1)
        off = (hi_r == 1) & (hi_c == 0) if lower else (hi_r == 0) & (hi_c == 1)
        masks.append(same & off)
        s += 1
    return masks


def _dot3_many(xs, ys):
    sx = [_split_bf16(x) for x in xs]
    sy = [_split_bf16(y) for y in ys]
    hh = [jnp.dot(x[0], y[0], preferred_element_type=F32) for x, y in zip(sx, sy)]
    hl = [jnp.dot(x[0], y[1], preferred_element_type=F32) for x, y in zip(sx, sy)]
    lh = [jnp.dot(x[1], y[0], preferred_element_type=F32) for x, y in zip(sx, sy)]
    return [a + (b + c) for a, b, c in zip(hh, hl, lh)]


def _dot1_many(xs, ys):
    return [jnp.dot(x.astype(BF16), y.astype(BF16), preferred_element_type=F32) for x, y in zip(xs, ys)]


def _unit_tri_inverse_many(mats, eye, masks, stricts):
    ts = [eye - jnp.where(m[0], a, 0.0) for a, m in zip(mats, masks)]
    for lvl in range(1, len(masks[0])):
        off = [jnp.where(m[lvl], a, 0.0) for a, m in zip(mats, masks)]
        upd = _dot1_many(_dot1_many(ts, off), ts)
        ts = [t - u for t, u in zip(ts, upd)]
    full = [eye + jnp.where(st, a, 0.0) for a, st in zip(mats, stricts)]
    res = [eye - p for p in _dot3_many(full, ts)]
    return [t + c for t, c in zip(ts, _dot1_many(ts, res))]


def _delta_kernel(hb, n_chunks, group, has_s0, *refs):
    if has_s0:
        q_ref, k_ref, v_ref, z_ref, cols_ref, gct_ref, ng_ref, s0_ref, o_ref, sfin_ref = refs[:10]
        scr = refs[10:]
    else:
        q_ref, k_ref, v_ref, z_ref, cols_ref, gct_ref, ng_ref, o_ref, sfin_ref = refs[:9]
        scr = refs[9:]
    u_scr, wq_scr, qk_scr, s_scr, o_scr = scr
    C = DN_CHUNK
    n = n_chunks
    ri = lax.broadcasted_iota(I32, (C, C), 0)
    ci = lax.broadcasted_iota(I32, (C, C), 1)
    eye = (ri == ci).astype(F32)
    incl = (ri >= ci, ri <= ci)

    def gate_cols(hh, d, rows):
        bcol = cols_ref[0, 0, rows, hh * 4 + d:hh * 4 + d + 1]
        gcol = cols_ref[0, 0, rows, hh * 4 + 2 + d:hh * 4 + 3 + d]
        gl = gcol[C - 1:C, :] if d == 0 else gcol[0:1, :]
        return bcol, gcol, gl

    def prep(it, carry):
        tri_masks = (_merge_masks(ri, ci, C, True), _merge_masks(ri, ci, C, False))
        tri_strict = (ri > ci, ri < ci)
        where_, mats, masks, stricts, rhss = [], [], [], [], []
        for gi in range(group):
            c = it * group + gi
            rows = pl.ds(pl.multiple_of(c * C, C), C)
            for hh in range(hb):
                ls = slice(hh * HEAD_DIM, (hh + 1) * HEAD_DIM)
                qb, kb16, vb = q_ref[0, rows, ls], k_ref[0, rows, ls], v_ref[0, rows, ls]
                qf, kf, vf = qb.astype(F32), kb16.astype(F32), vb.astype(F32)
                qkt = lax.dot_general(qb, kb16, NT_DIMS, preferred_element_type=F32)
                for d in range(2):
                    ch = hh * 2 + d
                    bcol, gcol, _ = gate_cols(hh, d, rows)
                    grow = gct_ref[0, hh, d, pl.ds(c, 1), :]
                    dm = jnp.where(incl[d], jnp.exp(jnp.where(incl[d], gcol - grow, 0.0)), 0.0)
                    kbeta = kf * bcol
                    eg = jnp.exp(gcol)
                    mats.append(lax.dot_general(kbeta.astype(BF16), kb16, NT_DIMS, preferred_element_type=F32) * dm)
                    masks.append(tri_masks[d])
                    stricts.append(tri_strict[d])
                    rhss.append(jnp.concatenate([vf * bcol, kbeta * eg], axis=1))
                    where_.append((ch, c, rows))
                    wq_scr[ch, pl.ds(pl.multiple_of(c * 2 * C + C, C), C), :] = (qf * eg).astype(BF16)
                    qk_scr[ch, rows, :] = (qkt * dm).astype(BF16)
        sols = _dot3_many(_unit_tri_inverse_many(mats, eye, masks, stricts), rhss)
        for (ch, c, rows), sol in zip(where_, sols):
            u_scr[ch, rows, :] = sol[:, :HEAD_DIM]
            wq_scr[ch, pl.ds(pl.multiple_of(c * 2 * C, 2 * C), C), :] = sol[:, HEAD_DIM:].astype(BF16)
        return carry

    lax.fori_loop(0, n // group, prep, 0)

    for hh in range(hb):
        for d in range(2):
            s_scr[hh * 2 + d] = s0_ref[0, d, hh] if has_s0 else jnp.zeros((HEAD_DIM, HEAD_DIM), F32)

    T = n * C
    tr = min(T, 256)

    def clear(b, carry):
        o_scr[:, pl.ds(pl.multiple_of(b * tr, tr), tr), :] = jnp.zeros((hb, tr, HEAD_DIM), F32)
        return carry

    lax.fori_loop(0, T // tr, clear, 0)

    def scan(it, carry):
        chains = [(hh, d) for hh in range(hb) for d in range(2)]
        cs = [it if d == 0 else n - 1 - it for _, d in chains]
        rows = [pl.ds(pl.multiple_of(c * C, C), C) for c in cs]
        ss = [s_scr[hh * 2 + d] for hh, d in chains]
        rs = [jnp.dot(wq_scr[hh * 2 + d, pl.ds(pl.multiple_of(c * 2 * C, 2 * C), 2 * C), :], s.astype(BF16),
                      preferred_element_type=F32) for (hh, d), c, s in zip(chains, cs, ss)]
        vns = [(u_scr[hh * 2 + d, rw, :] - r[:C]).astype(BF16) for (hh, d), rw, r in zip(chains, rows, rs)]
        for i, (hh, d) in enumerate(chains):
            _, gcol, gl = gate_cols(hh, d, rows[i])
            kt = (k_ref[0, rows[i], hh * HEAD_DIM:(hh + 1) * HEAD_DIM].astype(F32) * jnp.exp(gl - gcol)).astype(BF16)
            s_scr[hh * 2 + d] = (ss[i] * jnp.exp(gl)
                                 + lax.dot_general(kt, vns[i], TN_DIMS, preferred_element_type=F32))
        for i, (hh, d) in enumerate(chains):
            o = rs[i][C:] + jnp.dot(qk_scr[hh * 2 + d, rows[i], :], vns[i], preferred_element_type=F32)
            o_scr[hh, rows[i], :] += o
        return carry

    lax.fori_loop(0, n, scan, 0)

    for hh in range(hb):
        for d in range(2):
            sfin_ref[0, d, hh] = s_scr[hh * 2 + d]

    def gate(b, carry):
        rows = pl.ds(pl.multiple_of(b * tr, tr), tr)
        for hh in range(hb):
            ls = slice(hh * HEAD_DIM, (hh + 1) * HEAD_DIM)
            o = o_scr[hh, rows, :]
            o = o * lax.rsqrt(jnp.mean(o * o, axis=-1, keepdims=True) + RMS_EPS) * ng_ref[...]
            o_ref[0, rows, ls] = (o * _silu(z_ref[0, rows, ls].astype(F32))).astype(BF16)
        return carry

    lax.fori_loop(0, T // tr, gate, 0)


def _delta(qkv, pm, cols, gct, norm_g, s0, *, n_batch, batch_off, T, hb):
    M, cw = qkv.shape
    H = cw // (3 * HEAD_DIM)
    n = T // DN_CHUNK
    W = hb * HEAD_DIM
    nb = H // hb
    qkv3 = qkv.reshape(M // T, T, cw)
    pm3 = pm.reshape(M // T, T, pm.shape[1])
    once = pl.Buffered(1)
    blk = lambda off: pl.BlockSpec((1, T, W), lambda b, j, off=off: (b + batch_off, 0, off * nb + j),
                                   pipeline_mode=once)
    in_specs = [blk(0), blk(1), blk(2), blk(3),
                pl.BlockSpec((1, 1, T, 4 * hb), lambda b, j: (b, j, 0, 0), pipeline_mode=once),
                pl.BlockSpec((1, hb, 2, n, DN_CHUNK), lambda b, j: (b, j, 0, 0, 0)),
                pl.BlockSpec((1, HEAD_DIM), lambda b, j: (0, 0))]
    args = [qkv3, qkv3, qkv3, pm3, cols, gct, norm_g.reshape(1, HEAD_DIM)]
    if s0 is not None:
        in_specs.append(pl.BlockSpec((1, 2, hb, HEAD_DIM, HEAD_DIM), lambda b, j: (b, 0, j, 0, 0)))
        args.append(s0)
    nch = 2 * hb
    group = math.gcd(n, max(1, 16 // nch))
    out, sfin = pl.pallas_call(
        functools.partial(_delta_kernel, hb, n, group, s0 is not None),
        grid=(n_batch, nb),
        in_specs=in_specs,
        out_specs=[pl.BlockSpec((1, T, W), lambda b, j: (b, 0, j)),
                   pl.BlockSpec((1, 2, hb, HEAD_DIM, HEAD_DIM), lambda b, j: (b, 0, j, 0, 0))],
        out_shape=[jax.ShapeDtypeStruct((n_batch, T, H * HEAD_DIM), BF16),
                   jax.ShapeDtypeStruct((n_batch, 2, H, HEAD_DIM, HEAD_DIM), F32)],
        scratch_shapes=[pltpu.VMEM((nch, T, HEAD_DIM), F32),
                        pltpu.VMEM((nch, 2 * T, HEAD_DIM), BF16),
                        pltpu.VMEM((nch, T, DN_CHUNK), BF16),
                        pltpu.VMEM((nch, HEAD_DIM, HEAD_DIM), F32),
                        pltpu.VMEM((hb, T, HEAD_DIM), F32)],
        compiler_params=_params("parallel", "parallel"),
        name="delta_rule",
    )(*args)
    return out.reshape(n_batch * T, H * HEAD_DIM), sfin


def _delta_side_inputs(bg, t0, n_batch, T, H, hb):
    b5 = bg[t0:t0 + n_batch * T, :4 * H].reshape(n_batch, T, 2, 2, H)
    cols = jnp.transpose(b5, (0, 4, 1, 2, 3)).reshape(n_batch, H // hb, hb, T, 4)
    cols = jnp.transpose(cols, (0, 1, 3, 2, 4)).reshape(n_batch, H // hb, T, 4 * hb)
    gct = jnp.transpose(b5[:, :, 1], (0, 3, 2, 1)).reshape(n_batch, H, 2, T // DN_CHUNK, DN_CHUNK)
    return cols, gct


def _hy_pre_kernel(D, hp_ref, hn_ref, x_ref, xp_ref, xn_ref, cw_ref, x0_ref, s_ref):
    i = pl.program_id(0)
    hp = hp_ref[i].astype(F32)
    hn = hn_ref[i].astype(F32)
    tT = x_ref.shape[0]
    W = min(D, 2 * LANES)
    rid = lax.broadcasted_iota(I32, (tT, W), 0)

    def conv(cb, part):
        sl = slice(part * D + cb * W, part * D + (cb + 1) * W)
        x = x_ref[:, sl].astype(F32)
        pr = xp_ref[:, sl].astype(F32)[BF16_ROWS - 1:BF16_ROWS] * hp
        nx = xn_ref[:, sl].astype(F32)[0:1] * hn
        return _conv3(x, pr, nx, cw_ref[:, sl], rid)

    for cb in range(D // W):
        sl = slice(cb * W, (cb + 1) * W)
        x0_ref[:, sl] = conv(cb, 0).astype(BF16)
        s_ref[:, sl] = (conv(cb, 1) * conv(cb, 2)).astype(BF16)


def _hy_pre(u, conv_w, has_prev, has_next, tT):
    M, W3 = u.shape
    D = W3 // 3
    out = pl.BlockSpec((tT, D), lambda i, hp, hn: (i, 0))
    return pl.pallas_call(
        functools.partial(_hy_pre_kernel, D),
        grid_spec=pltpu.PrefetchScalarGridSpec(
            num_scalar_prefetch=2, grid=(M // tT,),
            in_specs=_halo_specs(tT, W3, 0) + [pl.BlockSpec((3, W3), lambda i, hp, hn: (0, 0))],
            out_specs=[out, out]),
        out_shape=[jax.ShapeDtypeStruct((M, D), BF16)] * 2,
        compiler_params=_params("parallel"),
        name="hy_pre",
    )(has_prev, has_next, u, u, u, conv_w)


def _bmm_kernel(a_ref, b_ref, o_ref, acc_ref):
    k = pl.program_id(3)
    part = jnp.dot(a_ref[...], b_ref[0], preferred_element_type=F32)

    @pl.when(k == 0)
    def _():
        acc_ref[...] = part

    @pl.when(k > 0)
    def _():
        acc_ref[...] += part

    @pl.when(k == pl.num_programs(3) - 1)
    def _():
        o_ref[0] = acc_ref[...].astype(o_ref.dtype)


def _bmm(a, b, b_batch_off=0, n_batch=None, out_dtype=BF16, tm=1024, tn=1024, tk=4096):
    M, K = a.shape
    N = b.shape[2]
    nb = b.shape[0] if n_batch is None else n_batch
    tm, tn, tk = _tile(M, tm), _tile(N, tn), _tile(K, tk)
    return pl.pallas_call(
        _bmm_kernel,
        grid=(nb, M // tm, N // tn, K // tk),
        in_specs=[pl.BlockSpec((tm, tk), lambda i, m, n, k: (m, k)),
                  pl.BlockSpec((1, tk, tn), lambda i, m, n, k: (i + b_batch_off, k, n))],
        out_specs=pl.BlockSpec((1, tm, tn), lambda i, m, n, k: (i, m, n)),
        out_shape=jax.ShapeDtypeStruct((nb, M, N), out_dtype),
        scratch_shapes=[pltpu.VMEM((tm, tn), F32)],
        compiler_params=_params("parallel", "parallel", "parallel", "arbitrary"),
        name="dft_mm",
    )(a, b)


def _spec_prod_kernel(inv_n, s_ref, k_ref, y_ref):
    sc, ss = s_ref[0, 0].astype(F32), s_ref[0, 1].astype(F32)
    kc, ks = k_ref[0], k_ref[1]
    first = (lax.broadcasted_iota(I32, sc.shape, 0) == 0) & (pl.program_id(1) == 0)
    y_ref[0, 0] = (jnp.where(first, sc * kc, 2.0 * (sc * kc - ss * ks)) * inv_n).astype(y_ref.dtype)
    y_ref[0, 1] = (jnp.where(first, ss * ks, 2.0 * (sc * ks + ss * kc)) * inv_n).astype(y_ref.dtype)


def _spec_prod(sf, kf):
    B, _, T, D = sf.shape
    tr, tc = _tile(T, 256), _tile(D, 1024)
    return pl.pallas_call(
        functools.partial(_spec_prod_kernel, 1.0 / (2 * T)),
        grid=(B, T // tr, D // tc),
        in_specs=[pl.BlockSpec((1, 2, tr, tc), lambda b, i, j: (b, 0, i, j)),
                  pl.BlockSpec((2, tr, tc), lambda b, i, j: (0, i, j))],
        out_specs=pl.BlockSpec((1, 2, tr, tc), lambda b, i, j: (b, 0, i, j)),
        out_shape=jax.ShapeDtypeStruct(sf.shape, BF16),
        compiler_params=_params("parallel", "parallel", "parallel"),
        name="spec_prod",
    )(sf, kf)


def _dft_matrix(T):
    k = jnp.arange(T, dtype=I32)[:, None]
    t = jnp.arange(T, dtype=I32)[None, :]
    step = math.gcd(T, 64)
    th = jnp.arange(T // step, dtype=I32)[None, :] * step
    tl = jnp.arange(step, dtype=I32)[None, :]
    ang_h = ((k * th) % (2 * T)).astype(F32) * (math.pi / T)
    ang_l = ((k * tl) % (2 * T)).astype(F32) * (math.pi / T)
    ch, sh = jnp.cos(ang_h)[:, :, None], jnp.sin(ang_h)[:, :, None]
    cl, sl = jnp.cos(ang_l)[:, None, :], jnp.sin(ang_l)[:, None, :]
    cos = (ch * cl - sh * sl).reshape(T, T)
    sin = (sh * cl + ch * sl).reshape(T, T)
    nyq = jnp.where(t % 2 == 0, 1.0, -1.0).astype(F32)
    sin = jnp.where(k == 0, nyq, sin)
    return jnp.concatenate([cos, sin], axis=0).astype(BF16)


def _hyena_filter_taps(L, w1, b1, w2, b2, w3, b3, w4, freq):
    D = w4.shape[1] // 2
    n_bands = (w1.shape[0] - 1) // 2
    pos = jnp.arange(L, dtype=F32)
    t = pos / max(L - 1, 1)
    bands = jnp.linspace(1e-4, n_bands - 1, n_bands, dtype=F32)
    ang = (2.0 * math.pi / L) * pos[:, None] * bands[None]
    feats = jnp.concatenate([t[:, None], jnp.cos(ang), -jnp.sin(ang)], axis=-1)
    zf = jnp.sin(freq[0] * (feats @ w1 + b1))
    zf = jnp.sin(freq[1] * (zf @ w2 + b2))
    zf = jnp.sin(freq[2] * (zf @ w3 + b3))
    filt = (zf @ w4).reshape(L, 2, D)
    deltas = jnp.abs(jnp.linspace(math.log(HY_TARGET) / HY_SLOW, math.log(HY_TARGET) / HY_FAST, D, dtype=F32))
    filt = filt * jnp.exp(-t[:, None, None] * deltas[None, None])
    f = filt[:, 0]
    b = filt[:, 1].at[0].set(0.0)
    scale = lax.rsqrt(jnp.sum(f * f, axis=0) + jnp.sum(b * b, axis=0) + 1e-6)
    return f * scale, b * scale


def _hyena_conv(s, n_batch, t0, T, taps):
    M, D = s.shape
    f, b = taps
    fwd = _dft_matrix(T)
    kf = _bmm(fwd, jnp.concatenate([f, b], axis=1).astype(BF16)[None], out_dtype=F32)[0]
    p, q = kf[:, :D].reshape(2, T, D), kf[:, D:].reshape(2, T, D)
    first = (jnp.arange(T) == 0)[:, None]
    kspec = jnp.stack([p[0] + q[0], jnp.where(first, p[1] + q[1], p[1] - q[1])])
    s3 = s.reshape(M // T, T, D)
    sf = _bmm(fwd, s3, b_batch_off=t0 // T, n_batch=n_batch)
    y = _spec_prod(sf.reshape(n_batch, 2, T, D), kspec).reshape(n_batch, 2 * T, D)
    conv = _bmm(fwd.T, y)
    return conv.reshape(n_batch * T, D)


def _store_token_major(ref, x):
    t, d = x.shape
    nb = d // LANES
    for j in range(nb):
        ref[pl.ds(j, t, stride=nb), :] = x[:, j * LANES:(j + 1) * LANES]


def _load_token_major(ref, t, j, lead=()):
    nb = ref.shape[-2] // t
    return ref[lead + (pl.ds(j, t, stride=nb), slice(None))]


def _router_kernel(n_experts, rm_ref, x_ref, g_ref, sh_ref, sc_ref, wr_ref, br_ref,
                   h_ref, ti_ref, tg_ref, rk_ref, cnt_ref, base_scr):
    del rm_ref

    @pl.when(pl.program_id(0) == 0)
    def _():
        base_scr[...] = jnp.zeros(base_scr.shape, F32)

    h = _norm_mod_prologue([x_ref[...]], [g_ref[...]], (sh_ref[0, 0], sc_ref[0, 0]))
    _store_token_major(h_ref, h)
    logits = jnp.dot(h, wr_ref[...], precision=HIGHEST, preferred_element_type=F32) + br_ref[...]
    tm = logits.shape[0]
    lane = lax.broadcasted_iota(I32, logits.shape, 1)
    lane_f = lane.astype(F32)
    neg = jnp.float32(-jnp.inf)
    l = jnp.where(lane < n_experts, logits, neg)
    ti = jnp.zeros(logits.shape, I32)
    tl = jnp.full(logits.shape, neg, F32)
    onehot = jnp.zeros(logits.shape, F32)
    picks = []
    for r in range(TOP_K):
        m = jnp.max(l, axis=-1, keepdims=True)
        idx = jnp.min(jnp.where(l == m, lane_f, float(LANES)), axis=-1, keepdims=True).astype(I32)
        ti = jnp.where(lane == r, idx, ti)
        tl = jnp.where(lane == r, m, tl)
        l = jnp.where(lane == idx, neg, l)
        onehot = jnp.where(lane == idx, 1.0, onehot)
        picks.append(idx)
    e = jnp.exp(tl - jnp.max(tl, axis=-1, keepdims=True))
    ti_ref[...] = ti
    tg_ref[...] = e / jnp.sum(e, axis=-1, keepdims=True)

    r_i = lax.broadcasted_iota(I32, (tm, tm), 0)
    c_i = lax.broadcasted_iota(I32, (tm, tm), 1)
    earlier = jnp.where(c_i < r_i, 1.0, 0.0).astype(BF16)
    before = jnp.dot(earlier, onehot.astype(BF16), preferred_element_type=F32) + base_scr[...]
    rk = jnp.zeros(logits.shape, F32)
    for r in range(TOP_K):
        mine = jnp.sum(jnp.where(lane == picks[r], before, 0.0), axis=-1, keepdims=True)
        rk = jnp.where(lane == r, mine, rk)
    rk_ref[...] = rk.astype(I32)
    base_scr[...] += jnp.sum(onehot, axis=0, keepdims=True)
    cnt_ref[...] = base_scr[...].astype(I32)


def _router(x, g, mod4, rowmap, w_router, b_router, tm):
    M, D = x.shape
    E = w_router.shape[1]
    wr = jnp.zeros((D, LANES), F32).at[:, :E].set(w_router)
    br = jnp.zeros((1, LANES), F32).at[0, :E].set(b_router)
    row = lambda c: pl.BlockSpec((1, 1, 1, D), lambda i, rm, c=c: (rm[i], c, 0, 0))
    tile = lambda w: pl.BlockSpec((tm, w), lambda i, rm: (i, 0))
    return pl.pallas_call(
        functools.partial(_router_kernel, E),
        grid_spec=pltpu.PrefetchScalarGridSpec(
            num_scalar_prefetch=1, grid=(M // tm,),
            in_specs=[tile(D), pl.BlockSpec((1, D), lambda i, rm: (0, 0)), row(3), row(4),
                      pl.BlockSpec((D, LANES), lambda i, rm: (0, 0)),
                      pl.BlockSpec((1, LANES), lambda i, rm: (0, 0))],
            out_specs=[pl.BlockSpec((tm * D // LANES, LANES), lambda i, rm: (i, 0)),
                       tile(LANES), tile(LANES), tile(LANES),
                       pl.BlockSpec((1, LANES), lambda i, rm: (0, 0))],
            scratch_shapes=[pltpu.VMEM((1, LANES), F32)]),
        out_shape=[jax.ShapeDtypeStruct((M * D // LANES, LANES), F32), jax.ShapeDtypeStruct((M, LANES), I32),
                   jax.ShapeDtypeStruct((M, LANES), F32), jax.ShapeDtypeStruct((M, LANES), I32),
                   jax.ShapeDtypeStruct((1, LANES), I32)],
        compiler_params=_params("arbitrary"),
        name="router",
    )(rowmap, x, g.reshape(1, D), mod4, mod4, wr, br)


def _dest_kernel(ps_ref, ti_ref, rk_ref, o_ref):
    ti = ti_ref[...]
    start = jnp.zeros(ti.shape, I32)
    for e in range(ps_ref.shape[0]):
        start = jnp.where(ti == e, ps_ref[e], start)
    o_ref[...] = start + rk_ref[...]


def _dest_rows(seg_start, ti, rk, tm):
    M = ti.shape[0]
    tile = pl.BlockSpec((tm, LANES), lambda i, ps: (i, 0))
    return pl.pallas_call(
        _dest_kernel,
        grid_spec=pltpu.PrefetchScalarGridSpec(num_scalar_prefetch=1, grid=(M // tm,),
                                               in_specs=[tile, tile], out_specs=tile),
        out_shape=jax.ShapeDtypeStruct((M, LANES), I32),
        compiler_params=_params("parallel"),
        name="moe_dest",
    )(seg_start, ti, rk)


def _load_tile_dest(dest_ref, dest_smem, sem):
    cp = pltpu.make_async_copy(dest_ref, dest_smem, sem)
    cp.start()
    cp.wait()


def _zero_pad_rows(nb, pad_start_ref, pad_len_ref, o_ref, zero_ref, sem, start):
    def copy(pos, p):
        cp = pltpu.make_async_copy(zero_ref.at[pl.ds(0, p * nb)],
                                   o_ref.at[pl.ds(pl.multiple_of(pos * nb, nb), p * nb)], sem)
        if start:
            cp.start()
        else:
            cp.wait()

    def per_expert(e, carry):
        pos, length = pad_start_ref[e], pad_len_ref[e]
        p = zero_ref.shape[0] // nb // 2
        while p >= 1:
            hit = jnp.bitwise_and(length, p) != 0

            @pl.when(hit)
            def _(pos=pos, p=p):
                copy(pos, p)

            pos = pos + jnp.where(hit, p, 0)
            p //= 2
        return carry

    lax.fori_loop(0, pad_start_ref.shape[0], per_expert, 0)


def _dispatch_kernel(nb, pad_start_ref, pad_len_ref, tail_ref, dest_ref, h_ref, o_ref, dest_smem, zero_ref, sem):
    tm = h_ref.shape[0] // nb
    tz = zero_ref.shape[0]

    @pl.when(pl.program_id(0) == 0)
    def _():
        zero_ref[...] = jnp.zeros(zero_ref.shape, F32)

        def tail_copy(t):
            return pltpu.make_async_copy(zero_ref, o_ref.at[pl.ds(pl.multiple_of(t * tz, tz), tz)], sem.at[2])

        def tail_start(t, carry):
            tail_copy(t).start()
            return carry

        def tail_wait(t, carry):
            tail_copy(t).wait()
            return carry

        _zero_pad_rows(nb, pad_start_ref, pad_len_ref, o_ref, zero_ref, sem.at[2], True)
        lax.fori_loop(tail_ref[0], tail_ref[1], tail_start, 0)
        _zero_pad_rows(nb, pad_start_ref, pad_len_ref, o_ref, zero_ref, sem.at[2], False)
        lax.fori_loop(tail_ref[0], tail_ref[1], tail_wait, 0)

    _load_tile_dest(dest_ref, dest_smem, sem.at[0])

    def row_copy(j):
        dst = dest_smem[jnp.right_shift(j, LOG2_LANES), jnp.bitwise_and(j, LANES - 1)]
        return pltpu.make_async_copy(h_ref.at[pl.ds(pl.multiple_of(jnp.right_shift(j, LOG2_TOP_K) * nb, nb), nb)],
                                     o_ref.at[pl.ds(pl.multiple_of(dst * nb, nb), nb)], sem.at[1])

    def start(g, carry):
        for q in range(DMA_BURST):
            row_copy(g * DMA_BURST + q).start(priority=q % 2)
        return carry

    def wait(j, carry):
        row_copy(j).wait()
        return carry

    lax.fori_loop(0, tm * TOP_K // DMA_BURST, start, 0)
    lax.fori_loop(0, tm * TOP_K, wait, 0, unroll=8)


def _dispatch(h, nb, dest2d, pad_start, pad_len, tail, n_rows, tm, tm_rows):
    M = h.shape[0] // nb
    nd = tm * TOP_K // LANES
    return pl.pallas_call(
        functools.partial(_dispatch_kernel, nb),
        grid_spec=pltpu.PrefetchScalarGridSpec(
            num_scalar_prefetch=3, grid=(M // tm,),
            in_specs=[pl.BlockSpec((nd, LANES), lambda i, ps, pn, tl: (i, 0)),
                      pl.BlockSpec((tm * nb, LANES), lambda i, ps, pn, tl: (i, 0))],
            out_specs=pl.BlockSpec(memory_space=pl.ANY),
            scratch_shapes=[pltpu.SMEM((nd, LANES), I32), pltpu.VMEM((tm_rows * nb, LANES), F32),
                            pltpu.SemaphoreType.DMA((3,))]),
        out_shape=jax.ShapeDtypeStruct((n_rows * nb, LANES), F32),
        compiler_params=_params("arbitrary"),
        name="moe_dispatch",
    )(pad_start, pad_len, tail, dest2d, h)


def _combine_kernel(final, rm_ref, dest_ref, y_ref, tg_ref, x_ref, gate_ref, ng_ref, *refs):
    del rm_ref
    n_out = 1 if final is None else 2
    o_ref = refs[0] if final is None else refs[:2]
    buf, dest_smem, sem = refs[n_out:]
    tm = x_ref.shape[0]
    _load_tile_dest(dest_ref, dest_smem, sem.at[0])

    nb = x_ref.shape[1] // LANES

    def row_copy(j):
        src = dest_smem[jnp.right_shift(j, LOG2_LANES), jnp.bitwise_and(j, LANES - 1)]
        return pltpu.make_async_copy(
            y_ref.at[pl.ds(pl.multiple_of(src * nb, nb), nb)],
            buf.at[jnp.bitwise_and(j, TOP_K - 1), pl.ds(pl.multiple_of(jnp.right_shift(j, LOG2_TOP_K) * nb, nb), nb)],
            sem.at[1])

    def start(g, carry):
        for q in range(DMA_BURST):
            row_copy(g * DMA_BURST + q).start(priority=q % 2)
        return carry

    def wait(j, carry):
        row_copy(j).wait()
        return carry

    lax.fori_loop(0, tm * TOP_K // DMA_BURST, start, 0)
    lax.fori_loop(0, tm * TOP_K, wait, 0, unroll=8)
    tg = tg_ref[...]
    blocks = []
    for j in range(nb):
        acc = tg[:, 0:1] * _load_token_major(buf, tm, j, (0,))
        for k in range(1, TOP_K):
            acc = acc + tg[:, k:k + 1] * _load_token_major(buf, tm, j, (k,))
        blocks.append(acc)
    x = x_ref[...] + gate_ref[0, 0] * jnp.concatenate(blocks, axis=1)
    if final is None:
        o_ref[...] = x
    else:
        x = x * lax.rsqrt(jnp.mean(x * x, axis=-1, keepdims=True) + RMS_EPS) * ng_ref[...]
        first_ref, second_ref = o_ref

        @pl.when(pl.program_id(0) < final)
        def _():
            first_ref[...] = x

        @pl.when(pl.program_id(0) >= final)
        def _():
            second_ref[...] = x


def _combine(y_rows, dest2d, tg, x, mod4, rowmap, norm_g, split_rows, tm):
    M, D = x.shape
    nd = tm * TOP_K // LANES
    if split_rows is None:
        final = None
        out_specs = pl.BlockSpec((tm, D), lambda i, rm: (i, 0))
        out_shape = jax.ShapeDtypeStruct((M, D), F32)
    else:
        final = split_rows // tm
        out_specs = [pl.BlockSpec((tm, D), lambda i, rm: (jnp.minimum(i, final - 1), 0)),
                     pl.BlockSpec((tm, D), lambda i, rm: (jnp.maximum(i - final, 0), 0))]
        out_shape = [jax.ShapeDtypeStruct((split_rows, D), F32), jax.ShapeDtypeStruct((M - split_rows, D), F32)]
    return _combine_call(final, out_specs, out_shape, y_rows, dest2d, tg, x, mod4, rowmap, norm_g, tm, nd)


def _combine_call(final, out_specs, out_shape, y_rows, dest2d, tg, x, mod4, rowmap, norm_g, tm, nd):
    M, D = x.shape
    return pl.pallas_call(
        functools.partial(_combine_kernel, final),
        grid_spec=pltpu.PrefetchScalarGridSpec(
            num_scalar_prefetch=1, grid=(M // tm,),
            in_specs=[pl.BlockSpec((nd, LANES), lambda i, rm: (i, 0)),
                      pl.BlockSpec(memory_space=pl.ANY),
                      pl.BlockSpec((tm, LANES), lambda i, rm: (i, 0)),
                      pl.BlockSpec((tm, D), lambda i, rm: (i, 0)),
                      pl.BlockSpec((1, 1, 1, D), lambda i, rm: (rm[i], 5, 0, 0)),
                      pl.BlockSpec((1, D), lambda i, rm: (0, 0))],
            out_specs=out_specs,
            scratch_shapes=[pltpu.VMEM((TOP_K, tm * D // LANES, LANES), F32), pltpu.SMEM((nd, LANES), I32),
                            pltpu.SemaphoreType.DMA((2,))]),
        out_shape=out_shape,
        compiler_params=_params("arbitrary"),
        name="moe_combine",
    )(rowmap, dest2d, y_rows, tg, x, mod4, norm_g.reshape(1, D))


def _moe_kernel(nf, te_ref, nv_ref, x_ref, wg_ref, wu_ref, bg_ref, bu_ref, wo_ref, bo_ref, o_ref,
                acc_ref, xb_ref):
    del te_ref
    t, f = pl.program_id(0), pl.program_id(1)
    last = nf - 1
    valid = t < nv_ref[0]

    tm, d = xb_ref.shape
    nb = d // LANES

    @pl.when(valid & (f == 0))
    def _():
        for j in range(nb):
            xb_ref[:, j * LANES:(j + 1) * LANES] = _load_token_major(x_ref, tm, j).astype(BF16)

    @pl.when(valid)
    def _():
        x = xb_ref[...]
        g = jnp.dot(x, wg_ref[0], preferred_element_type=F32) + bg_ref[0]
        u = jnp.dot(x, wu_ref[0], preferred_element_type=F32) + bu_ref[0]
        g = jnp.minimum(g, SWIGLU_LIMIT)
        u = jnp.clip(u, -SWIGLU_LIMIT, SWIGLU_LIMIT)
        h = ((u + 1.0) * g * jax.nn.sigmoid(SWIGLU_ALPHA * g)).astype(BF16)
        part = jnp.dot(h, wo_ref[0], preferred_element_type=F32)

        if nf == 1:
            _store_token_major(o_ref, part + bo_ref[0])
        else:
            @pl.when(f == 0)
            def _():
                acc_ref[...] = part + bo_ref[0]

            @pl.when((f > 0) & (f < last))
            def _():
                acc_ref[...] += part

            @pl.when(f == last)
            def _():
                _store_token_major(o_ref, acc_ref[...] + part)

    @pl.when(jnp.logical_not(valid) & (f == last))
    def _():
        o_ref[...] = jnp.zeros(o_ref.shape, o_ref.dtype)


def _moe_experts(x_rows, tile_expert, n_valid, w_in, b_in, w_out, b_out, tm, tf):
    E, D, F2 = w_in.shape
    nb = D // LANES
    R = x_rows.shape[0] // nb
    F = F2 // 2
    tf = _tile(F, tf)
    nf = F // tf

    def fi(t, f, nv):
        return jnp.where(t < nv[0], f, nf - 1)

    def ti(t, nv):
        return jnp.minimum(t, jnp.maximum(nv[0] - 1, 0))

    return pl.pallas_call(
        functools.partial(_moe_kernel, nf),
        grid_spec=pltpu.PrefetchScalarGridSpec(
            num_scalar_prefetch=2, grid=(R // tm, nf),
            in_specs=[pl.BlockSpec((tm * nb, LANES), lambda t, f, te, nv: (ti(t, nv), 0)),
                      pl.BlockSpec((1, D, tf), lambda t, f, te, nv: (te[t], 0, fi(t, f, nv))),
                      pl.BlockSpec((1, D, tf), lambda t, f, te, nv: (te[t], 0, nf + fi(t, f, nv))),
                      pl.BlockSpec((1, 1, tf), lambda t, f, te, nv: (te[t], 0, fi(t, f, nv))),
                      pl.BlockSpec((1, 1, tf), lambda t, f, te, nv: (te[t], 0, nf + fi(t, f, nv))),
                      pl.BlockSpec((1, tf, D), lambda t, f, te, nv: (te[t], fi(t, f, nv), 0)),
                      pl.BlockSpec((1, 1, D), lambda t, f, te, nv: (te[t], 0, 0))],
            out_specs=pl.BlockSpec((tm * nb, LANES), lambda t, f, te, nv: (t, 0)),
            scratch_shapes=[pltpu.VMEM((tm, D), F32), pltpu.VMEM((tm, D), BF16)]),
        out_shape=jax.ShapeDtypeStruct((R * nb, LANES), F32),
        compiler_params=_params("parallel", "arbitrary"),
        name="moe_experts",
    )(tile_expert, n_valid, x_rows, w_in, w_in, b_in.reshape(E, 1, F2), b_in.reshape(E, 1, F2),
      w_out, b_out.reshape(E, 1, D))


def _moe_layer(x, norm_g, mod4, rowmap, tm_tok, w_router, b_router, w_in, b_in, w_out, b_out, first_expert,
               final_g=None, split_rows=None, tm=512, tf=1024):
    M, D = x.shape
    E = w_router.shape[1]
    h, ti, tg, rk, cnt = _router(x, norm_g, mod4, rowmap, w_router, b_router, tm_tok)
    A = M * TOP_K
    n_tiles = -(-A // tm) + E
    counts = cnt[0, :E]
    ptiles = (counts + tm - 1) // tm
    pend = jnp.cumsum(ptiles)
    pstart = pend - ptiles
    n_valid = pend[-1]
    tix = jnp.minimum(jnp.arange(n_tiles, dtype=I32), jnp.maximum(n_valid - 1, 0))
    tile_expert = jnp.minimum(jnp.searchsorted(pend, tix, side='right'), E - 1).astype(I32)
    dest = _dest_rows((pstart * tm).astype(I32), ti, rk, tm_tok)
    dest2d = dest[:, :TOP_K].reshape(A // LANES, LANES)
    tail = jnp.stack([n_valid, jnp.asarray(n_tiles, n_valid.dtype)]).astype(I32)
    x_rows = _dispatch(h, D // LANES, dest2d, (pstart * tm + counts).astype(I32), (ptiles * tm - counts).astype(I32),
                       tail, n_tiles * tm, tm_tok, tm)
    y_rows = _moe_experts(x_rows, tile_expert + first_expert, n_valid.reshape(1).astype(I32),
                          w_in, b_in, w_out, b_out, tm, tf)
    return _combine(y_rows, dest2d, tg, x, mod4, rowmap, norm_g if final_g is None else final_g,
                    split_rows if final_g is not None else None, tm_tok)


def _grid_pos_embedding(T, D):
    rows = T // GRID_W
    row = jnp.repeat(jnp.arange(rows), GRID_W)
    col = jnp.tile(jnp.arange(GRID_W), rows)
    quarter = D // 4
    omega = 1.0 / (POS_BASE ** (jnp.arange(quarter, dtype=F32) / quarter))

    def axis_emb(p):
        ang = p.astype(F32)[:, None] * omega[None]
        return jnp.concatenate([jnp.sin(ang), jnp.cos(ang)], axis=-1)

    return jnp.concatenate([axis_emb(row), axis_emb(col)], axis=-1)


def kernel(x_prompt, x_sample, state_delta, c, c_ctx, w_mod, b_mod, norm_mix, norm_ffn, norm_final, dn_w_in, dn_conv, dn_a_log, dn_dt_bias, dn_norm, dn_w_out, hy_w_in, hy_conv, hy_w1, hy_b1, hy_w2, hy_b2, hy_w3, hy_b3, hy_w4, hy_freq, hy_bias, hy_w_out, moe_w_router, moe_b_router, moe_w_in, moe_b_in, moe_w_out, moe_b_out):
    Bp, Tp, D = x_prompt.shape
    Bs, Ts, _ = x_sample.shape
    depth = w_mod.shape[0]
    H = state_delta.shape[3]
    Mp, Ms = Bp * Tp, Bs * Ts
    M = Mp + Ms
    assert Mp % Ts == 0 and Ts % Tp == 0 and Tp % DN_CHUNK == 0, "token groups must tile each other"
    tT = Tp
    tm_big = _tile(math.gcd(Mp, Ts), 1024)
    tm_mid = _tile(math.gcd(Mp, Ts), 512)
    tm_tok = min(256, Tp)

    def rowmap_for(tm):
        tile_start = np.arange(M // tm) * tm
        return jnp.asarray(np.where(tile_start < Mp, 0, 1 + (tile_start - Mp) // Ts), I32)

    rm_big, rm_mid, rm_tok = rowmap_for(tm_big), rowmap_for(tm_mid), rowmap_for(tm_tok)

    xs = x_sample + _grid_pos_embedding(Ts, D)[None]
    x = jnp.concatenate([x_prompt.reshape(Mp, D), xs.reshape(Ms, D)], axis=0)
    conv_start = np.arange(M // tT) * tT
    seq_len = np.where(conv_start < Mp, Tp, Ts)
    seq_pos = np.where(conv_start < Mp, conv_start % Tp, (conv_start - Mp) % Ts)
    has_prev = jnp.asarray(seq_pos > 0, I32)
    has_next = jnp.asarray(seq_pos + tT < seq_len, I32)

    n_cond = 1 + Bs
    r_pad = -(-n_cond // 8) * 8
    cond = jnp.zeros((r_pad, D), F32).at[0].set(c_ctx).at[1:n_cond].set(c)
    mod = _modulation(cond, w_mod, b_mod)

    n_experts = moe_w_in.shape[1]
    w_in_all = moe_w_in.astype(BF16).reshape((depth * n_experts,) + moe_w_in.shape[2:])
    w_out_all = moe_w_out.astype(BF16).reshape((depth * n_experts,) + moe_w_out.shape[2:])
    b_in_all = moe_b_in.reshape(depth * n_experts, -1)
    b_out_all = moe_b_out.reshape(depth * n_experts, -1)

    new_states = []
    i_dn = i_hy = 0
    for l in range(depth):
        mod4 = mod[l].reshape(r_pad, 6, 1, D)
        if l % 2 == 0:
            i = i_dn
            i_dn += 1
            w_in = dn_w_in[i]
            n_main = 4 * H * HEAD_DIM
            w_ba = jnp.zeros((D, LANES), F32).at[:, :4 * H].set(w_in[:, n_main:]).astype(BF16)
            pm, ba = _fused_mm(_norm_mod_prologue, [x], [norm_mix[l]], [0, 1], w_in[:, :n_main].astype(BF16),
                               rm_big, mod4=mod4, side_w=w_ba, tm=tm_big, tn=2048)
            qkv, bg = _dn_act(pm, ba, dn_conv[i], dn_a_log[i], dn_dt_bias[i], has_prev, has_next, tT)
            hb = 2 if H % 2 == 0 else 1
            cols_p, gct_p = _delta_side_inputs(bg, 0, Bp, Tp, H, hb)
            cols_s, gct_s = _delta_side_inputs(bg, Mp, Bs, Ts, H, hb)
            og_p, s_fin = _delta(qkv, pm, cols_p, gct_p, dn_norm[i], None,
                                 n_batch=Bp, batch_off=0, T=Tp, hb=hb)
            og_s, _ = _delta(qkv, pm, cols_s, gct_s, dn_norm[i], state_delta[:, i].astype(F32),
                             n_batch=Bs, batch_off=Mp // Ts, T=Ts, hb=hb)
            new_states.append(s_fin.astype(state_delta.dtype))
            x = _fused_mm(functools.partial(_two_part_prologue, Mp // tm_big), [og_p], [], [],
                          dn_w_out[i].astype(BF16), rm_big, mod4=mod4, a_rest=og_s,
                          res=x, gate_chunk=2, out_dtype=F32, tm=tm_big, tn=1024)
        else:
            i = i_hy
            i_hy += 1
            u = _fused_mm(_norm_mod_prologue, [x], [norm_mix[l]], [0, 1], hy_w_in[i].astype(BF16),
                          rm_big, mod4=mod4, tm=tm_big, tn=2048)
            x0, s = _hy_pre(u, hy_conv[i], has_prev, has_next, tT)
            hy = (hy_w1[i], hy_b1[i], hy_w2[i], hy_b2[i], hy_w3[i], hy_b3[i], hy_w4[i], hy_freq[i])
            conv_p = _hyena_conv(s, Bp, 0, Tp, _hyena_filter_taps(Tp, *hy))
            conv_s = _hyena_conv(s, Bs, Mp, Ts, _hyena_filter_taps(Ts, *hy))
            x = _fused_mm(functools.partial(_hyena_gate_prologue, Mp // tm_mid), [conv_p, x0, s], [hy_bias[i]], [],
                          hy_w_out[i].astype(BF16), rm_mid, mod4=mod4, a_rest=conv_s,
                          res=x, gate_chunk=2, out_dtype=F32, tm=tm_mid, tn=1024)
        last = l == depth - 1
        x = _moe_layer(x, norm_ffn[l], mod4, rm_tok, tm_tok, moe_w_router[l], moe_b_router[l],
                       w_in_all, b_in_all, w_out_all, b_out_all, l * n_experts,
                       final_g=norm_final if last else None, split_rows=Mp)

    y_prompt, y_sample = x
    return y_prompt.reshape(Bp, Tp, D), y_sample.reshape(Bs, Ts, D), jnp.stack(new_states, axis=1)
```
